```python
import math
import jax
import jax.numpy as jnp
from jax import lax
import numpy as np

D_MODEL = 1024
BATCH = 8
SEQ = 4096
DEPTH = 2

GRID_W = 64
CTX_LEN = 256
HEAD_DIM = 64
A_Q_HEADS = 6
A_KV_HEADS = 2
A_Q_PER_KV = A_Q_HEADS // A_KV_HEADS
A_WINDOW = 128
A_BLOCK = 128
B_HEADS = 6
B_MAX_ROWS = 8
B_COLS = 16
C_HEADS = 4
C_QK_DIM = 32
C_V_DIM = 64
C_BLOCK = 128
MIX_WIDTH = A_Q_HEADS * HEAD_DIM + B_HEADS * HEAD_DIM + C_HEADS * C_V_DIM
IN_SECTIONS = (
    A_Q_HEADS * HEAD_DIM, A_KV_HEADS * HEAD_DIM, A_KV_HEADS * HEAD_DIM,
    B_HEADS * HEAD_DIM, B_HEADS * HEAD_DIM, B_HEADS * HEAD_DIM,
    C_HEADS * 2 * C_QK_DIM, C_HEADS * 2 * C_QK_DIM, C_HEADS * C_V_DIM,
)
IN_WIDTH = sum(IN_SECTIONS)
IN_OFFSETS = tuple(sum(IN_SECTIONS[: i + 1]) for i in range(len(IN_SECTIONS) - 1))
N_EXPERTS = 16
EXPERT_FF = 2 * D_MODEL
CAPACITY_FACTOR = 2
N_MOD = 6
ROPE_THETA = 10000.0
EPS = 1e-6
NEG_INF = -1e30

kernel_name = 'hybrid_diffusion_parallel_heads_ec_moe'


def _rms(x, g):
    xf = x.astype(jnp.float32)
    y = xf * lax.rsqrt(jnp.mean(xf * xf, axis=-1, keepdims=True) + EPS)
    return (y * g.astype(jnp.float32)).astype(x.dtype)


def _modulation(cvec, w, b):
    m = jnp.dot(jax.nn.silu(cvec), w) + b
    return jnp.split(m[..., None, :], N_MOD, axis=-1)


def _rope_1d(x, pos):
    d = x.shape[-1]
    inv = ROPE_THETA ** (-jnp.arange(d // 2, dtype=jnp.float32) / (d // 2))
    ang = pos.astype(jnp.float32)[:, None] * inv[None, :]
    cos, sin = jnp.cos(ang), jnp.sin(ang)
    xf = x.astype(jnp.float32)
    x1, x2 = xf[..., : d // 2], xf[..., d // 2:]
    return jnp.concatenate([x1 * cos - x2 * sin, x1 * sin + x2 * cos], axis=-1).astype(x.dtype)


def _rope_2d(x, rows, cols):
    d = x.shape[-1]
    return jnp.concatenate([_rope_1d(x[..., : d // 2], rows), _rope_1d(x[..., d // 2:], cols)], axis=-1)


def _softmax32(s):
    return jax.nn.softmax(s.astype(jnp.float32), axis=-1)


def _heads(t, n_heads, d):
    b, n = t.shape[:2]
    return t.reshape(b, n, n_heads, d).transpose(0, 2, 1, 3)


def _merge_heads(o):
    b, h, n, d = o.shape
    return o.transpose(0, 2, 1, 3).reshape(b, n, h * d)


def _project_groups(p):
    b, n = p.shape[:2]
    parts = jnp.split(p, IN_OFFSETS, axis=-1)
    a_q = parts[0].reshape(b, n, A_KV_HEADS, A_Q_PER_KV, HEAD_DIM).transpose(0, 2, 3, 1, 4)
    a_k = _heads(parts[1], A_KV_HEADS, HEAD_DIM)
    a_v = _heads(parts[2], A_KV_HEADS, HEAD_DIM)
    b_q = _heads(parts[3], B_HEADS, HEAD_DIM)
    b_k = _heads(parts[4], B_HEADS, HEAD_DIM)
    b_v = _heads(parts[5], B_HEADS, HEAD_DIM)
    c_q = parts[6].reshape(b, n, C_HEADS, 2, C_QK_DIM).transpose(0, 2, 3, 1, 4)
    c_k = parts[7].reshape(b, n, C_HEADS, 2, C_QK_DIM).transpose(0, 2, 3, 1, 4)
    c_v = _heads(parts[8], C_HEADS, C_V_DIM)
    return (a_q, a_k, a_v, b_q, b_k, b_v, c_q, c_k, c_v)


def _window_gqa_sink(q, k, v, qc, kc, vc, sink, rows, cols, need_ctx):
    b, g, r, s_len, hd = q.shape
    l_ctx = kc.shape[2]
    scale = hd ** -0.5
    q = _rope_2d(q, rows, cols)
    k = _rope_2d(k, rows, cols)
    nb = s_len // A_BLOCK
    pad = ((0, 0), (0, 0), (A_BLOCK, A_BLOCK), (0, 0))
    kp, vp = jnp.pad(k, pad), jnp.pad(v, pad)
    sink_l = sink.reshape(1, g, r, 1, 1).astype(jnp.float32)
    offs_q = jnp.arange(A_BLOCK)
    offs_k = jnp.arange(3 * A_BLOCK) - A_BLOCK
    band = 3 * A_BLOCK

    def block(n):
        s0 = n * A_BLOCK
        qb = lax.dynamic_slice_in_dim(q, s0, A_BLOCK, axis=3)
        kb = lax.dynamic_slice_in_dim(kp, s0, band, axis=2)
        vb = lax.dynamic_slice_in_dim(vp, s0, band, axis=2)
        qpos, kpos = s0 + offs_q, s0 + offs_k
        valid = (jnp.abs(qpos[:, None] - kpos[None, :]) <= A_WINDOW) & (kpos >= 0) & (kpos < s_len)
        s_lat = jnp.einsum('bgrqd,bgkd->bgrqk', qb, kb).astype(jnp.float32) * scale
        s_lat = jnp.where(valid, s_lat, NEG_INF)
        s_ctx = jnp.einsum('bgrqd,bgkd->bgrqk', qb, kc).astype(jnp.float32) * scale
        sk = jnp.broadcast_to(sink_l, s_lat.shape[:-1] + (1,))
        p = _softmax32(jnp.concatenate([s_lat, s_ctx, sk], axis=-1)).astype(v.dtype)
        return (jnp.einsum('bgrqk,bgkd->bgrqd', p[..., :band], vb)
                + jnp.einsum('bgrqk,bgkd->bgrqd', p[..., band:band + l_ctx], vc))

    out = lax.map(block, jnp.arange(nb))
    out = jnp.moveaxis(out, 0, 3).reshape(b, g * r, s_len, hd)
    oc = None
    if need_ctx:
        s_c = jnp.einsum('bgrqd,bgkd->bgrqk', qc, kc).astype(jnp.float32) * scale
        sk = jnp.broadcast_to(sink_l, s_c.shape[:-1] + (1,))
        p = _softmax32(jnp.concatenate([s_c, sk], axis=-1)).astype(vc.dtype)
        oc = jnp.einsum('bgrqk,bgkd->bgrqd', p[..., :l_ctx], vc).reshape(b, g * r, l_ctx, hd)
    return out, oc


def _neighbourhood_attn(q, k, v, qc, kc, vc, rpb, need_ctx):
    b, h, s_len, hd = q.shape
    l_ctx = kc.shape[2]
    rows_n = s_len // GRID_W
    kr = min(B_MAX_ROWS, rows_n)
    scale = hd ** -0.5
    qg = q.reshape(b, h, rows_n, GRID_W, hd)
    kg = k.reshape(b, h, rows_n, GRID_W, hd)
    vg = v.reshape(b, h, rows_n, GRID_W, hd)
    col = jnp.arange(GRID_W)
    cstart = jnp.clip(col - B_COLS // 2, 0, GRID_W - B_COLS)
    col_ok = (col[None, :] >= cstart[:, None]) & (col[None, :] < cstart[:, None] + B_COLS)
    col_ok = jnp.broadcast_to(col_ok[:, None, :], (GRID_W, kr, GRID_W)).reshape(GRID_W, kr * GRID_W)
    dc_idx = jnp.clip(col[None, :] - col[:, None], -(B_COLS - 1), B_COLS - 1) + (B_COLS - 1)
    nk = kr * GRID_W

    def row(r):
        rs = jnp.clip(r - kr // 2, 0, rows_n - kr)
        qr = lax.dynamic_index_in_dim(qg, r, axis=2, keepdims=False)
        kb = lax.dynamic_slice_in_dim(kg, rs, kr, axis=2).reshape(b, h, nk, hd)
        vb = lax.dynamic_slice_in_dim(vg, rs, kr, axis=2).reshape(b, h, nk, hd)
        dr_idx = rs + jnp.arange(kr) - r + (B_MAX_ROWS - 1)
        bias = rpb[:, dr_idx[None, :, None], dc_idx[:, None, :]].reshape(h, GRID_W, nk)
        s_lat = jnp.einsum('bhqd,bhkd->bhqk', qr, kb).astype(jnp.float32) * scale + bias.astype(jnp.float32)
        s_lat = jnp.where(col_ok, s_lat, NEG_INF)
        s_ctx = jnp.einsum('bhqd,bhkd->bhqk', qr, kc).astype(jnp.float32) * scale
        p = _softmax32(jnp.concatenate([s_lat, s_ctx], axis=-1)).astype(v.dtype)
        return (jnp.einsum('bhqk,bhkd->bhqd', p[..., :nk], vb)
                + jnp.einsum('bhqk,bhkd->bhqd', p[..., nk:], vc))

    out = lax.map(row, jnp.arange(rows_n))
    out = jnp.moveaxis(out, 0, 2).reshape(b, h, s_len, hd)
    oc = None
    if need_ctx:
        p = _softmax32(jnp.einsum('bhqd,bhkd->bhqk', qc, kc) * scale).astype(vc.dtype)
        oc = jnp.einsum('bhqk,bhkd->bhqd', p, vc)
    return out, oc


def _diff_attn(q, k, v, qc, kc, vc, lam, lam_init, subln, rows, cols, need_ctx):
    b, h, _, s_len, dq = q.shape
    dv = v.shape[-1]
    scale = dq ** -0.5
    q = _rope_2d(q, rows, cols)
    k = _rope_2d(k, rows, cols)
    k_all = jnp.concatenate([k, kc], axis=3)
    v_all = jnp.concatenate([v, vc], axis=2)

    def attend(qb, kk, vv):
        p = _softmax32(jnp.einsum('bhmqd,bhmkd->bhmqk', qb, kk) * scale)
        pd = (p[:, :, 0] - lam * p[:, :, 1]).astype(vv.dtype)
        o = jnp.einsum('bhqk,bhkd->bhqd', pd, vv)
        return _rms(o, subln) * (1.0 - lam_init)

    def block(n):
        qb = lax.dynamic_slice_in_dim(q, n * C_BLOCK, C_BLOCK, axis=3)
        return attend(qb, k_all, v_all)

    out = lax.map(block, jnp.arange(s_len // C_BLOCK))
    out = jnp.moveaxis(out, 0, 2).reshape(b, h, s_len, dv)
    oc = attend(qc, kc, vc) if need_ctx else None
    return out, oc


def _hybrid_mixer(px, pc, rows, cols, a_sink, b_rpb, lam, lam_init, c_subln, need_ctx):
    aq, ak, av, bq, bk, bv, cq, ck, cv = _project_groups(px)
    aqc, akc, avc, bqc, bkc, bvc, cqc, ckc, cvc = _project_groups(pc)
    oa, oac = _window_gqa_sink(aq, ak, av, aqc, akc, avc, a_sink, rows, cols, need_ctx)
    ob, obc = _neighbourhood_attn(bq, bk, bv, bqc, bkc, bvc, b_rpb, need_ctx)
    od, odc = _diff_attn(cq, ck, cv, cqc, ckc, cvc, lam, lam_init, c_subln, rows, cols, need_ctx)
    ox = _merge_heads(jnp.concatenate([oa, ob, od], axis=1))
    oc = _merge_heads(jnp.concatenate([oac, obc, odc], axis=1)) if need_ctx else None
    return ox, oc


def _expert_choice_moe(h, w_router, w_gate, w_up, w_down):
    b, n, _ = h.shape
    cap = n * CAPACITY_FACTOR // N_EXPERTS
    aff = _softmax32(jnp.einsum('bnd,de->bne', h, w_router))
    gates, idx = lax.top_k(jnp.swapaxes(aff, 1, 2), cap)
    b_idx = jnp.arange(b)[:, None, None]
    xs = h[b_idx, idx]
    hid = jax.nn.silu(jnp.einsum('becd,edf->becf', xs, w_gate)) * jnp.einsum('becd,edf->becf', xs, w_up)
    y = jnp.einsum('becf,efd->becd', hid, w_down) * gates[..., None].astype(h.dtype)
    return jnp.zeros_like(h).at[b_idx, idx].add(y)


def setup_inputs(seed: int = 0) -> dict:
    key = jax.random.key(seed)
    ks = jax.random.split(key, 24)
    f32 = jnp.float32
    d, nl = D_MODEL, DEPTH

    def nrm(k, shape, scale):
        return jax.random.normal(k, shape, f32) * scale

    return {
        'x': nrm(ks[0], (BATCH, SEQ, d), 1.0),
        'c': nrm(ks[1], (BATCH, d), 1.0),
        'ctx': nrm(ks[2], (BATCH, CTX_LEN, d), 1.0),
        'c_ctx': nrm(ks[3], (d,), 1.0),
        'w_ada': nrm(ks[4], (nl, d, N_MOD * d), 0.5 * d ** -0.5),
        'b_ada': nrm(ks[5], (nl, N_MOD * d), 0.02),
        'g_mix_pre': 1.0 + nrm(ks[6], (nl, d), 0.02),
        'g_mix_post': 1.0 + nrm(ks[7], (nl, d), 0.02),
        'g_ffn_pre': 1.0 + nrm(ks[8], (nl, d), 0.02),
        'g_ffn_post': 1.0 + nrm(ks[9], (nl, d), 0.02),
        'w_in': nrm(ks[10], (nl, d, IN_WIDTH), d ** -0.5),
        'w_out': nrm(ks[11], (nl, MIX_WIDTH, d), MIX_WIDTH ** -0.5),
        'a_sink': nrm(ks[12], (nl, A_Q_HEADS), 0.5),
        'b_rpb': nrm(ks[13], (nl, B_HEADS, 2 * B_MAX_ROWS - 1, 2 * B_COLS - 1), 0.1),
        'c_lam_q1': nrm(ks[14], (nl, C_QK_DIM), 0.1),
        'c_lam_k1': nrm(ks[15], (nl, C_QK_DIM), 0.1),
        'c_lam_q2': nrm(ks[16], (nl, C_QK_DIM), 0.1),
        'c_lam_k2': nrm(ks[17], (nl, C_QK_DIM), 0.1),
        'c_subln': 1.0 + nrm(ks[18], (nl, C_V_DIM), 0.02),
        'w_router': nrm(ks[19], (nl, d, N_EXPERTS), d ** -0.5),
        'w_gate': nrm(ks[20], (nl, N_EXPERTS, d, EXPERT_FF), d ** -0.5),
        'w_up': nrm(ks[21], (nl, N_EXPERTS, d, EXPERT_FF), d ** -0.5),
        'w_down': nrm(ks[22], (nl, N_EXPERTS, EXPERT_FF, d), EXPERT_FF ** -0.5),
    }


def reference(x, c, ctx, c_ctx, w_ada, b_ada, g_mix_pre, g_mix_post, g_ffn_pre, g_ffn_post,
              w_in, w_out, a_sink, b_rpb, c_lam_q1, c_lam_k1, c_lam_q2, c_lam_k2, c_subln,
              w_router, w_gate, w_up, w_down):
    s_len = x.shape[1]
    t = jnp.arange(s_len)
    rows, cols = t // GRID_W, t % GRID_W
    h_ctx = ctx
    for l in range(DEPTH):
        need_ctx = l < DEPTH - 1
        sh1, sc1, gt1, sh2, sc2, gt2 = _modulation(c, w_ada[l], b_ada[l])
        csh1, csc1, cgt1, csh2, csc2, cgt2 = _modulation(c_ctx, w_ada[l], b_ada[l])
        hx = _rms(x, g_mix_pre[l]) * (1.0 + sc1) + sh1
        hc = _rms(h_ctx, g_mix_pre[l]) * (1.0 + csc1) + csh1
        lam_init = 0.8 - 0.6 * math.exp(-0.3 * l)
        lam = (jnp.exp(jnp.sum(c_lam_q1[l].astype(jnp.float32) * c_lam_k1[l].astype(jnp.float32)))
               - jnp.exp(jnp.sum(c_lam_q2[l].astype(jnp.float32) * c_lam_k2[l].astype(jnp.float32)))
               + lam_init)
        ox, oc = _hybrid_mixer(hx @ w_in[l], hc @ w_in[l], rows, cols, a_sink[l], b_rpb[l],
                               lam, lam_init, c_subln[l], need_ctx)
        x = x + gt1 * _rms(ox @ w_out[l], g_mix_post[l])
        if need_ctx:
            h_ctx = h_ctx + cgt1 * _rms(oc @ w_out[l], g_mix_post[l])
            hc2 = _rms(h_ctx, g_ffn_pre[l]) * (1.0 + csc2) + csh2
            h_ctx = h_ctx + cgt2 * _rms(_expert_choice_moe(hc2, w_router[l], w_gate[l], w_up[l], w_down[l]),
                                        g_ffn_post[l])
        hx2 = _rms(x, g_ffn_pre[l]) * (1.0 + sc2) + sh2
        x = x + gt2 * _rms(_expert_choice_moe(hx2, w_router[l], w_gate[l], w_up[l], w_down[l]),
                           g_ffn_post[l])
    return x
```

```python
import functools
import math

import numpy as np
import jax
import jax.numpy as jnp
from jax import lax
from jax.experimental import pallas as pl
from jax.experimental.pallas import tpu as pltpu

F32 = jnp.float32
BF16 = jnp.bfloat16

GRID_W = 64
HEAD_DIM = 64
A_Q_HEADS, A_KV_HEADS = 6, 2
A_Q_PER_KV = A_Q_HEADS // A_KV_HEADS
A_WINDOW = 128
A_BLOCK = 128
B_HEADS, B_MAX_ROWS, B_COLS = 6, 8, 16
C_HEADS, C_QK_DIM, C_V_DIM = 4, 32, 64
N_EXPERTS = 16
CAPACITY_FACTOR = 2
N_MOD = 6
ROPE_THETA = 10000.0
EPS = 1e-6
NEG_INF = -1e30

LANES = 128
ROPE_W = 1024
OFF_AQ, OFF_AK, OFF_CQ, OFF_CK, OFF_AV, OFF_BQ, OFF_BK, OFF_BV, OFF_CV = (
    0, 384, 512, 768, 1024, 1152, 1536, 1920, 2304)
IN_W = 2560
VMEM_LIMIT = 56 * 1024 * 1024


def _cparams(sem):
    return pltpu.CompilerParams(dimension_semantics=sem, vmem_limit_bytes=VMEM_LIMIT)


def _dot(a, b):
    return jnp.dot(a, b, preferred_element_type=F32)


def _dot_nt(a, b):
    return lax.dot_general(a, b, (((1,), (1,)), ((), ())), preferred_element_type=F32)


def _rms(x, g):
    return x * lax.rsqrt(jnp.mean(x * x, axis=-1, keepdims=True) + EPS) * g


def _mod_kernel(c_ref, w_ref, b_ref, o_ref):
    c = c_ref[...]
    a = c / (1.0 + jnp.exp(-c))
    o_ref[0] = jnp.dot(a, w_ref[0], preferred_element_type=F32,
                       precision=lax.Precision.HIGHEST) + b_ref[0]


def _modulation(cstack, w_ada, b_ada):
    nl, d, n = w_ada.shape
    rows = cstack.shape[0]
    tn = 1536
    return pl.pallas_call(
        _mod_kernel,
        out_shape=jax.ShapeDtypeStruct((nl, rows, n), F32),
        grid=(nl, n // tn),
        in_specs=[
            pl.BlockSpec((rows, d), lambda l, j: (0, 0)),
            pl.BlockSpec((1, d, tn), lambda l, j: (l, 0, j)),
            pl.BlockSpec((1, 1, tn), lambda l, j: (l, 0, j)),
        ],
        out_specs=pl.BlockSpec((1, rows, tn), lambda l, j: (l, 0, j)),
        compiler_params=_cparams(("parallel", "parallel")),
        name="adaln_modulation",
    )(cstack, w_ada, b_ada.reshape(nl, 1, n))


_ROPE_TABLE_OF_UNIT = (0, 0, 0, 1, 2, 2, 3, 3)


def _inproj_kernel(x_ref, g_ref, sc_ref, sh_ref, cos_ref, sin_ref, w_ref, o_ref):
    x = x_ref[0]
    h = _rms(x, g_ref[...]) * (1.0 + sc_ref[0]) + sh_ref[0]
    hb = h.astype(BF16)
    for j in range(ROPE_W // 256):
        t0, t1 = _ROPE_TABLE_OF_UNIT[2 * j], _ROPE_TABLE_OF_UNIT[2 * j + 1]
        cos = jnp.concatenate([cos_ref[:, t0 * LANES:(t0 + 1) * LANES],
                               cos_ref[:, t1 * LANES:(t1 + 1) * LANES]], axis=1)
        sin = jnp.concatenate([sin_ref[:, t0 * LANES:(t0 + 1) * LANES],
                               sin_ref[:, t1 * LANES:(t1 + 1) * LANES]], axis=1)
        p = _dot(hb, w_ref[:, 256 * j:256 * (j + 1)])
        ps = _dot(hb, w_ref[:, IN_W + 256 * j:IN_W + 256 * (j + 1)])
        o_ref[0, :, 256 * j:256 * (j + 1)] = (p * cos + ps * sin).astype(BF16)
    for j in range(ROPE_W // 256, IN_W // 256):
        o_ref[0, :, 256 * j:256 * (j + 1)] = _dot(hb, w_ref[:, 256 * j:256 * (j + 1)]).astype(BF16)


def _inproj(x, g, sc, sh, cos, sin, w_cat, tm):
    b, s, d = x.shape
    per_sample = sc.shape[0] > 1
    mod_map = (lambda bi, i: (bi, 0, 0)) if per_sample else (lambda bi, i: (0, 0, 0))
    return pl.pallas_call(
        _inproj_kernel,
        out_shape=jax.ShapeDtypeStruct((b, s, IN_W), BF16),
        grid=(b, s // tm),
        in_specs=[
            pl.BlockSpec((1, tm, d), lambda bi, i: (bi, i, 0)),
            pl.BlockSpec((1, d), lambda bi, i: (0, 0)),
            pl.BlockSpec((1, 1, d), mod_map),
            pl.BlockSpec((1, 1, d), mod_map),
            pl.BlockSpec((tm, 4 * LANES), lambda bi, i: (i, 0)),
            pl.BlockSpec((tm, 4 * LANES), lambda bi, i: (i, 0)),
            pl.BlockSpec((d, IN_W + ROPE_W), lambda bi, i: (0, 0)),
        ],
        out_specs=pl.BlockSpec((1, tm, IN_W), lambda bi, i: (bi, i, 0)),
        compiler_params=_cparams(("parallel", "parallel")),
        name="norm_inproj_rope",
    )(x, g.reshape(1, d), sc, sh, cos, sin, w_cat)


def _lo_mask():
    return lax.broadcasted_iota(jnp.int32, (1, LANES), 1) < HEAD_DIM


def _split_halves(q):
    lo = _lo_mask()
    zero = jnp.zeros_like(q)
    return jnp.where(lo, q, zero), jnp.where(lo, zero, q)


def _attn_a_kernel(sink_ref, q_ref, *refs, seq, latent):
    if latent:
        k_ref, v_ref, kc_ref, vc_ref, o_ref = refs
    else:
        kc_ref, vc_ref, o_ref = refs
    tq = q_ref.shape[1]
    npair = A_Q_HEADS // 2
    los, his = [], []
    for t in range(npair):
        a, b = _split_halves(q_ref[0, :, t * LANES:(t + 1) * LANES])
        los.append(a)
        his.append(b)
    q6 = jnp.concatenate(los + his, axis=0)
    sink = jnp.concatenate([jnp.full((tq, 1), sink_ref[j], F32) for j in range(A_Q_HEADS)], axis=0)
    kc = kc_ref[0]
    vc = vc_ref[0]
    s_ctx = _dot_nt(q6, kc)
    m = jnp.maximum(jnp.max(s_ctx, axis=-1, keepdims=True), sink)
    if latent:
        i = pl.program_id(1)
        band = 3 * A_BLOCK
        start = pl.multiple_of(jnp.clip((i - 1) * A_BLOCK, 0, seq - band), A_BLOCK)
        kb = k_ref[0, pl.ds(start, band), :]
        vb = v_ref[0, pl.ds(start, band), :]
        s_lat = _dot_nt(q6, kb)
        row = lax.broadcasted_iota(jnp.int32, (A_Q_HEADS * tq, 1), 0)
        qpos = i * tq + (row & (tq - 1))
        kpos = start + lax.broadcasted_iota(jnp.int32, (1, band), 1)
        s_lat = jnp.where(jnp.abs(qpos - kpos) <= A_WINDOW, s_lat, NEG_INF)
        m = jnp.maximum(m, jnp.max(s_lat, axis=-1, keepdims=True))
        e_lat = jnp.exp(s_lat - m)
    e_ctx = jnp.exp(s_ctx - m)
    l = jnp.sum(e_ctx, axis=-1, keepdims=True) + jnp.exp(sink - m)
    o = _dot(e_ctx.astype(BF16), vc)
    if latent:
        l = l + jnp.sum(e_lat, axis=-1, keepdims=True)
        o = o + _dot(e_lat.astype(BF16), vb)
    o = o / l
    lo = _lo_mask()
    for t in range(npair):
        o_ref[0, :, t * LANES:(t + 1) * LANES] = jnp.where(
            lo, o[t * tq:(t + 1) * tq], o[(npair + t) * tq:(npair + t + 1) * tq]).astype(BF16)


def _attn_a(p, pc, sink, latent):
    src = p if latent else pc
    b, s, _ = src.shape
    lc = pc.shape[1]
    tq = A_BLOCK if latent else s
    wq = A_Q_HEADS * HEAD_DIM
    in_specs = [pl.BlockSpec(memory_space=pltpu.SMEM),
                pl.BlockSpec((1, tq, wq), lambda bi, i: (bi, i, OFF_AQ // wq))]
    args = [sink, src]
    if latent:
        in_specs += [pl.BlockSpec((1, s, LANES), lambda bi, i: (bi, 0, OFF_AK // LANES)),
                     pl.BlockSpec((1, s, LANES), lambda bi, i: (bi, 0, OFF_AV // LANES))]
        args += [p, p]
    in_specs += [pl.BlockSpec((1, lc, LANES), lambda bi, i: (bi, 0, OFF_AK // LANES)),
                 pl.BlockSpec((1, lc, LANES), lambda bi, i: (bi, 0, OFF_AV // LANES))]
    args += [pc, pc]
    return pl.pallas_call(
        functools.partial(_attn_a_kernel, seq=s, latent=latent),
        out_shape=jax.ShapeDtypeStruct((b, s, wq), BF16),
        grid=(b, s // tq),
        in_specs=in_specs,
        out_specs=pl.BlockSpec((1, tq, wq), lambda bi, i: (bi, i, 0)),
        compiler_params=_cparams(("parallel", "arbitrary")),
        name="attn_window_gqa" + ("" if latent else "_ctx"),
    )(*args)


B_ROWS_PER_STEP = 8


def _attn_b_softmax_out(q2, s_parts, v_parts):
    m = functools.reduce(jnp.maximum, [jnp.max(s, axis=-1, keepdims=True) for s in s_parts])
    es = [jnp.exp(s - m) for s in s_parts]
    l = functools.reduce(jnp.add, [jnp.sum(e, axis=-1, keepdims=True) for e in es])
    o = functools.reduce(jnp.add, [_dot(e.astype(BF16), v) for e, v in zip(es, v_parts)])
    return o / l


def _attn_b_kernel(q_ref, k_ref, v_ref, kc_ref, vc_ref, bias_ref, o_ref, *, rows_n):
    i = pl.program_id(1)
    npair = B_HEADS // 2
    lo = _lo_mask()
    nk = B_MAX_ROWS * GRID_W

    def body(rr, carry):
        r = i * B_ROWS_PER_STEP + rr
        rs = jnp.clip(r - B_MAX_ROWS // 2, 0, rows_n - B_MAX_ROWS)
        off = r - rs
        kstart = pl.multiple_of(rs * GRID_W, GRID_W)
        qstart = pl.multiple_of(rr * GRID_W, GRID_W)
        for t in range(npair):
            cols = slice(t * LANES, (t + 1) * LANES)
            qa, qb = _split_halves(q_ref[0, pl.ds(qstart, GRID_W), cols])
            q2 = jnp.concatenate([qa, qb], axis=0)
            s_lat = _dot_nt(q2, k_ref[0, pl.ds(kstart, nk), cols]) + bias_ref[off, t]
            s_ctx = _dot_nt(q2, kc_ref[0, :, cols])
            o = _attn_b_softmax_out(q2, [s_lat, s_ctx],
                                    [v_ref[0, pl.ds(kstart, nk), cols], vc_ref[0, :, cols]])
            o_ref[0, pl.ds(qstart, GRID_W), cols] = jnp.where(lo, o[:GRID_W], o[GRID_W:]).astype(BF16)
        return carry

    lax.fori_loop(0, B_ROWS_PER_STEP, body, 0)


def _attn_b_ctx_kernel(q_ref, kc_ref, vc_ref, o_ref):
    npair = B_HEADS // 2
    lo = _lo_mask()
    n = q_ref.shape[1]
    for t in range(npair):
        cols = slice(t * LANES, (t + 1) * LANES)
        qa, qb = _split_halves(q_ref[0, :, cols])
        q2 = jnp.concatenate([qa, qb], axis=0)
        s_ctx = _dot_nt(q2, kc_ref[0, :, cols])
        o = _attn_b_softmax_out(q2, [s_ctx], [vc_ref[0, :, cols]])
        o_ref[0, :, cols] = jnp.where(lo, o[:n], o[n:]).astype(BF16)


def _attn_b(p, pc, bias):
    b, s, _ = p.shape
    lc = pc.shape[1]
    w = B_HEADS * HEAD_DIM
    rows_n = s // GRID_W
    tq = B_ROWS_PER_STEP * GRID_W
    return pl.pallas_call(
        functools.partial(_attn_b_kernel, rows_n=rows_n),
        out_shape=jax.ShapeDtypeStruct((b, s, w), BF16),
        grid=(b, s // tq),
        in_specs=[
            pl.BlockSpec((1, tq, w), lambda bi, i: (bi, i, OFF_BQ // w)),
            pl.BlockSpec((1, s, w), lambda bi, i: (bi, 0, OFF_BK // w)),
            pl.BlockSpec((1, s, w), lambda bi, i: (bi, 0, OFF_BV // w)),
            pl.BlockSpec((1, lc, w), lambda bi, i: (bi, 0, OFF_BK // w)),
            pl.BlockSpec((1, lc, w), lambda bi, i: (bi, 0, OFF_BV // w)),
            pl.BlockSpec(bias.shape, lambda bi, i: (0, 0, 0, 0)),
        ],
        out_specs=pl.BlockSpec((1, tq, w), lambda bi, i: (bi, i, 0)),
        compiler_params=_cparams(("parallel", "arbitrary")),
        name="attn_neighbourhood",
    )(p, p, p, pc, pc, bias)


def _attn_b_ctx(pc):
    b, lc, _ = pc.shape
    w = B_HEADS * HEAD_DIM
    return pl.pallas_call(
        _attn_b_ctx_kernel,
        out_shape=jax.ShapeDtypeStruct((b, lc, w), BF16),
        grid=(b,),
        in_specs=[
            pl.BlockSpec((1, lc, w), lambda bi: (bi, 0, OFF_BQ // w)),
            pl.BlockSpec((1, lc, w), lambda bi: (bi, 0, OFF_BK // w)),
            pl.BlockSpec((1, lc, w), lambda bi: (bi, 0, OFF_BV // w)),
        ],
        out_specs=pl.BlockSpec((1, lc, w), lambda bi: (bi, 0, 0)),
        compiler_params=_cparams(("parallel",)),
        name="attn_neighbourhood_ctx",
    )(pc, pc, pc)


def _b_bias_tables(rpb):
    col = np.arange(GRID_W)
    cstart = np.clip(col - B_COLS // 2, 0, GRID_W - B_COLS)
    col_ok = (col[None, :] >= cstart[:, None]) & (col[None, :] < cstart[:, None] + B_COLS)
    dc_idx = np.clip(col[None, :] - col[:, None], -(B_COLS - 1), B_COLS - 1) + (B_COLS - 1)
    off = np.arange(B_MAX_ROWS)
    dr_idx = np.arange(B_MAX_ROWS)[None, :] + (B_MAX_ROWS - 1) - off[:, None]
    t = rpb[:, dr_idx[:, None, :, None], dc_idx[None, :, None, :]]
    t = jnp.where(col_ok[None, None, :, None, :], t.astype(F32), NEG_INF)
    t = t.reshape(B_HEADS, B_MAX_ROWS, GRID_W, B_MAX_ROWS * GRID_W)
    t = t.transpose(1, 0, 2, 3).reshape(B_MAX_ROWS, B_HEADS // 2, 2 * GRID_W, B_MAX_ROWS * GRID_W)
    return t


def _attn_c_kernel(lam_ref, q_ref, *refs, latent, out_scale):
    if latent:
        k_ref, v_ref, kc_ref, vc_ref, g_ref, o_ref = refs
    else:
        kc_ref, vc_ref, g_ref, o_ref = refs
    tq = q_ref.shape[1]
    lam = lam_ref[0]
    q = q_ref[0]
    quarter = lax.broadcasted_iota(jnp.int32, (1, LANES), 1) // C_QK_DIM
    zero = jnp.zeros_like(q)
    q4 = jnp.concatenate([jnp.where(quarter == j, q, zero) for j in range(4)], axis=0)
    kc = kc_ref[0]
    s_ctx = _dot_nt(q4, kc)
    m = jnp.max(s_ctx, axis=-1, keepdims=True)
    if latent:
        s_lat = _dot_nt(q4, k_ref[0])
        m = jnp.maximum(m, jnp.max(s_lat, axis=-1, keepdims=True))
        e_lat = jnp.exp(s_lat - m)
    e_ctx = jnp.exp(s_ctx - m)
    l = jnp.sum(e_ctx, axis=-1, keepdims=True)
    if latent:
        l = l + jnp.sum(e_lat, axis=-1, keepdims=True)
    inv = 1.0 / l
    outs = []
    for h in range(2):
        r0, r1 = slice(2 * h * tq, (2 * h + 1) * tq), slice((2 * h + 1) * tq, (2 * h + 2) * tq)
        c0, c1 = inv[r0], lam * inv[r1]
        o = _dot((e_ctx[r0] * c0 - e_ctx[r1] * c1).astype(BF16), vc_ref[0])
        if latent:
            o = o + _dot((e_lat[r0] * c0 - e_lat[r1] * c1).astype(BF16), v_ref[0])
        outs.append(o)
    lo = _lo_mask()
    o = jnp.where(lo, outs[0], outs[1])
    sq = o * o
    s_lo = jnp.sum(jnp.where(lo, sq, 0.0), axis=-1, keepdims=True)
    s_hi = jnp.sum(jnp.where(lo, 0.0, sq), axis=-1, keepdims=True)
    ms = jnp.where(lo, s_lo, s_hi) * (1.0 / C_V_DIM)
    o_ref[0] = (o * lax.rsqrt(ms + EPS) * g_ref[...] * out_scale).astype(BF16)


def _attn_c(p, pc, lam, subln2, lam_init, latent):
    src = p if latent else pc
    b, s, _ = src.shape
    lc = pc.shape[1]
    tq = 128 if latent else s
    npair = C_HEADS // 2
    in_specs = [pl.BlockSpec(memory_space=pltpu.SMEM),
                pl.BlockSpec((1, tq, LANES), lambda bi, hp, i: (bi, i, OFF_CQ // LANES + hp))]
    args = [lam, src]
    if latent:
        in_specs += [pl.BlockSpec((1, s, LANES), lambda bi, hp, i: (bi, 0, OFF_CK // LANES + hp)),
                     pl.BlockSpec((1, s, LANES), lambda bi, hp, i: (bi, 0, OFF_CV // LANES + hp))]
        args += [p, p]
    in_specs += [pl.BlockSpec((1, lc, LANES), lambda bi, hp, i: (bi, 0, OFF_CK // LANES + hp)),
                 pl.BlockSpec((1, lc, LANES), lambda bi, hp, i: (bi, 0, OFF_CV // LANES + hp)),
                 pl.BlockSpec((1, LANES), lambda bi, hp, i: (0, 0))]
    args += [pc, pc, subln2]
    return pl.pallas_call(
        functools.partial(_attn_c_kernel, latent=latent, out_scale=1.0 - lam_init),
        out_shape=jax.ShapeDtypeStruct((b, s, C_HEADS * C_V_DIM), BF16),
        grid=(b, npair, s // tq),
        in_specs=in_specs,
        out_specs=pl.BlockSpec((1, tq, LANES), lambda bi, hp, i: (bi, i, hp)),
        compiler_params=_cparams(("parallel", "parallel", "arbitrary")),
        name="attn_differential" + ("" if latent else "_ctx"),
    )(*args)


def _outproj_kernel(oa_ref, ob_ref, oc_ref, x_ref, w_ref, gpost_ref, gt_ref, gpre_ref, sc_ref, sh_ref,
                    x1_ref, h2_ref):
    wa = oa_ref.shape[2]
    wb = ob_ref.shape[2]
    y = (_dot(oa_ref[0], w_ref[0:wa, :]) + _dot(ob_ref[0], w_ref[wa:wa + wb, :])
         + _dot(oc_ref[0], w_ref[wa + wb:, :]))
    x1 = x_ref[0] + gt_ref[0] * _rms(y, gpost_ref[...])
    x1_ref[0] = x1
    h2_ref[0] = _rms(x1, gpre_ref[...]) * (1.0 + sc_ref[0]) + sh_ref[0]


def _outproj(oa, ob, oc, x, w_out, g_post, gt, g_pre, sc, sh, tm):
    b, s, d = x.shape
    per_sample = gt.shape[0] > 1
    mod_map = (lambda bi, i: (bi, 0, 0)) if per_sample else (lambda bi, i: (0, 0, 0))
    row = lambda bi, i: (bi, i, 0)
    const2 = lambda bi, i: (0, 0)
    return pl.pallas_call(
        _outproj_kernel,
        out_shape=(jax.ShapeDtypeStruct((b, s, d), F32), jax.ShapeDtypeStruct((b, s, d), F32)),
        grid=(b, s // tm),
        in_specs=[
            pl.BlockSpec((1, tm, oa.shape[2]), row),
            pl.BlockSpec((1, tm, ob.shape[2]), row),
            pl.BlockSpec((1, tm, oc.shape[2]), row),
            pl.BlockSpec((1, tm, d), row),
            pl.BlockSpec(w_out.shape, const2),
            pl.BlockSpec((1, d), const2),
            pl.BlockSpec((1, 1, d), mod_map),
            pl.BlockSpec((1, d), const2),
            pl.BlockSpec((1, 1, d), mod_map),
            pl.BlockSpec((1, 1, d), mod_map),
        ],
        out_specs=(pl.BlockSpec((1, tm, d), row), pl.BlockSpec((1, tm, d), row)),
        compiler_params=_cparams(("parallel", "parallel")),
        name="outproj_residual_norm",
    )(oa, ob, oc, x, w_out, g_post.reshape(1, d), gt, g_pre.reshape(1, d), sc, sh)


ROW_UNROLL = 8


def _gather_kernel(idx_ref, h_ref, o_ref, buf_ref):
    cap = buf_ref.shape[0]

    def body(j, carry):
        base = pl.multiple_of(j * ROW_UNROLL, ROW_UNROLL)
        for u in range(ROW_UNROLL):
            n = idx_ref[0, 0, base + u]
            buf_ref[pl.ds(base + u, 1), :] = h_ref[0, pl.ds(n, 1), :]
        return carry

    lax.fori_loop(0, cap // ROW_UNROLL, body, 0)
    o_ref[0, 0] = buf_ref[...].astype(BF16)


def _moe_gather(h2, idx):
    b, n, d = h2.shape
    e, cap = idx.shape[1], idx.shape[2]
    return pl.pallas_call(
        _gather_kernel,
        out_shape=jax.ShapeDtypeStruct((b, e, cap, d), BF16),
        grid=(b, e),
        in_specs=[
            pl.BlockSpec((1, 1, cap), lambda bi, ei: (bi * e + ei, 0, 0), memory_space=pltpu.SMEM),
            pl.BlockSpec((1, n, d), lambda bi, ei: (bi, 0, 0)),
        ],
        out_specs=pl.BlockSpec((1, 1, cap, d), lambda bi, ei: (bi, ei, 0, 0)),
        scratch_shapes=[pltpu.VMEM((cap, d), F32)],
        compiler_params=_cparams(("parallel", "arbitrary")),
        name="moe_gather",
    )(idx.reshape(b * e, 1, cap), h2)


FF_CHUNK = 512


def _ffn_kernel(xs_ref, gate_ref, wg_ref, wu_ref, wd_ref, o_ref):
    xs = xs_ref[0, 0]
    ff = wg_ref.shape[2]
    acc = None
    for c in range(ff // FF_CHUNK):
        cs = slice(c * FF_CHUNK, (c + 1) * FF_CHUNK)
        g = _dot(xs, wg_ref[0, :, cs])
        u = _dot(xs, wu_ref[0, :, cs])
        hid = ((g / (1.0 + jnp.exp(-g))) * u).astype(BF16)
        part = _dot(hid, wd_ref[0, cs, :])
        acc = part if acc is None else acc + part
    o_ref[0, 0] = acc * gate_ref[0, 0]


def _moe_ffn(xs, gates, wg, wu, wd):
    b, e, cap, d = xs.shape
    ff = wg.shape[2]
    return pl.pallas_call(
        _ffn_kernel,
        out_shape=jax.ShapeDtypeStruct((b, e, cap, d), F32),
        grid=(e, b),
        in_specs=[
            pl.BlockSpec((1, 1, cap, d), lambda ei, bi: (bi, ei, 0, 0)),
            pl.BlockSpec((1, 1, cap, 1), lambda ei, bi: (bi, ei, 0, 0)),
            pl.BlockSpec((1, d, ff), lambda ei, bi: (ei, 0, 0)),
            pl.BlockSpec((1, d, ff), lambda ei, bi: (ei, 0, 0)),
            pl.BlockSpec((1, ff, d), lambda ei, bi: (ei, 0, 0)),
        ],
        out_specs=pl.BlockSpec((1, 1, cap, d), lambda ei, bi: (bi, ei, 0, 0)),
        compiler_params=_cparams(("parallel", "arbitrary")),
        name="moe_expert_ffn",
    )(xs, gates.reshape(b, e, cap, 1), wg, wu, wd)


def _combine_kernel(idx_ref, y_ref, o_ref):
    cap = y_ref.shape[2]

    @pl.when(pl.program_id(1) == 0)
    def _():
        o_ref[...] = jnp.zeros_like(o_ref)

    def body(j, carry):
        base = pl.multiple_of(j * ROW_UNROLL, ROW_UNROLL)
        for u in range(ROW_UNROLL):
            n = idx_ref[0, 0, base + u]
            o_ref[0, pl.ds(n, 1), :] = o_ref[0, pl.ds(n, 1), :] + y_ref[0, 0, pl.ds(base + u, 1), :]
        return carry

    lax.fori_loop(0, cap // ROW_UNROLL, body, 0)


def _moe_combine(y, idx, n):
    b, e, cap, d = y.shape
    return pl.pallas_call(
        _combine_kernel,
        out_shape=jax.ShapeDtypeStruct((b, n, d), F32),
        grid=(b, e),
        in_specs=[
            pl.BlockSpec((1, 1, cap), lambda bi, ei: (bi * e + ei, 0, 0), memory_space=pltpu.SMEM),
            pl.BlockSpec((1, 1, cap, d), lambda bi, ei: (bi, ei, 0, 0)),
        ],
        out_specs=pl.BlockSpec((1, n, d), lambda bi, ei: (bi, 0, 0)),
        compiler_params=_cparams(("parallel", "arbitrary")),
        name="moe_combine",
    )(idx.reshape(b * e, 1, cap), y)


def _resid_kernel(x_ref, y_ref, g_ref, gt_ref, o_ref):
    o_ref[0] = x_ref[0] + gt_ref[0] * _rms(y_ref[0], g_ref[...])


def _resid(x, y, g, gt, tm):
    b, s, d = x.shape
    per_sample = gt.shape[0] > 1
    mod_map = (lambda bi, i: (bi, 0, 0)) if per_sample else (lambda bi, i: (0, 0, 0))
    row = lambda bi, i: (bi, i, 0)
    return pl.pallas_call(
        _resid_kernel,
        out_shape=jax.ShapeDtypeStruct((b, s, d), F32),
        grid=(b, s // tm),
        in_specs=[pl.BlockSpec((1, tm, d), row), pl.BlockSpec((1, tm, d), row),
                  pl.BlockSpec((1, d), lambda bi, i: (0, 0)), pl.BlockSpec((1, 1, d), mod_map)],
        out_specs=pl.BlockSpec((1, tm, d), row),
        compiler_params=_cparams(("parallel", "parallel")),
        name="ffn_residual",
    )(x, y, g.reshape(1, d), gt)


def _route(h2, w_router, cap):
    logits = jnp.einsum('bnd,de->bne', h2, w_router, precision=lax.Precision.HIGHEST)
    aff = jax.nn.softmax(logits, axis=-1)
    gates, idx = lax.top_k(jnp.swapaxes(aff, 1, 2), cap)
    return gates, idx.astype(jnp.int32)


def _moe(h2, w_router, wg, wu, wd):
    b, n, d = h2.shape
    cap = n * CAPACITY_FACTOR // N_EXPERTS
    gates, idx = _route(h2, w_router, cap)
    xs = _moe_gather(h2, idx)
    y = _moe_ffn(xs, gates, wg, wu, wd)
    return _moe_combine(y, idx, n)


def _rope_partner(i, half):
    return i + half if (i % (2 * half)) < half else i - half


def _column_maps():
    sec = np.cumsum([0, 384, 128, 128, 384, 384, 384, 256, 256, 256])
    o_aq, o_ak, o_av, o_bq, o_bk, o_bv, o_cq, o_ck, o_cv = sec[:9]
    main, partner = [], []
    for t in range(A_Q_PER_KV):
        for g in range(A_KV_HEADS):
            base = o_aq + (g * A_Q_PER_KV + t) * HEAD_DIM
            main += [base + i for i in range(HEAD_DIM)]
            partner += [base + _rope_partner(i, HEAD_DIM // 4) for i in range(HEAD_DIM)]
    for g in range(A_KV_HEADS):
        base = o_ak + g * HEAD_DIM
        main += [base + i for i in range(HEAD_DIM)]
        partner += [base + _rope_partner(i, HEAD_DIM // 4) for i in range(HEAD_DIM)]
    for o_sec in (o_cq, o_ck):
        for hm in range(2 * C_HEADS):
            base = o_sec + hm * C_QK_DIM
            main += [base + i for i in range(C_QK_DIM)]
            partner += [base + _rope_partner(i, C_QK_DIM // 4) for i in range(C_QK_DIM)]
    assert len(main) == ROPE_W
    main += list(range(o_av, o_av + 128)) + list(range(o_bq, o_bq + 3 * 384)) + list(range(o_cv, o_cv + 256))
    assert len(main) == IN_W
    return np.asarray(main, np.int32), np.asarray(partner, np.int32)


def _rope_tables(s, scale_c):
    t = jnp.arange(s)
    rows, cols = t // GRID_W, t % GRID_W

    def head_tables(d):
        q = d // 4
        inv = ROPE_THETA ** (-jnp.arange(q, dtype=F32) / q)
        cs, sn = [], []
        for pos in (rows, cols):
            ang = pos.astype(F32)[:, None] * inv[None, :]
            c, s_ = jnp.cos(ang), jnp.sin(ang)
            cs += [c, c]
            sn += [-s_, s_]
        c = jnp.concatenate(cs, axis=1)
        s_ = jnp.concatenate(sn, axis=1)
        reps = LANES // d
        return jnp.tile(c, (1, reps)), jnp.tile(s_, (1, reps))

    ca, sa = head_tables(HEAD_DIM)
    cc, sc = head_tables(C_QK_DIM)
    scale_a = HEAD_DIM ** -0.5
    cos = jnp.concatenate([ca * scale_a, ca, cc * scale_c, cc], axis=1)
    sin = jnp.concatenate([sa * scale_a, sa, sc * scale_c, sc], axis=1)
    return cos, sin


def _ctx_tables(lc, scale_c):
    ones = jnp.ones((lc, LANES), F32)
    cos = jnp.concatenate([ones * HEAD_DIM ** -0.5, ones, ones * scale_c, ones], axis=1)
    return cos, jnp.zeros_like(cos)


def _out_row_map():
    rows = []
    for t in range(A_Q_PER_KV):
        for g in range(A_KV_HEADS):
            base = (g * A_Q_PER_KV + t) * HEAD_DIM
            rows += list(range(base, base + HEAD_DIM))
    rows += list(range(A_Q_HEADS * HEAD_DIM, 1024))
    return np.asarray(rows, np.int32)


def kernel(x, c, ctx, c_ctx, w_ada, b_ada, g_mix_pre, g_mix_post, g_ffn_pre, g_ffn_post, w_in, w_out,
           a_sink, b_rpb, c_lam_q1, c_lam_k1, c_lam_q2, c_lam_k2, c_subln, w_router, w_gate, w_up, w_down):
    b, s, d = x.shape
    lc = ctx.shape[1]
    depth = w_in.shape[0]
    scale_c = C_QK_DIM ** -0.5

    main_map, partner_map = _column_maps()
    out_rows = _out_row_map()
    cos_x, sin_x = _rope_tables(s, scale_c)
    cos_c, sin_c = _ctx_tables(lc, scale_c)

    pad = (-(b + 1)) % 8
    cstack = jnp.concatenate([c, c_ctx[None, :], jnp.zeros((pad, d), F32)], axis=0)
    mod = _modulation(cstack, w_ada, b_ada)

    tm = 512
    h_ctx = ctx
    for l in range(depth):
        need_ctx = l < depth - 1
        mx = mod[l, :b].reshape(b, 1, N_MOD, d)
        sh1, sc1, gt1, sh2, sc2, gt2 = [mx[:, :, k] for k in range(N_MOD)]
        mc = mod[l, b:b + 1].reshape(1, 1, N_MOD, d)
        csh1, csc1, cgt1, csh2, csc2, cgt2 = [mc[:, :, k] for k in range(N_MOD)]

        bq_scale = jnp.ones((IN_W,), F32).at[OFF_BQ:OFF_BK].set(HEAD_DIM ** -0.5)
        w_cat = jnp.concatenate([w_in[l][:, main_map] * bq_scale[None, :], w_in[l][:, partner_map]],
                                axis=1).astype(BF16)
        w_o = w_out[l][out_rows, :].astype(BF16)
        lam_init = 0.8 - 0.6 * math.exp(-0.3 * l)
        lam = (jnp.exp(jnp.sum(c_lam_q1[l] * c_lam_k1[l])) - jnp.exp(jnp.sum(c_lam_q2[l] * c_lam_k2[l]))
               + lam_init).reshape(1).astype(F32)
        subln2 = jnp.tile(c_subln[l], 2).reshape(1, LANES)
        bias = _b_bias_tables(b_rpb[l])
        wg, wu, wd = w_gate[l].astype(BF16), w_up[l].astype(BF16), w_down[l].astype(BF16)

        p = _inproj(x, g_mix_pre[l], sc1, sh1, cos_x, sin_x, w_cat, tm)
        pc = _inproj(h_ctx, g_mix_pre[l], csc1, csh1, cos_c, sin_c, w_cat, lc)

        oa = _attn_a(p, pc, a_sink[l], True)
        ob = _attn_b(p, pc, bias)
        od = _attn_c(p, pc, lam, subln2, lam_init, True)
        x1, h2 = _outproj(oa, ob, od, x, w_o, g_mix_post[l], gt1, g_ffn_pre[l], sc2, sh2, tm)
        if need_ctx:
            oac = _attn_a(p, pc, a_sink[l], False)
            obc = _attn_b_ctx(pc)
            odc = _attn_c(p, pc, lam, subln2, lam_init, False)
            c1, hc2 = _outproj(oac, obc, odc, h_ctx, w_o, g_mix_post[l], cgt1, g_ffn_pre[l], csc2, csh2, lc)
            yc = _moe(hc2, w_router[l], wg, wu, wd)
            h_ctx = _resid(c1, yc, g_ffn_post[l], cgt2, lc)
        y = _moe(h2, w_router[l], wg, wu, wd)
        x = _resid(x1, y, g_ffn_post[l], gt2, tm)
    return x
```

```python
import functools
import math

import numpy as np
import jax
import jax.numpy as jnp
from jax import lax
from jax.experimental import pallas as pl
from jax.experimental.pallas import tpu as pltpu

F32 = jnp.float32
BF16 = jnp.bfloat16

GRID_W = 64
HEAD_DIM = 64
A_Q_HEADS, A_KV_HEADS = 6, 2
A_Q_PER_KV = A_Q_HEADS // A_KV_HEADS
A_WINDOW = 128
A_BLOCK = 128
B_HEADS, B_MAX_ROWS, B_COLS = 6, 8, 16
C_HEADS, C_QK_DIM, C_V_DIM = 4, 32, 64
N_EXPERTS = 16
CAPACITY_FACTOR = 2
N_MOD = 6
ROPE_THETA = 10000.0
EPS = 1e-6
NEG_INF = -1e30

LANES = 128
ROPE_W = 1024
OFF_AQ, OFF_AK, OFF_CQ, OFF_CK, OFF_AV, OFF_BQ, OFF_BK, OFF_BV, OFF_CV = (
    0, 384, 512, 768, 1024, 1152, 1536, 1920, 2304)
IN_W = 2560
VMEM_LIMIT = 56 * 1024 * 1024


def _cparams(sem):
    return pltpu.CompilerParams(dimension_semantics=sem, vmem_limit_bytes=VMEM_LIMIT)


def _dot(a, b):
    return jnp.dot(a, b, preferred_element_type=F32)


def _dot_nt(a, b):
    return lax.dot_general(a, b, (((1,), (1,)), ((), ())), preferred_element_type=F32)


def _rms(x, g):
    return x * lax.rsqrt(jnp.mean(x * x, axis=-1, keepdims=True) + EPS) * g


def _mod_kernel(c_ref, w_ref, b_ref, o_ref):
    c = c_ref[...]
    a = c / (1.0 + jnp.exp(-c))
    o_ref[0] = jnp.dot(a, w_ref[0], preferred_element_type=F32,
                       precision=lax.Precision.HIGHEST) + b_ref[0]


def _modulation(cstack, w_ada, b_ada):
    nl, d, n = w_ada.shape
    rows = cstack.shape[0]
    tn = 1536
    return pl.pallas_call(
        _mod_kernel,
        out_shape=jax.ShapeDtypeStruct((nl, rows, n), F32),
        grid=(nl, n // tn),
        in_specs=[
            pl.BlockSpec((rows, d), lambda l, j: (0, 0)),
            pl.BlockSpec((1, d, tn), lambda l, j: (l, 0, j)),
            pl.BlockSpec((1, 1, tn), lambda l, j: (l, 0, j)),
        ],
        out_specs=pl.BlockSpec((1, rows, tn), lambda l, j: (l, 0, j)),
        compiler_params=_cparams(("parallel", "parallel")),
        name="adaln_modulation",
    )(cstack, w_ada, b_ada.reshape(nl, 1, n))


_ROPE_TABLE_OF_UNIT = (0, 0, 0, 1, 2, 2, 3, 3)


def _inproj_kernel(x_ref, g_ref, sc_ref, sh_ref, cos_ref, sin_ref, w_ref, o_ref):
    x = x_ref[0]
    h = _rms(x, g_ref[...]) * (1.0 + sc_ref[0]) + sh_ref[0]
    hb = h.astype(BF16)
    for j in range(ROPE_W // 256):
        t0, t1 = _ROPE_TABLE_OF_UNIT[2 * j], _ROPE_TABLE_OF_UNIT[2 * j + 1]
        cos = jnp.concatenate([cos_ref[:, t0 * LANES:(t0 + 1) * LANES],
                               cos_ref[:, t1 * LANES:(t1 + 1) * LANES]], axis=1)
        sin = jnp.concatenate([sin_ref[:, t0 * LANES:(t0 + 1) * LANES],
                               sin_ref[:, t1 * LANES:(t1 + 1) * LANES]], axis=1)
        p = _dot(hb, w_ref[:, 256 * j:256 * (j + 1)])
        ps = _dot(hb, w_ref[:, IN_W + 256 * j:IN_W + 256 * (j + 1)])
        o_ref[0, :, 256 * j:256 * (j + 1)] = (p * cos + ps * sin).astype(BF16)
    for j in range(ROPE_W // 256, IN_W // 256):
        o_ref[0, :, 256 * j:256 * (j + 1)] = _dot(hb, w_ref[:, 256 * j:256 * (j + 1)]).astype(BF16)


def _inproj(x, g, sc, sh, cos, sin, w_cat, tm):
    b, s, d = x.shape
    per_sample = sc.shape[0] > 1
    mod_map = (lambda bi, i: (bi, 0, 0)) if per_sample else (lambda bi, i: (0, 0, 0))
    return pl.pallas_call(
        _inproj_kernel,
        out_shape=jax.ShapeDtypeStruct((b, s, IN_W), BF16),
        grid=(b, s // tm),
        in_specs=[
            pl.BlockSpec((1, tm, d), lambda bi, i: (bi, i, 0)),
            pl.BlockSpec((1, d), lambda bi, i: (0, 0)),
            pl.BlockSpec((1, 1, d), mod_map),
            pl.BlockSpec((1, 1, d), mod_map),
            pl.BlockSpec((tm, 4 * LANES), lambda bi, i: (i, 0)),
            pl.BlockSpec((tm, 4 * LANES), lambda bi, i: (i, 0)),
            pl.BlockSpec((d, IN_W + ROPE_W), lambda bi, i: (0, 0)),
        ],
        out_specs=pl.BlockSpec((1, tm, IN_W), lambda bi, i: (bi, i, 0)),
        compiler_params=_cparams(("parallel", "parallel")),
        name="norm_inproj_rope",
    )(x, g.reshape(1, d), sc, sh, cos, sin, w_cat)


def _lo_mask():
    return lax.broadcasted_iota(jnp.int32, (1, LANES), 1) < HEAD_DIM


def _split_halves(q):
    lo = _lo_mask()
    zero = jnp.zeros_like(q)
    return jnp.where(lo, q, zero), jnp.where(lo, zero, q)


def _attn_a_kernel(sink_ref, q_ref, *refs, seq, latent):
    if latent:
        k_ref, v_ref, kc_ref, vc_ref, o_ref = refs
    else:
        kc_ref, vc_ref, o_ref = refs
    tq = q_ref.shape[1]
    npair = A_Q_HEADS // 2
    los, his = [], []
    for t in range(npair):
        a, b = _split_halves(q_ref[0, :, t * LANES:(t + 1) * LANES])
        los.append(a)
        his.append(b)
    q6 = jnp.concatenate(los + his, axis=0)
    sink = jnp.concatenate([jnp.full((tq, 1), sink_ref[j], F32) for j in range(A_Q_HEADS)], axis=0)
    kc = kc_ref[0]
    vc = vc_ref[0]
    s_ctx = _dot_nt(q6, kc)
    m = jnp.maximum(jnp.max(s_ctx, axis=-1, keepdims=True), sink)
    if latent:
        i = pl.program_id(1)
        band = 3 * A_BLOCK
        start = pl.multiple_of(jnp.clip((i - 1) * A_BLOCK, 0, seq - band), A_BLOCK)
        kb = k_ref[0, pl.ds(start, band), :]
        vb = v_ref[0, pl.ds(start, band), :]
        s_lat = _dot_nt(q6, kb)
        row = lax.broadcasted_iota(jnp.int32, (A_Q_HEADS * tq, 1), 0)
        qpos = i * tq + (row & (tq - 1))
        kpos = start + lax.broadcasted_iota(jnp.int32, (1, band), 1)
        s_lat = jnp.where(jnp.abs(qpos - kpos) <= A_WINDOW, s_lat, NEG_INF)
        m = jnp.maximum(m, jnp.max(s_lat, axis=-1, keepdims=True))
        e_lat = jnp.exp(s_lat - m)
    e_ctx = jnp.exp(s_ctx - m)
    l = jnp.sum(e_ctx, axis=-1, keepdims=True) + jnp.exp(sink - m)
    o = _dot(e_ctx.astype(BF16), vc)
    if latent:
        l = l + jnp.sum(e_lat, axis=-1, keepdims=True)
        o = o + _dot(e_lat.astype(BF16), vb)
    o = o / l
    lo = _lo_mask()
    for t in range(npair):
        o_ref[0, :, t * LANES:(t + 1) * LANES] = jnp.where(
            lo, o[t * tq:(t + 1) * tq], o[(npair + t) * tq:(npair + t + 1) * tq]).astype(BF16)


def _attn_a(p, pc, sink, latent):
    src = p if latent else pc
    b, s, _ = src.shape
    lc = pc.shape[1]
    tq = A_BLOCK if latent else s
    wq = A_Q_HEADS * HEAD_DIM
    in_specs = [pl.BlockSpec(memory_space=pltpu.SMEM),
                pl.BlockSpec((1, tq, wq), lambda bi, i: (bi, i, OFF_AQ // wq))]
    args = [sink, src]
    if latent:
        in_specs += [pl.BlockSpec((1, s, LANES), lambda bi, i: (bi, 0, OFF_AK // LANES)),
                     pl.BlockSpec((1, s, LANES), lambda bi, i: (bi, 0, OFF_AV // LANES))]
        args += [p, p]
    in_specs += [pl.BlockSpec((1, lc, LANES), lambda bi, i: (bi, 0, OFF_AK // LANES)),
                 pl.BlockSpec((1, lc, LANES), lambda bi, i: (bi, 0, OFF_AV // LANES))]
    args += [pc, pc]
    return pl.pallas_call(
        functools.partial(_attn_a_kernel, seq=s, latent=latent),
        out_shape=jax.ShapeDtypeStruct((b, s, wq), BF16),
        grid=(b, s // tq),
        in_specs=in_specs,
        out_specs=pl.BlockSpec((1, tq, wq), lambda bi, i: (bi, i, 0)),
        compiler_params=_cparams(("parallel", "arbitrary")),
        name="attn_window_gqa" + ("" if latent else "_ctx"),
    )(*args)


B_ROWS_PER_STEP = 8


def _attn_b_softmax_out(q2, s_parts, v_parts):
    m = functools.reduce(jnp.maximum, [jnp.max(s, axis=-1, keepdims=True) for s in s_parts])
    es = [jnp.exp(s - m) for s in s_parts]
    l = functools.reduce(jnp.add, [jnp.sum(e, axis=-1, keepdims=True) for e in es])
    o = functools.reduce(jnp.add, [_dot(e.astype(BF16), v) for e, v in zip(es, v_parts)])
    return o / l


def _attn_b_kernel(q_ref, k_ref, v_ref, kc_ref, vc_ref, bias_ref, o_ref, *, rows_n):
    i = pl.program_id(1)
    npair = B_HEADS // 2
    lo = _lo_mask()
    nk = B_MAX_ROWS * GRID_W

    def body(rr, carry):
        r = i * B_ROWS_PER_STEP + rr
        rs = jnp.clip(r - B_MAX_ROWS // 2, 0, rows_n - B_MAX_ROWS)
        off = r - rs
        kstart = pl.multiple_of(rs * GRID_W, GRID_W)
        qstart = pl.multiple_of(rr * GRID_W, GRID_W)
        for t in range(npair):
            cols = slice(t * LANES, (t + 1) * LANES)
            qa, qb = _split_halves(q_ref[0, pl.ds(qstart, GRID_W), cols])
            q2 = jnp.concatenate([qa, qb], axis=0)
            s_lat = _dot_nt(q2, k_ref[0, pl.ds(kstart, nk), cols]) + bias_ref[off, t]
            s_ctx = _dot_nt(q2, kc_ref[0, :, cols])
            o = _attn_b_softmax_out(q2, [s_lat, s_ctx],
                                    [v_ref[0, pl.ds(kstart, nk), cols], vc_ref[0, :, cols]])
            o_ref[0, pl.ds(qstart, GRID_W), cols] = jnp.where(lo, o[:GRID_W], o[GRID_W:]).astype(BF16)
        return carry

    lax.fori_loop(0, B_ROWS_PER_STEP, body, 0)


def _attn_b_ctx_kernel(q_ref, kc_ref, vc_ref, o_ref):
    npair = B_HEADS // 2
    lo = _lo_mask()
    n = q_ref.shape[1]
    for t in range(npair):
        cols = slice(t * LANES, (t + 1) * LANES)
        qa, qb = _split_halves(q_ref[0, :, cols])
        q2 = jnp.concatenate([qa, qb], axis=0)
        s_ctx = _dot_nt(q2, kc_ref[0, :, cols])
        o = _attn_b_softmax_out(q2, [s_ctx], [vc_ref[0, :, cols]])
        o_ref[0, :, cols] = jnp.where(lo, o[:n], o[n:]).astype(BF16)


def _attn_b(p, pc, bias):
    b, s, _ = p.shape
    lc = pc.shape[1]
    w = B_HEADS * HEAD_DIM
    rows_n = s // GRID_W
    tq = B_ROWS_PER_STEP * GRID_W
    return pl.pallas_call(
        functools.partial(_attn_b_kernel, rows_n=rows_n),
        out_shape=jax.ShapeDtypeStruct((b, s, w), BF16),
        grid=(b, s // tq),
        in_specs=[
            pl.BlockSpec((1, tq, w), lambda bi, i: (bi, i, OFF_BQ // w)),
            pl.BlockSpec((1, s, w), lambda bi, i: (bi, 0, OFF_BK // w)),
            pl.BlockSpec((1, s, w), lambda bi, i: (bi, 0, OFF_BV // w)),
            pl.BlockSpec((1, lc, w), lambda bi, i: (bi, 0, OFF_BK // w)),
            pl.BlockSpec((1, lc, w), lambda bi, i: (bi, 0, OFF_BV // w)),
            pl.BlockSpec(bias.shape, lambda bi, i: (0, 0, 0, 0)),
        ],
        out_specs=pl.BlockSpec((1, tq, w), lambda bi, i: (bi, i, 0)),
        compiler_params=_cparams(("parallel", "arbitrary")),
        name="attn_neighbourhood",
    )(p, p, p, pc, pc, bias)


def _attn_b_ctx(pc):
    b, lc, _ = pc.shape
    w = B_HEADS * HEAD_DIM
    return pl.pallas_call(
        _attn_b_ctx_kernel,
        out_shape=jax.ShapeDtypeStruct((b, lc, w), BF16),
        grid=(b,),
        in_specs=[
            pl.BlockSpec((1, lc, w), lambda bi: (bi, 0, OFF_BQ // w)),
            pl.BlockSpec((1, lc, w), lambda bi: (bi, 0, OFF_BK // w)),
            pl.BlockSpec((1, lc, w), lambda bi: (bi, 0, OFF_BV // w)),
        ],
        out_specs=pl.BlockSpec((1, lc, w), lambda bi: (bi, 0, 0)),
        compiler_params=_cparams(("parallel",)),
        name="attn_neighbourhood_ctx",
    )(pc, pc, pc)


def _b_bias_tables(rpb):
    col = np.arange(GRID_W)
    cstart = np.clip(col - B_COLS // 2, 0, GRID_W - B_COLS)
    col_ok = (col[None, :] >= cstart[:, None]) & (col[None, :] < cstart[:, None] + B_COLS)
    dc_idx = np.clip(col[None, :] - col[:, None], -(B_COLS - 1), B_COLS - 1) + (B_COLS - 1)
    off = np.arange(B_MAX_ROWS)
    dr_idx = np.arange(B_MAX_ROWS)[None, :] + (B_MAX_ROWS - 1) - off[:, None]
    t = rpb[:, dr_idx[:, None, :, None], dc_idx[None, :, None, :]]
    t = jnp.where(col_ok[None, None, :, None, :], t.astype(F32), NEG_INF)
    t = t.reshape(B_HEADS, B_MAX_ROWS, GRID_W, B_MAX_ROWS * GRID_W)
    t = t.transpose(1, 0, 2, 3).reshape(B_MAX_ROWS, B_HEADS // 2, 2 * GRID_W, B_MAX_ROWS * GRID_W)
    return t


def _attn_c_kernel(lam_ref, q_ref, *refs, latent, out_scale):
    if latent:
        k_ref, v_ref, kc_ref, vc_ref, g_ref, o_ref = refs
    else:
        kc_ref, vc_ref, g_ref, o_ref = refs
    tq = q_ref.shape[1]
    lam = lam_ref[0]
    q = q_ref[0]
    quarter = lax.broadcasted_iota(jnp.int32, (1, LANES), 1) // C_QK_DIM
    zero = jnp.zeros_like(q)
    q4 = jnp.concatenate([jnp.where(quarter == j, q, zero) for j in range(4)], axis=0)
    kc = kc_ref[0]
    s_ctx = _dot_nt(q4, kc)
    m = jnp.max(s_ctx, axis=-1, keepdims=True)
    if latent:
        s_lat = _dot_nt(q4, k_ref[0])
        m = jnp.maximum(m, jnp.max(s_lat, axis=-1, keepdims=True))
        e_lat = jnp.exp(s_lat - m)
    e_ctx = jnp.exp(s_ctx - m)
    l = jnp.sum(e_ctx, axis=-1, keepdims=True)
    if latent:
        l = l + jnp.sum(e_lat, axis=-1, keepdims=True)
    inv = 1.0 / l
    outs = []
    for h in range(2):
        r0, r1 = slice(2 * h * tq, (2 * h + 1) * tq), slice((2 * h + 1) * tq, (2 * h + 2) * tq)
        c0, c1 = inv[r0], lam * inv[r1]
        o = _dot((e_ctx[r0] * c0 - e_ctx[r1] * c1).astype(BF16), vc_ref[0])
        if latent:
            o = o + _dot((e_lat[r0] * c0 - e_lat[r1] * c1).astype(BF16), v_ref[0])
        outs.append(o)
    lo = _lo_mask()
    o = jnp.where(lo, outs[0], outs[1])
    sq = o * o
    s_lo = jnp.sum(jnp.where(lo, sq, 0.0), axis=-1, keepdims=True)
    s_hi = jnp.sum(jnp.where(lo, 0.0, sq), axis=-1, keepdims=True)
    ms = jnp.where(lo, s_lo, s_hi) * (1.0 / C_V_DIM)
    o_ref[0] = (o * lax.rsqrt(ms + EPS) * g_ref[...] * out_scale).astype(BF16)


def _attn_c(p, pc, lam, subln2, lam_init, latent):
    src = p if latent else pc
    b, s, _ = src.shape
    lc = pc.shape[1]
    tq = 128 if latent else s
    npair = C_HEADS // 2
    in_specs = [pl.BlockSpec(memory_space=pltpu.SMEM),
                pl.BlockSpec((1, tq, LANES), lambda bi, hp, i: (bi, i, OFF_CQ // LANES + hp))]
    args = [lam, src]
    if latent:
        in_specs += [pl.BlockSpec((1, s, LANES), lambda bi, hp, i: (bi, 0, OFF_CK // LANES + hp)),
                     pl.BlockSpec((1, s, LANES), lambda bi, hp, i: (bi, 0, OFF_CV // LANES + hp))]
        args += [p, p]
    in_specs += [pl.BlockSpec((1, lc, LANES), lambda bi, hp, i: (bi, 0, OFF_CK // LANES + hp)),
                 pl.BlockSpec((1, lc, LANES), lambda bi, hp, i: (bi, 0, OFF_CV // LANES + hp)),
                 pl.BlockSpec((1, LANES), lambda bi, hp, i: (0, 0))]
    args += [pc, pc, subln2]
    return pl.pallas_call(
        functools.partial(_attn_c_kernel, latent=latent, out_scale=1.0 - lam_init),
        out_shape=jax.ShapeDtypeStruct((b, s, C_HEADS * C_V_DIM), BF16),
        grid=(b, npair, s // tq),
        in_specs=in_specs,
        out_specs=pl.BlockSpec((1, tq, LANES), lambda bi, hp, i: (bi, i, hp)),
        compiler_params=_cparams(("parallel", "parallel", "arbitrary")),
        name="attn_differential" + ("" if latent else "_ctx"),
    )(*args)


def _outproj_kernel(oa_ref, ob_ref, oc_ref, x_ref, w_ref, gpost_ref, gt_ref, gpre_ref, sc_ref, sh_ref,
                    x1_ref, h2_ref):
    wa = oa_ref.shape[2]
    wb = ob_ref.shape[2]
    y = (_dot(oa_ref[0], w_ref[0:wa, :]) + _dot(ob_ref[0], w_ref[wa:wa + wb, :])
         + _dot(oc_ref[0], w_ref[wa + wb:, :]))
    x1 = x_ref[0] + gt_ref[0] * _rms(y, gpost_ref[...])
    x1_ref[0] = x1
    h2_ref[0] = _rms(x1, gpre_ref[...]) * (1.0 + sc_ref[0]) + sh_ref[0]


def _outproj(oa, ob, oc, x, w_out, g_post, gt, g_pre, sc, sh, tm):
    b, s, d = x.shape
    per_sample = gt.shape[0] > 1
    mod_map = (lambda bi, i: (bi, 0, 0)) if per_sample else (lambda bi, i: (0, 0, 0))
    row = lambda bi, i: (bi, i, 0)
    const2 = lambda bi, i: (0, 0)
    return pl.pallas_call(
        _outproj_kernel,
        out_shape=(jax.ShapeDtypeStruct((b, s, d), F32), jax.ShapeDtypeStruct((b, s, d), F32)),
        grid=(b, s // tm),
        in_specs=[
            pl.BlockSpec((1, tm, oa.shape[2]), row),
            pl.BlockSpec((1, tm, ob.shape[2]), row),
            pl.BlockSpec((1, tm, oc.shape[2]), row),
            pl.BlockSpec((1, tm, d), row),
            pl.BlockSpec(w_out.shape, const2),
            pl.BlockSpec((1, d), const2),
            pl.BlockSpec((1, 1, d), mod_map),
            pl.BlockSpec((1, d), const2),
            pl.BlockSpec((1, 1, d), mod_map),
            pl.BlockSpec((1, 1, d), mod_map),
        ],
        out_specs=(pl.BlockSpec((1, tm, d), row), pl.BlockSpec((1, tm, d), row)),
        compiler_params=_cparams(("parallel", "parallel")),
        name="outproj_residual_norm",
    )(oa, ob, oc, x, w_out, g_post.reshape(1, d), gt, g_pre.reshape(1, d), sc, sh)


ROW_UNROLL = 8


def _gather_kernel(idx_ref, h_ref, o_ref, buf_ref):
    cap = buf_ref.shape[0]

    def body(j, carry):
        base = pl.multiple_of(j * ROW_UNROLL, ROW_UNROLL)
        for u in range(ROW_UNROLL):
            n = idx_ref[0, 0, base + u]
            buf_ref[pl.ds(base + u, 1), :] = h_ref[0, pl.ds(n, 1), :]
        return carry

    lax.fori_loop(0, cap // ROW_UNROLL, body, 0)
    o_ref[0, 0] = buf_ref[...].astype(BF16)


def _moe_gather(h2, idx):
    b, n, d = h2.shape
    e, cap = idx.shape[1], idx.shape[2]
    return pl.pallas_call(
        _gather_kernel,
        out_shape=jax.ShapeDtypeStruct((e, b, cap, d), BF16),
        grid=(b, e),
        in_specs=[
            pl.BlockSpec((1, 1, cap), lambda bi, ei: (bi * e + ei, 0, 0), memory_space=pltpu.SMEM),
            pl.BlockSpec((1, n, d), lambda bi, ei: (bi, 0, 0)),
        ],
        out_specs=pl.BlockSpec((1, 1, cap, d), lambda bi, ei: (ei, bi, 0, 0)),
        scratch_shapes=[pltpu.VMEM((cap, d), F32)],
        compiler_params=_cparams(("parallel", "arbitrary")),
        name="moe_gather",
    )(idx.reshape(b * e, 1, cap), h2)


FF_CHUNK = 512


def _ffn_kernel(xs_ref, gate_ref, wg_ref, wu_ref, wd_ref, o_ref):
    xs = xs_ref[0]
    ff = wg_ref.shape[2]
    acc = None
    for c in range(ff // FF_CHUNK):
        cs = slice(c * FF_CHUNK, (c + 1) * FF_CHUNK)
        g = _dot(xs, wg_ref[0, :, cs])
        u = _dot(xs, wu_ref[0, :, cs])
        hid = ((g / (1.0 + jnp.exp(-g))) * u).astype(BF16)
        part = _dot(hid, wd_ref[0, cs, :])
        acc = part if acc is None else acc + part
    o_ref[0] = acc * gate_ref[0]


FFN_ROWS = 512


def _moe_ffn(xs, gates, wg, wu, wd):
    e, r, d = xs.shape
    ff = wg.shape[2]
    tr = min(r, FFN_ROWS)
    return pl.pallas_call(
        _ffn_kernel,
        out_shape=jax.ShapeDtypeStruct((e, r, d), F32),
        grid=(e, r // tr),
        in_specs=[
            pl.BlockSpec((1, tr, d), lambda ei, ri: (ei, ri, 0)),
            pl.BlockSpec((1, tr, 1), lambda ei, ri: (ei, ri, 0)),
            pl.BlockSpec((1, d, ff), lambda ei, ri: (ei, 0, 0)),
            pl.BlockSpec((1, d, ff), lambda ei, ri: (ei, 0, 0)),
            pl.BlockSpec((1, ff, d), lambda ei, ri: (ei, 0, 0)),
        ],
        out_specs=pl.BlockSpec((1, tr, d), lambda ei, ri: (ei, ri, 0)),
        compiler_params=_cparams(("parallel", "arbitrary")),
        name="moe_expert_ffn",
    )(xs, gates, wg, wu, wd)


def _combine_kernel(idx_ref, y_ref, o_ref):
    cap = y_ref.shape[2]

    @pl.when(pl.program_id(1) == 0)
    def _():
        o_ref[...] = jnp.zeros_like(o_ref)

    def body(j, carry):
        base = pl.multiple_of(j * ROW_UNROLL, ROW_UNROLL)
        for u in range(ROW_UNROLL):
            n = idx_ref[0, 0, base + u]
            o_ref[0, pl.ds(n, 1), :] = o_ref[0, pl.ds(n, 1), :] + y_ref[0, 0, pl.ds(base + u, 1), :]
        return carry

    lax.fori_loop(0, cap // ROW_UNROLL, body, 0)


def _moe_combine(y, idx, n):
    e, b, cap, d = y.shape
    return pl.pallas_call(
        _combine_kernel,
        out_shape=jax.ShapeDtypeStruct((b, n, d), F32),
        grid=(b, e),
        in_specs=[
            pl.BlockSpec((1, 1, cap), lambda bi, ei: (bi * e + ei, 0, 0), memory_space=pltpu.SMEM),
            pl.BlockSpec((1, 1, cap, d), lambda bi, ei: (ei, bi, 0, 0)),
        ],
        out_specs=pl.BlockSpec((1, n, d), lambda bi, ei: (bi, 0, 0)),
        compiler_params=_cparams(("parallel", "arbitrary")),
        name="moe_combine",
    )(idx.reshape(b * e, 1, cap), y)


def _resid_kernel(x_ref, y_ref, g_ref, gt_ref, o_ref):
    o_ref[0] = x_ref[0] + gt_ref[0] * _rms(y_ref[0], g_ref[...])


def _resid(x, y, g, gt, tm):
    b, s, d = x.shape
    per_sample = gt.shape[0] > 1
    mod_map = (lambda bi, i: (bi, 0, 0)) if per_sample else (lambda bi, i: (0, 0, 0))
    row = lambda bi, i: (bi, i, 0)
    return pl.pallas_call(
        _resid_kernel,
        out_shape=jax.ShapeDtypeStruct((b, s, d), F32),
        grid=(b, s // tm),
        in_specs=[pl.BlockSpec((1, tm, d), row), pl.BlockSpec((1, tm, d), row),
                  pl.BlockSpec((1, d), lambda bi, i: (0, 0)), pl.BlockSpec((1, 1, d), mod_map)],
        out_specs=pl.BlockSpec((1, tm, d), row),
        compiler_params=_cparams(("parallel", "parallel")),
        name="ffn_residual",
    )(x, y, g.reshape(1, d), gt)


ROUTE_VALS = 16
ROUTE_TOK_SPLIT = 64


def _count(mask):
    return jnp.sum(jnp.where(mask, 1.0, 0.0), axis=1, keepdims=True)


def _route_kernel(h_ref, wr_ref, idx_ref, gate_ref, logit_ref, aff_ref, posm_ref, *, cap, tn):
    i = pl.program_id(1)
    n_exp, n_tok = logit_ref.shape
    h = h_ref[0]
    h_hi = h.astype(BF16)
    h_lo = (h - h_hi.astype(F32)).astype(BF16)
    w = wr_ref[...]
    w_hi = w.astype(BF16)
    w_lo = (w - w_hi.astype(F32)).astype(BF16)
    logit_ref[:, pl.ds(pl.multiple_of(i * tn, tn), tn)] = (
        _dot_nt(w_hi, h_hi) + (_dot_nt(w_hi, h_lo) + _dot_nt(w_lo, h_hi)))

    @pl.when(i == pl.num_programs(1) - 1)
    def _():
        lg = logit_ref[...]
        ex = jnp.exp(lg - jnp.max(lg, axis=0, keepdims=True))
        aff = ex / jnp.sum(ex, axis=0, keepdims=True)
        aff_ref[...] = aff
        capf = float(cap)

        def tbody(it, t):
            cand = t | jnp.left_shift(jnp.int32(1), 30 - it)
            cnt = _count(aff >= lax.bitcast_convert_type(cand, F32))
            return jnp.where(cnt >= capf, cand, t)

        t = lax.fori_loop(0, 31, tbody, jnp.zeros((n_exp, 1), jnp.int32))
        above = aff >= lax.bitcast_convert_type(t + 1, F32)
        tied = jnp.logical_and(aff >= lax.bitcast_convert_type(t, F32), jnp.logical_not(above))
        need = capf - _count(above)
        tok = lax.broadcasted_iota(jnp.int32, (1, n_tok), 1)
        nbits = n_tok.bit_length()

        def mbody(it, bound):
            cand = bound | jnp.left_shift(jnp.int32(1), nbits - 1 - it)
            f = _count(jnp.logical_and(tied, tok < cand))
            return jnp.where(f <= need, cand, bound)

        bound = lax.fori_loop(0, nbits, mbody, jnp.zeros((n_exp, 1), jnp.int32))
        sel = jnp.logical_or(above, jnp.logical_and(tied, tok < bound))
        self = jnp.where(sel, 1.0, 0.0)
        csum = self
        shift = 1
        while shift < n_tok:
            csum = csum + jnp.where(tok >= shift, pltpu.roll(csum, shift, 1), 0.0)
            shift *= 2
        posm_ref[...] = jnp.where(sel, csum - self, -1.0)

        tok_hi = (tok // ROUTE_TOK_SPLIT).astype(F32)
        tok_lo = (tok % ROUTE_TOK_SPLIT).astype(F32)
        zeros = jnp.zeros((ROUTE_VALS - 5, n_tok), F32)
        jc = min(cap, 128)

        def ebody(e, carry):
            prow = posm_ref[pl.ds(e, 1), :]
            a = aff_ref[pl.ds(e, 1), :]
            a_hi = a.astype(BF16).astype(F32)
            a_mid = (a - a_hi).astype(BF16).astype(F32)
            a_lo = (a - a_hi) - a_mid
            vals = jnp.concatenate([tok_hi, tok_lo, a_hi, a_mid, a_lo, zeros], axis=0).astype(BF16)
            for c in range(cap // jc):
                slot = (lax.broadcasted_iota(jnp.int32, (jc, 1), 0) + c * jc).astype(F32)
                onehot = jnp.where(prow == slot, 1.0, 0.0).astype(BF16)
                r = _dot_nt(onehot, vals)
                idx_ref[0, e, c * jc:(c + 1) * jc, :] = (
                    r[:, 0:1] * float(ROUTE_TOK_SPLIT) + r[:, 1:2]).astype(jnp.int32)
                gate_ref[0, e, c * jc:(c + 1) * jc, :] = r[:, 2:3] + (r[:, 3:4] + r[:, 4:5])
            return carry

        lax.fori_loop(0, n_exp, ebody, 0)


def _route(h2, w_router, cap):
    b, n, d = h2.shape
    e = w_router.shape[1]
    tn = min(n, 1024)
    return pl.pallas_call(
        functools.partial(_route_kernel, cap=cap, tn=tn),
        out_shape=(jax.ShapeDtypeStruct((b, e, cap, 1), jnp.int32),
                   jax.ShapeDtypeStruct((b, e, cap, 1), F32)),
        grid=(b, n // tn),
        in_specs=[pl.BlockSpec((1, tn, d), lambda bi, i: (bi, i, 0)),
                  pl.BlockSpec((e, d), lambda bi, i: (0, 0))],
        out_specs=(pl.BlockSpec((1, e, cap, 1), lambda bi, i: (bi, 0, 0, 0)),
                   pl.BlockSpec((1, e, cap, 1), lambda bi, i: (bi, 0, 0, 0))),
        scratch_shapes=[pltpu.VMEM((e, n), F32), pltpu.VMEM((e, n), F32), pltpu.VMEM((e, n), F32)],
        compiler_params=_cparams(("parallel", "arbitrary")),
        name="moe_route",
    )(h2, w_router.T)


def _moe(h2, w_router, wg, wu, wd):
    b, n, d = h2.shape
    e = w_router.shape[1]
    cap = n * CAPACITY_FACTOR // e
    idx, gates = _route(h2, w_router, cap)
    idx = idx.reshape(b, e, cap)
    xs = _moe_gather(h2, idx)
    y = _moe_ffn(xs.reshape(e, b * cap, d), gates.transpose(1, 0, 2, 3).reshape(e, b * cap, 1),
                 wg, wu, wd)
    return _moe_combine(y.reshape(e, b, cap, d), idx, n)


def _swap_rope_halves(w, half):
    d, n = w.shape
    return w.reshape(d, n // (2 * half), 2, half)[:, :, ::-1, :].reshape(d, n)


def _inproj_weights(w):
    d = w.shape[0]
    sec = np.cumsum([0, 384, 128, 128, 384, 384, 384, 256, 256, 256])
    aq, ak, av, bq, bk, bv, cq, ck, cv = [w[:, sec[i]:sec[i + 1]] for i in range(9)]
    aq = aq.reshape(d, A_KV_HEADS, A_Q_PER_KV, HEAD_DIM).transpose(0, 2, 1, 3).reshape(d, -1)
    rope = [aq, ak, cq, ck]
    halves = [HEAD_DIM // 4, HEAD_DIM // 4, C_QK_DIM // 4, C_QK_DIM // 4]
    partner = [_swap_rope_halves(m, h) for m, h in zip(rope, halves)]
    return jnp.concatenate(rope + [av, bq * HEAD_DIM ** -0.5, bk, bv, cv] + partner, axis=1).astype(BF16)


def _outproj_weights(w):
    d = w.shape[1]
    wa = A_Q_HEADS * HEAD_DIM
    a = w[:wa].reshape(A_KV_HEADS, A_Q_PER_KV, HEAD_DIM, d).transpose(1, 0, 2, 3).reshape(wa, d)
    return jnp.concatenate([a, w[wa:]], axis=0).astype(BF16)


def _rope_tables(s, scale_c):
    t = jnp.arange(s)
    rows, cols = t // GRID_W, t % GRID_W

    def head_tables(d):
        q = d // 4
        inv = ROPE_THETA ** (-jnp.arange(q, dtype=F32) / q)
        cs, sn = [], []
        for pos in (rows, cols):
            ang = pos.astype(F32)[:, None] * inv[None, :]
            c, s_ = jnp.cos(ang), jnp.sin(ang)
            cs += [c, c]
            sn += [-s_, s_]
        c = jnp.concatenate(cs, axis=1)
        s_ = jnp.concatenate(sn, axis=1)
        reps = LANES // d
        return jnp.tile(c, (1, reps)), jnp.tile(s_, (1, reps))

    ca, sa = head_tables(HEAD_DIM)
    cc, sc = head_tables(C_QK_DIM)
    scale_a = HEAD_DIM ** -0.5
    cos = jnp.concatenate([ca * scale_a, ca, cc * scale_c, cc], axis=1)
    sin = jnp.concatenate([sa * scale_a, sa, sc * scale_c, sc], axis=1)
    return cos, sin


def _ctx_tables(lc, scale_c):
    ones = jnp.ones((lc, LANES), F32)
    cos = jnp.concatenate([ones * HEAD_DIM ** -0.5, ones, ones * scale_c, ones], axis=1)
    return cos, jnp.zeros_like(cos)


def kernel(x, c, ctx, c_ctx, w_ada, b_ada, g_mix_pre, g_mix_post, g_ffn_pre, g_ffn_post, w_in, w_out,
           a_sink, b_rpb, c_lam_q1, c_lam_k1, c_lam_q2, c_lam_k2, c_subln, w_router, w_gate, w_up, w_down):
    b, s, d = x.shape
    lc = ctx.shape[1]
    depth = w_in.shape[0]
    scale_c = C_QK_DIM ** -0.5

    cos_x, sin_x = _rope_tables(s, scale_c)
    cos_c, sin_c = _ctx_tables(lc, scale_c)

    pad = (-(b + 1)) % 8
    cstack = jnp.concatenate([c, c_ctx[None, :], jnp.zeros((pad, d), F32)], axis=0)
    mod = _modulation(cstack, w_ada, b_ada)

    tm = 512
    h_ctx = ctx
    for l in range(depth):
        need_ctx = l < depth - 1
        mx = mod[l, :b].reshape(b, 1, N_MOD, d)
        sh1, sc1, gt1, sh2, sc2, gt2 = [mx[:, :, k] for k in range(N_MOD)]
        mc = mod[l, b:b + 1].reshape(1, 1, N_MOD, d)
        csh1, csc1, cgt1, csh2, csc2, cgt2 = [mc[:, :, k] for k in range(N_MOD)]

        w_cat = _inproj_weights(w_in[l])
        w_o = _outproj_weights(w_out[l])
        lam_init = 0.8 - 0.6 * math.exp(-0.3 * l)
        lam = (jnp.exp(jnp.sum(c_lam_q1[l] * c_lam_k1[l])) - jnp.exp(jnp.sum(c_lam_q2[l] * c_lam_k2[l]))
               + lam_init).reshape(1).astype(F32)
        subln2 = jnp.tile(c_subln[l], 2).reshape(1, LANES)
        bias = _b_bias_tables(b_rpb[l])
        wg, wu, wd = w_gate[l].astype(BF16), w_up[l].astype(BF16), w_down[l].astype(BF16)

        p = _inproj(x, g_mix_pre[l], sc1, sh1, cos_x, sin_x, w_cat, tm)
        pc = _inproj(h_ctx, g_mix_pre[l], csc1, csh1, cos_c, sin_c, w_cat, lc)

        oa = _attn_a(p, pc, a_sink[l], True)
        ob = _attn_b(p, pc, bias)
        od = _attn_c(p, pc, lam, subln2, lam_init, True)
        x1, h2 = _outproj(oa, ob, od, x, w_o, g_mix_post[l], gt1, g_ffn_pre[l], sc2, sh2, tm)
        if need_ctx:
            oac = _attn_a(p, pc, a_sink[l], False)
            obc = _attn_b_ctx(pc)
            odc = _attn_c(p, pc, lam, subln2, lam_init, False)
            c1, hc2 = _outproj(oac, obc, odc, h_ctx, w_o, g_mix_post[l], cgt1, g_ffn_pre[l], csc2, csh2, lc)
            yc = _moe(hc2, w_router[l], wg, wu, wd)
            h_ctx = _resid(c1, yc, g_ffn_post[l], cgt2, lc)
        y = _moe(h2, w_router[l], wg, wu, wd)
        x = _resid(x1, y, g_ffn_post[l], gt2, tm)
    return x
```

```python
import functools
import math

import numpy as np
import jax
import jax.numpy as jnp
from jax import lax
from jax.experimental import pallas as pl
from jax.experimental.pallas import tpu as pltpu

F32 = jnp.float32
BF16 = jnp.bfloat16

GRID_W = 64
HEAD_DIM = 64
A_Q_HEADS, A_KV_HEADS = 6, 2
A_Q_PER_KV = A_Q_HEADS // A_KV_HEADS
A_WINDOW = 128
A_BLOCK = 128
B_HEADS, B_MAX_ROWS, B_COLS = 6, 8, 16
C_HEADS, C_QK_DIM, C_V_DIM = 4, 32, 64
N_EXPERTS = 16
CAPACITY_FACTOR = 2
N_MOD = 6
ROPE_THETA = 10000.0
EPS = 1e-6
NEG_INF = -1e30

LANES = 128
ROPE_W = 1024
OFF_AQ, OFF_AK, OFF_CQ, OFF_CK, OFF_AV, OFF_BQ, OFF_BK, OFF_BV, OFF_CV = (
    0, 384, 512, 768, 1024, 1152, 1536, 1920, 2304)
IN_W = 2560
VMEM_LIMIT = 56 * 1024 * 1024


def _cparams(sem):
    return pltpu.CompilerParams(dimension_semantics=sem, vmem_limit_bytes=VMEM_LIMIT)


def _dot(a, b):
    return jnp.dot(a, b, preferred_element_type=F32)


def _dot_nt(a, b):
    return lax.dot_general(a, b, (((1,), (1,)), ((), ())), preferred_element_type=F32)


def _rms(x, g):
    return x * lax.rsqrt(jnp.mean(x * x, axis=-1, keepdims=True) + EPS) * g


def _mod_kernel(c_ref, w_ref, b_ref, o_ref):
    c = c_ref[...]
    a = c / (1.0 + jnp.exp(-c))
    o_ref[0] = jnp.dot(a, w_ref[0], preferred_element_type=F32,
                       precision=lax.Precision.HIGHEST) + b_ref[0]


def _modulation(cstack, w_ada, b_ada):
    nl, d, n = w_ada.shape
    rows = cstack.shape[0]
    tn = 1536
    return pl.pallas_call(
        _mod_kernel,
        out_shape=jax.ShapeDtypeStruct((nl, rows, n), F32),
        grid=(nl, n // tn),
        in_specs=[
            pl.BlockSpec((rows, d), lambda l, j: (0, 0)),
            pl.BlockSpec((1, d, tn), lambda l, j: (l, 0, j)),
            pl.BlockSpec((1, 1, tn), lambda l, j: (l, 0, j)),
        ],
        out_specs=pl.BlockSpec((1, rows, tn), lambda l, j: (l, 0, j)),
        compiler_params=_cparams(("parallel", "parallel")),
        name="adaln_modulation",
    )(cstack, w_ada, b_ada.reshape(nl, 1, n))


_ROPE_TABLE_OF_UNIT = (0, 0, 0, 1, 2, 2, 3, 3)


def _inproj_kernel(x_ref, g_ref, sc_ref, sh_ref, cos_ref, sin_ref, w_ref, o_ref):
    x = x_ref[0]
    h = _rms(x, g_ref[...]) * (1.0 + sc_ref[0]) + sh_ref[0]
    hb = h.astype(BF16)
    for j in range(ROPE_W // 256):
        t0, t1 = _ROPE_TABLE_OF_UNIT[2 * j], _ROPE_TABLE_OF_UNIT[2 * j + 1]
        cos = jnp.concatenate([cos_ref[:, t0 * LANES:(t0 + 1) * LANES],
                               cos_ref[:, t1 * LANES:(t1 + 1) * LANES]], axis=1)
        sin = jnp.concatenate([sin_ref[:, t0 * LANES:(t0 + 1) * LANES],
                               sin_ref[:, t1 * LANES:(t1 + 1) * LANES]], axis=1)
        p = _dot(hb, w_ref[:, 256 * j:256 * (j + 1)])
        ps = _dot(hb, w_ref[:, IN_W + 256 * j:IN_W + 256 * (j + 1)])
        o_ref[0, :, 256 * j:256 * (j + 1)] = (p * cos + ps * sin).astype(BF16)
    for j in range(ROPE_W // 256, IN_W // 256):
        o_ref[0, :, 256 * j:256 * (j + 1)] = _dot(hb, w_ref[:, 256 * j:256 * (j + 1)]).astype(BF16)


def _inproj(x, g, sc, sh, cos, sin, w_cat, tm):
    b, s, d = x.shape
    per_sample = sc.shape[0] > 1
    mod_map = (lambda bi, i: (bi, 0, 0)) if per_sample else (lambda bi, i: (0, 0, 0))
    return pl.pallas_call(
        _inproj_kernel,
        out_shape=jax.ShapeDtypeStruct((b, s, IN_W), BF16),
        grid=(b, s // tm),
        in_specs=[
            pl.BlockSpec((1, tm, d), lambda bi, i: (bi, i, 0)),
            pl.BlockSpec((1, d), lambda bi, i: (0, 0)),
            pl.BlockSpec((1, 1, d), mod_map),
            pl.BlockSpec((1, 1, d), mod_map),
            pl.BlockSpec((tm, 4 * LANES), lambda bi, i: (i, 0)),
            pl.BlockSpec((tm, 4 * LANES), lambda bi, i: (i, 0)),
            pl.BlockSpec((d, IN_W + ROPE_W), lambda bi, i: (0, 0)),
        ],
        out_specs=pl.BlockSpec((1, tm, IN_W), lambda bi, i: (bi, i, 0)),
        compiler_params=_cparams(("parallel", "parallel")),
        name="norm_inproj_rope",
    )(x, g.reshape(1, d), sc, sh, cos, sin, w_cat)


def _lo_mask():
    return lax.broadcasted_iota(jnp.int32, (1, LANES), 1) < HEAD_DIM


def _split_halves(q):
    lo = _lo_mask()
    zero = jnp.zeros_like(q)
    return jnp.where(lo, q, zero), jnp.where(lo, zero, q)


A_STEP_BLOCKS = 2


def _attn_a_kernel(sink_ref, q_ref, *refs, seq, latent):
    if latent:
        k_ref, v_ref, kc_ref, vc_ref, o_ref = refs
    else:
        kc_ref, vc_ref, o_ref = refs
    tq = A_BLOCK if latent else q_ref.shape[1]
    npair = A_Q_HEADS // 2
    kc = kc_ref[0]
    vc = vc_ref[0]
    lo = _lo_mask()
    blocks = []
    for sb in range(q_ref.shape[1] // tq):
        rows = slice(sb * tq, (sb + 1) * tq)
        los, his = [], []
        for t in range(npair):
            a, b = _split_halves(q_ref[0, rows, t * LANES:(t + 1) * LANES])
            los.append(a)
            his.append(b)
        q6 = jnp.concatenate(los + his, axis=0)
        sink = jnp.concatenate([jnp.full((tq, 1), sink_ref[j], F32) for j in range(A_Q_HEADS)], axis=0)
        s_ctx = _dot_nt(q6, kc)
        m = jnp.maximum(jnp.max(s_ctx, axis=-1, keepdims=True), sink)
        if latent:
            i = pl.program_id(1) * (q_ref.shape[1] // tq) + sb
            band = 3 * A_BLOCK
            start = pl.multiple_of(jnp.clip((i - 1) * A_BLOCK, 0, seq - band), A_BLOCK)
            kb = k_ref[0, pl.ds(start, band), :]
            vb = v_ref[0, pl.ds(start, band), :]
            s_lat = _dot_nt(q6, kb)
            row = lax.broadcasted_iota(jnp.int32, (A_Q_HEADS * tq, 1), 0)
            qpos = i * tq + (row & (tq - 1))
            kpos = start + lax.broadcasted_iota(jnp.int32, (1, band), 1)
            s_lat = jnp.where(jnp.abs(qpos - kpos) <= A_WINDOW, s_lat, NEG_INF)
            m = jnp.maximum(m, jnp.max(s_lat, axis=-1, keepdims=True))
            e_lat = jnp.exp(s_lat - m)
        e_ctx = jnp.exp(s_ctx - m)
        l = jnp.sum(e_ctx, axis=-1, keepdims=True) + jnp.exp(sink - m)
        o = _dot(e_ctx.astype(BF16), vc)
        if latent:
            l = l + jnp.sum(e_lat, axis=-1, keepdims=True)
            o = o + _dot(e_lat.astype(BF16), vb)
        o = o / l
        blocks.append(jnp.concatenate(
            [jnp.where(lo, o[t * tq:(t + 1) * tq], o[(npair + t) * tq:(npair + t + 1) * tq])
             for t in range(npair)], axis=1))
    o_ref[0] = jnp.concatenate(blocks, axis=0).astype(BF16)


def _attn_a(p, pc, sink, latent):
    src = p if latent else pc
    b, s, _ = src.shape
    lc = pc.shape[1]
    tq = A_STEP_BLOCKS * A_BLOCK if latent else s
    wq = A_Q_HEADS * HEAD_DIM
    in_specs = [pl.BlockSpec(memory_space=pltpu.SMEM),
                pl.BlockSpec((1, tq, wq), lambda bi, i: (bi, i, OFF_AQ // wq))]
    args = [sink, src]
    if latent:
        in_specs += [pl.BlockSpec((1, s, LANES), lambda bi, i: (bi, 0, OFF_AK // LANES)),
                     pl.BlockSpec((1, s, LANES), lambda bi, i: (bi, 0, OFF_AV // LANES))]
        args += [p, p]
    in_specs += [pl.BlockSpec((1, lc, LANES), lambda bi, i: (bi, 0, OFF_AK // LANES)),
                 pl.BlockSpec((1, lc, LANES), lambda bi, i: (bi, 0, OFF_AV // LANES))]
    args += [pc, pc]
    return pl.pallas_call(
        functools.partial(_attn_a_kernel, seq=s, latent=latent),
        out_shape=jax.ShapeDtypeStruct((b, s, wq), BF16),
        grid=(b, s // tq),
        in_specs=in_specs,
        out_specs=pl.BlockSpec((1, tq, wq), lambda bi, i: (bi, i, 0)),
        compiler_params=_cparams(("parallel", "arbitrary")),
        name="attn_window_gqa" + ("" if latent else "_ctx"),
    )(*args)


B_ROWS_PER_STEP = 8
B_ROW_UNROLL = 2


def _attn_b_softmax_out(q2, s_parts, v_parts):
    m = functools.reduce(jnp.maximum, [jnp.max(s, axis=-1, keepdims=True) for s in s_parts])
    es = [jnp.exp(s - m) for s in s_parts]
    l = functools.reduce(jnp.add, [jnp.sum(e, axis=-1, keepdims=True) for e in es])
    o = functools.reduce(jnp.add, [_dot(e.astype(BF16), v) for e, v in zip(es, v_parts)])
    return o / l


def _attn_b_kernel(q_ref, k_ref, v_ref, kc_ref, vc_ref, bias_ref, o_ref, *, rows_n):
    i = pl.program_id(1)
    npair = B_HEADS // 2
    lo = _lo_mask()
    nk = B_MAX_ROWS * GRID_W

    def one_row(rr):
        r = i * B_ROWS_PER_STEP + rr
        rs = jnp.clip(r - B_MAX_ROWS // 2, 0, rows_n - B_MAX_ROWS)
        off = r - rs
        kstart = pl.multiple_of(rs * GRID_W, GRID_W)
        qstart = pl.multiple_of(rr * GRID_W, GRID_W)
        outs = []
        for t in range(npair):
            cols = slice(t * LANES, (t + 1) * LANES)
            qa, qb = _split_halves(q_ref[0, pl.ds(qstart, GRID_W), cols])
            q2 = jnp.concatenate([qa, qb], axis=0)
            s_lat = _dot_nt(q2, k_ref[0, pl.ds(kstart, nk), cols]) + bias_ref[off, t]
            s_ctx = _dot_nt(q2, kc_ref[0, :, cols])
            o = _attn_b_softmax_out(q2, [s_lat, s_ctx],
                                    [v_ref[0, pl.ds(kstart, nk), cols], vc_ref[0, :, cols]])
            outs.append(jnp.where(lo, o[:GRID_W], o[GRID_W:]))
        return jnp.concatenate(outs, axis=1)

    def body(it, carry):
        rows = [one_row(it * B_ROW_UNROLL + u) for u in range(B_ROW_UNROLL)]
        qstart = pl.multiple_of(it * (B_ROW_UNROLL * GRID_W), B_ROW_UNROLL * GRID_W)
        o_ref[0, pl.ds(qstart, B_ROW_UNROLL * GRID_W), :] = jnp.concatenate(rows, axis=0).astype(BF16)
        return carry

    lax.fori_loop(0, B_ROWS_PER_STEP // B_ROW_UNROLL, body, 0)


def _attn_b_ctx_kernel(q_ref, kc_ref, vc_ref, o_ref):
    npair = B_HEADS // 2
    lo = _lo_mask()
    n = q_ref.shape[1]
    for t in range(npair):
        cols = slice(t * LANES, (t + 1) * LANES)
        qa, qb = _split_halves(q_ref[0, :, cols])
        q2 = jnp.concatenate([qa, qb], axis=0)
        s_ctx = _dot_nt(q2, kc_ref[0, :, cols])
        o = _attn_b_softmax_out(q2, [s_ctx], [vc_ref[0, :, cols]])
        o_ref[0, :, cols] = jnp.where(lo, o[:n], o[n:]).astype(BF16)


def _attn_b(p, pc, bias):
    b, s, _ = p.shape
    lc = pc.shape[1]
    w = B_HEADS * HEAD_DIM
    rows_n = s // GRID_W
    tq = B_ROWS_PER_STEP * GRID_W
    return pl.pallas_call(
        functools.partial(_attn_b_kernel, rows_n=rows_n),
        out_shape=jax.ShapeDtypeStruct((b, s, w), BF16),
        grid=(b, s // tq),
        in_specs=[
            pl.BlockSpec((1, tq, w), lambda bi, i: (bi, i, OFF_BQ // w)),
            pl.BlockSpec((1, s, w), lambda bi, i: (bi, 0, OFF_BK // w)),
            pl.BlockSpec((1, s, w), lambda bi, i: (bi, 0, OFF_BV // w)),
            pl.BlockSpec((1, lc, w), lambda bi, i: (bi, 0, OFF_BK // w)),
            pl.BlockSpec((1, lc, w), lambda bi, i: (bi, 0, OFF_BV // w)),
            pl.BlockSpec(bias.shape, lambda bi, i: (0, 0, 0, 0)),
        ],
        out_specs=pl.BlockSpec((1, tq, w), lambda bi, i: (bi, i, 0)),
        compiler_params=_cparams(("parallel", "arbitrary")),
        name="attn_neighbourhood",
    )(p, p, p, pc, pc, bias)


def _attn_b_ctx(pc):
    b, lc, _ = pc.shape
    w = B_HEADS * HEAD_DIM
    return pl.pallas_call(
        _attn_b_ctx_kernel,
        out_shape=jax.ShapeDtypeStruct((b, lc, w), BF16),
        grid=(b,),
        in_specs=[
            pl.BlockSpec((1, lc, w), lambda bi: (bi, 0, OFF_BQ // w)),
            pl.BlockSpec((1, lc, w), lambda bi: (bi, 0, OFF_BK // w)),
            pl.BlockSpec((1, lc, w), lambda bi: (bi, 0, OFF_BV // w)),
        ],
        out_specs=pl.BlockSpec((1, lc, w), lambda bi: (bi, 0, 0)),
        compiler_params=_cparams(("parallel",)),
        name="attn_neighbourhood_ctx",
    )(pc, pc, pc)


def _b_bias_tables(rpb):
    col = np.arange(GRID_W)
    cstart = np.clip(col - B_COLS // 2, 0, GRID_W - B_COLS)
    col_ok = (col[None, :] >= cstart[:, None]) & (col[None, :] < cstart[:, None] + B_COLS)
    dc_idx = np.clip(col[None, :] - col[:, None], -(B_COLS - 1), B_COLS - 1) + (B_COLS - 1)
    pick_col = (dc_idx[None] == np.arange(2 * B_COLS - 1)[:, None, None]).astype(np.float32)
    rsel = jnp.stack([rpb[:, B_MAX_ROWS - 1 - off:2 * B_MAX_ROWS - 1 - off, :].astype(F32)
                      for off in range(B_MAX_ROWS)], axis=0)
    t = jnp.einsum('ohjc,cqk->ohqjk', rsel, pick_col, precision=lax.Precision.HIGHEST)
    t = jnp.where(col_ok[None, None, :, None, :], t, NEG_INF)
    return t.reshape(B_MAX_ROWS, B_HEADS // 2, 2 * GRID_W, B_MAX_ROWS * GRID_W)


C_KEY_CHUNK = 1024


def _attn_c_kernel(lam_ref, q_ref, *refs, latent, out_scale):
    if latent:
        k_ref, v_ref, kc_ref, vc_ref, g_ref, o_ref = refs
    else:
        kc_ref, vc_ref, g_ref, o_ref = refs
    tq = q_ref.shape[1]
    lam = lam_ref[0]
    q = q_ref[0]
    quarter = lax.broadcasted_iota(jnp.int32, (1, LANES), 1) // C_QK_DIM
    zero = jnp.zeros_like(q)
    q4 = jnp.concatenate([jnp.where(quarter == j, q, zero) for j in range(4)], axis=0)
    chunks = []
    if latent:
        kc_n = min(C_KEY_CHUNK, k_ref.shape[1])
        chunks += [(k_ref, v_ref, c * kc_n, kc_n) for c in range(k_ref.shape[1] // kc_n)]
    chunks.append((kc_ref, vc_ref, 0, kc_ref.shape[1]))
    m = l = acc = None
    for kr, vr, st, n in chunks:
        s = _dot_nt(q4, kr[0, st:st + n, :])
        mc = jnp.max(s, axis=-1, keepdims=True)
        if m is None:
            m = mc
            e = jnp.exp2(s - m)
            l = jnp.sum(e, axis=-1, keepdims=True)
            acc = _dot(e.astype(BF16), vr[0, st:st + n, :])
        else:
            m_new = jnp.maximum(m, mc)
            alpha = jnp.exp2(m - m_new)
            e = jnp.exp2(s - m_new)
            l = l * alpha + jnp.sum(e, axis=-1, keepdims=True)
            acc = acc * alpha + _dot(e.astype(BF16), vr[0, st:st + n, :])
            m = m_new
    o4 = acc / l
    outs = [o4[2 * h * tq:(2 * h + 1) * tq] - lam * o4[(2 * h + 1) * tq:(2 * h + 2) * tq]
            for h in range(2)]
    lo = _lo_mask()
    o = jnp.where(lo, outs[0], outs[1])
    sq = o * o
    s_lo = jnp.sum(jnp.where(lo, sq, 0.0), axis=-1, keepdims=True)
    s_hi = jnp.sum(jnp.where(lo, 0.0, sq), axis=-1, keepdims=True)
    ms = jnp.where(lo, s_lo, s_hi) * (1.0 / C_V_DIM)
    o_ref[0] = (o * lax.rsqrt(ms + EPS) * g_ref[...] * out_scale).astype(BF16)


def _attn_c(p, pc, lam, subln2, lam_init, latent):
    src = p if latent else pc
    b, s, _ = src.shape
    lc = pc.shape[1]
    tq = 128 if latent else s
    npair = C_HEADS // 2
    in_specs = [pl.BlockSpec(memory_space=pltpu.SMEM),
                pl.BlockSpec((1, tq, LANES), lambda bi, hp, i: (bi, i, OFF_CQ // LANES + hp))]
    args = [lam, src]
    if latent:
        in_specs += [pl.BlockSpec((1, s, LANES), lambda bi, hp, i: (bi, 0, OFF_CK // LANES + hp)),
                     pl.BlockSpec((1, s, LANES), lambda bi, hp, i: (bi, 0, OFF_CV // LANES + hp))]
        args += [p, p]
    in_specs += [pl.BlockSpec((1, lc, LANES), lambda bi, hp, i: (bi, 0, OFF_CK // LANES + hp)),
                 pl.BlockSpec((1, lc, LANES), lambda bi, hp, i: (bi, 0, OFF_CV // LANES + hp)),
                 pl.BlockSpec((1, LANES), lambda bi, hp, i: (0, 0))]
    args += [pc, pc, subln2]
    return pl.pallas_call(
        functools.partial(_attn_c_kernel, latent=latent, out_scale=1.0 - lam_init),
        out_shape=jax.ShapeDtypeStruct((b, s, C_HEADS * C_V_DIM), BF16),
        grid=(b, npair, s // tq),
        in_specs=in_specs,
        out_specs=pl.BlockSpec((1, tq, LANES), lambda bi, hp, i: (bi, i, hp)),
        compiler_params=_cparams(("parallel", "parallel", "arbitrary")),
        name="attn_differential" + ("" if latent else "_ctx"),
    )(*args)


def _outproj_kernel(oa_ref, ob_ref, oc_ref, x_ref, w_ref, gpost_ref, gt_ref, gpre_ref, sc_ref, sh_ref,
                    x1_ref, h2_ref):
    wa = oa_ref.shape[2]
    wb = ob_ref.shape[2]
    y = (_dot(oa_ref[0], w_ref[0:wa, :]) + _dot(ob_ref[0], w_ref[wa:wa + wb, :])
         + _dot(oc_ref[0], w_ref[wa + wb:, :]))
    x1 = x_ref[0] + gt_ref[0] * _rms(y, gpost_ref[...])
    x1_ref[0] = x1
    h2_ref[0] = _rms(x1, gpre_ref[...]) * (1.0 + sc_ref[0]) + sh_ref[0]


def _outproj(oa, ob, oc, x, w_out, g_post, gt, g_pre, sc, sh, tm):
    b, s, d = x.shape
    per_sample = gt.shape[0] > 1
    mod_map = (lambda bi, i: (bi, 0, 0)) if per_sample else (lambda bi, i: (0, 0, 0))
    row = lambda bi, i: (bi, i, 0)
    const2 = lambda bi, i: (0, 0)
    return pl.pallas_call(
        _outproj_kernel,
        out_shape=(jax.ShapeDtypeStruct((b, s, d), F32), jax.ShapeDtypeStruct((b, s, d), F32)),
        grid=(b, s // tm),
        in_specs=[
            pl.BlockSpec((1, tm, oa.shape[2]), row),
            pl.BlockSpec((1, tm, ob.shape[2]), row),
            pl.BlockSpec((1, tm, oc.shape[2]), row),
            pl.BlockSpec((1, tm, d), row),
            pl.BlockSpec(w_out.shape, const2),
            pl.BlockSpec((1, d), const2),
            pl.BlockSpec((1, 1, d), mod_map),
            pl.BlockSpec((1, d), const2),
            pl.BlockSpec((1, 1, d), mod_map),
            pl.BlockSpec((1, 1, d), mod_map),
        ],
        out_specs=(pl.BlockSpec((1, tm, d), row), pl.BlockSpec((1, tm, d), row)),
        compiler_params=_cparams(("parallel", "parallel")),
        name="outproj_residual_norm",
    )(oa, ob, oc, x, w_out, g_post.reshape(1, d), gt, g_pre.reshape(1, d), sc, sh)


ROW_UNROLL = 8


def _gather_kernel(idx_ref, h_ref, o_ref, buf_ref):
    cap = buf_ref.shape[0]

    def body(j, carry):
        base = pl.multiple_of(j * ROW_UNROLL, ROW_UNROLL)
        for u in range(ROW_UNROLL):
            n = idx_ref[0, 0, base + u]
            buf_ref[pl.ds(base + u, 1), :] = h_ref[0, pl.ds(n, 1), :]
        return carry

    lax.fori_loop(0, cap // ROW_UNROLL, body, 0)
    o_ref[0, 0] = buf_ref[...].astype(BF16)


def _moe_gather(h2, idx):
    b, n, d = h2.shape
    e, cap = idx.shape[1], idx.shape[2]
    return pl.pallas_call(
        _gather_kernel,
        out_shape=jax.ShapeDtypeStruct((e, b, cap, d), BF16),
        grid=(b, e),
        in_specs=[
            pl.BlockSpec((1, 1, cap), lambda bi, ei: (bi * e + ei, 0, 0), memory_space=pltpu.SMEM),
            pl.BlockSpec((1, n, d), lambda bi, ei: (bi, 0, 0)),
        ],
        out_specs=pl.BlockSpec((1, 1, cap, d), lambda bi, ei: (ei, bi, 0, 0)),
        scratch_shapes=[pltpu.VMEM((cap, d), F32)],
        compiler_params=_cparams(("parallel", "arbitrary")),
        name="moe_gather",
    )(idx.reshape(b * e, 1, cap), h2)


FF_CHUNK = 512


def _ffn_kernel(xs_ref, gate_ref, wg_ref, wu_ref, wd_ref, o_ref):
    xs = xs_ref[0]
    ff = wg_ref.shape[2]
    acc = None
    for c in range(ff // FF_CHUNK):
        cs = slice(c * FF_CHUNK, (c + 1) * FF_CHUNK)
        g = _dot(xs, wg_ref[0, :, cs])
        u = _dot(xs, wu_ref[0, :, cs])
        hid = ((g / (1.0 + jnp.exp(-g))) * u).astype(BF16)
        part = _dot(hid, wd_ref[0, cs, :])
        acc = part if acc is None else acc + part
    o_ref[0] = acc * gate_ref[0]


FFN_ROWS = 512


def _moe_ffn(xs, gates, wg, wu, wd):
    e, r, d = xs.shape
    ff = wg.shape[2]
    tr = min(r, FFN_ROWS)
    return pl.pallas_call(
        _ffn_kernel,
        out_shape=jax.ShapeDtypeStruct((e, r, d), F32),
        grid=(e, r // tr),
        in_specs=[
            pl.BlockSpec((1, tr, d), lambda ei, ri: (ei, ri, 0)),
            pl.BlockSpec((1, tr, 1), lambda ei, ri: (ei, ri, 0)),
            pl.BlockSpec((1, d, ff), lambda ei, ri: (ei, 0, 0)),
            pl.BlockSpec((1, d, ff), lambda ei, ri: (ei, 0, 0)),
            pl.BlockSpec((1, ff, d), lambda ei, ri: (ei, 0, 0)),
        ],
        out_specs=pl.BlockSpec((1, tr, d), lambda ei, ri: (ei, ri, 0)),
        compiler_params=_cparams(("parallel", "arbitrary")),
        name="moe_expert_ffn",
    )(xs, gates, wg, wu, wd)


def _combine_kernel(idx_ref, y_ref, o_ref):
    cap = y_ref.shape[2]

    @pl.when(pl.program_id(1) == 0)
    def _():
        o_ref[...] = jnp.zeros_like(o_ref)

    def body(j, carry):
        base = pl.multiple_of(j * ROW_UNROLL, ROW_UNROLL)
        toks = [idx_ref[0, 0, base + u] for u in range(ROW_UNROLL)]
        ys = y_ref[0, 0, pl.ds(base, ROW_UNROLL), :]
        rows = [o_ref[0, pl.ds(n, 1), :] for n in toks]
        for u, n in enumerate(toks):
            o_ref[0, pl.ds(n, 1), :] = rows[u] + ys[u:u + 1]
        return carry

    lax.fori_loop(0, cap // ROW_UNROLL, body, 0)


def _moe_combine(y, idx, n):
    e, b, cap, d = y.shape
    return pl.pallas_call(
        _combine_kernel,
        out_shape=jax.ShapeDtypeStruct((b, n, d), F32),
        grid=(b, e),
        in_specs=[
            pl.BlockSpec((1, 1, cap), lambda bi, ei: (bi * e + ei, 0, 0), memory_space=pltpu.SMEM),
            pl.BlockSpec((1, 1, cap, d), lambda bi, ei: (ei, bi, 0, 0)),
        ],
        out_specs=pl.BlockSpec((1, n, d), lambda bi, ei: (bi, 0, 0)),
        compiler_params=_cparams(("parallel", "arbitrary")),
        name="moe_combine",
    )(idx.reshape(b * e, 1, cap), y)


def _resid_kernel(x_ref, y_ref, g_ref, gt_ref, o_ref):
    o_ref[0] = x_ref[0] + gt_ref[0] * _rms(y_ref[0], g_ref[...])


def _resid(x, y, g, gt, tm):
    b, s, d = x.shape
    per_sample = gt.shape[0] > 1
    mod_map = (lambda bi, i: (bi, 0, 0)) if per_sample else (lambda bi, i: (0, 0, 0))
    row = lambda bi, i: (bi, i, 0)
    return pl.pallas_call(
        _resid_kernel,
        out_shape=jax.ShapeDtypeStruct((b, s, d), F32),
        grid=(b, s // tm),
        in_specs=[pl.BlockSpec((1, tm, d), row), pl.BlockSpec((1, tm, d), row),
                  pl.BlockSpec((1, d), lambda bi, i: (0, 0)), pl.BlockSpec((1, 1, d), mod_map)],
        out_specs=pl.BlockSpec((1, tm, d), row),
        compiler_params=_cparams(("parallel", "parallel")),
        name="ffn_residual",
    )(x, y, g.reshape(1, d), gt)


ROUTE_VALS = 16
ROUTE_TOK_SPLIT = 64


def _count(mask):
    return jnp.sum(jnp.where(mask, 1.0, 0.0), axis=1, keepdims=True)


def _route_kernel(h_ref, wr_ref, idx_ref, gate_ref, logit_ref, aff_ref, posm_ref, *, cap, tn):
    i = pl.program_id(1)
    n_exp, n_tok = logit_ref.shape
    h = h_ref[0]
    h_hi = h.astype(BF16)
    h_lo = (h - h_hi.astype(F32)).astype(BF16)
    w = wr_ref[...]
    w_hi = w.astype(BF16)
    w_lo = (w - w_hi.astype(F32)).astype(BF16)
    logit_ref[:, pl.ds(pl.multiple_of(i * tn, tn), tn)] = (
        _dot_nt(w_hi, h_hi) + (_dot_nt(w_hi, h_lo) + _dot_nt(w_lo, h_hi)))

    @pl.when(i == pl.num_programs(1) - 1)
    def _():
        lg = logit_ref[...]
        ex = jnp.exp(lg - jnp.max(lg, axis=0, keepdims=True))
        aff = ex / jnp.sum(ex, axis=0, keepdims=True)
        aff_ref[...] = aff
        capf = float(cap)

        def tbody(it, t):
            cand = t | jnp.left_shift(jnp.int32(1), 30 - it)
            cnt = _count(aff >= lax.bitcast_convert_type(cand, F32))
            return jnp.where(cnt >= capf, cand, t)

        t = lax.fori_loop(0, 31, tbody, jnp.zeros((n_exp, 1), jnp.int32))
        above = aff >= lax.bitcast_convert_type(t + 1, F32)
        tied = jnp.logical_and(aff >= lax.bitcast_convert_type(t, F32), jnp.logical_not(above))
        need = capf - _count(above)
        tok = lax.broadcasted_iota(jnp.int32, (1, n_tok), 1)
        nbits = n_tok.bit_length()

        def mbody(it, bound):
            cand = bound | jnp.left_shift(jnp.int32(1), nbits - 1 - it)
            f = _count(jnp.logical_and(tied, tok < cand))
            return jnp.where(f <= need, cand, bound)

        bound = lax.fori_loop(0, nbits, mbody, jnp.zeros((n_exp, 1), jnp.int32))
        sel = jnp.logical_or(above, jnp.logical_and(tied, tok < bound))
        self = jnp.where(sel, 1.0, 0.0)
        csum = self
        shift = 1
        while shift < n_tok:
            csum = csum + jnp.where(tok >= shift, pltpu.roll(csum, shift, 1), 0.0)
            shift *= 2
        posm_ref[...] = jnp.where(sel, csum - self, -1.0)

        tok_hi = (tok // ROUTE_TOK_SPLIT).astype(F32)
        tok_lo = (tok % ROUTE_TOK_SPLIT).astype(F32)
        zeros = jnp.zeros((ROUTE_VALS - 5, n_tok), F32)
        jc = min(cap, 128)

        def ebody(e, carry):
            prow = posm_ref[pl.ds(e, 1), :]
            a = aff_ref[pl.ds(e, 1), :]
            a_hi = a.astype(BF16).astype(F32)
            a_mid = (a - a_hi).astype(BF16).astype(F32)
            a_lo = (a - a_hi) - a_mid
            vals = jnp.concatenate([tok_hi, tok_lo, a_hi, a_mid, a_lo, zeros], axis=0).astype(BF16)
            for c in range(cap // jc):
                slot = (lax.broadcasted_iota(jnp.int32, (jc, 1), 0) + c * jc).astype(F32)
                onehot = jnp.where(prow == slot, 1.0, 0.0).astype(BF16)
                r = _dot_nt(onehot, vals)
                idx_ref[0, e, c * jc:(c + 1) * jc, :] = (
                    r[:, 0:1] * float(ROUTE_TOK_SPLIT) + r[:, 1:2]).astype(jnp.int32)
                gate_ref[0, e, c * jc:(c + 1) * jc, :] = r[:, 2:3] + (r[:, 3:4] + r[:, 4:5])
            return carry

        lax.fori_loop(0, n_exp, ebody, 0)


def _route(h2, w_router, cap):
    b, n, d = h2.shape
    e = w_router.shape[1]
    tn = min(n, 1024)
    return pl.pallas_call(
        functools.partial(_route_kernel, cap=cap, tn=tn),
        out_shape=(jax.ShapeDtypeStruct((b, e, cap, 1), jnp.int32),
                   jax.ShapeDtypeStruct((b, e, cap, 1), F32)),
        grid=(b, n // tn),
        in_specs=[pl.BlockSpec((1, tn, d), lambda bi, i: (bi, i, 0)),
                  pl.BlockSpec((e, d), lambda bi, i: (0, 0))],
        out_specs=(pl.BlockSpec((1, e, cap, 1), lambda bi, i: (bi, 0, 0, 0)),
                   pl.BlockSpec((1, e, cap, 1), lambda bi, i: (bi, 0, 0, 0))),
        scratch_shapes=[pltpu.VMEM((e, n), F32), pltpu.VMEM((e, n), F32), pltpu.VMEM((e, n), F32)],
        compiler_params=_cparams(("parallel", "arbitrary")),
        name="moe_route",
    )(h2, w_router.T)


def _moe(h2, w_router, wg, wu, wd):
    b, n, d = h2.shape
    e = w_router.shape[1]
    cap = n * CAPACITY_FACTOR // e
    idx, gates = _route(h2, w_router, cap)
    idx = idx.reshape(b, e, cap)
    xs = _moe_gather(h2, idx)
    y = _moe_ffn(xs.reshape(e, b * cap, d), gates.transpose(1, 0, 2, 3).reshape(e, b * cap, 1),
                 wg, wu, wd)
    return _moe_combine(y.reshape(e, b, cap, d), idx, n)


def _swap_rope_halves(w, half):
    d, n = w.shape
    return w.reshape(d, n // (2 * half), 2, half)[:, :, ::-1, :].reshape(d, n)


def _inproj_weights(w):
    d = w.shape[0]
    sec = np.cumsum([0, 384, 128, 128, 384, 384, 384, 256, 256, 256])
    aq, ak, av, bq, bk, bv, cq, ck, cv = [w[:, sec[i]:sec[i + 1]] for i in range(9)]
    aq = aq.reshape(d, A_KV_HEADS, A_Q_PER_KV, HEAD_DIM).transpose(0, 2, 1, 3).reshape(d, -1)
    rope = [aq, ak, cq, ck]
    halves = [HEAD_DIM // 4, HEAD_DIM // 4, C_QK_DIM // 4, C_QK_DIM // 4]
    partner = [_swap_rope_halves(m, h) for m, h in zip(rope, halves)]
    return jnp.concatenate(rope + [av, bq * HEAD_DIM ** -0.5, bk, bv, cv] + partner, axis=1).astype(BF16)


def _outproj_weights(w):
    d = w.shape[1]
    wa = A_Q_HEADS * HEAD_DIM
    a = w[:wa].reshape(A_KV_HEADS, A_Q_PER_KV, HEAD_DIM, d).transpose(1, 0, 2, 3).reshape(wa, d)
    return jnp.concatenate([a, w[wa:]], axis=0).astype(BF16)


def _rope_tables(s, scale_c):
    t = jnp.arange(s)
    rows, cols = t // GRID_W, t % GRID_W

    def head_tables(d):
        q = d // 4
        inv = ROPE_THETA ** (-jnp.arange(q, dtype=F32) / q)
        cs, sn = [], []
        for pos in (rows, cols):
            ang = pos.astype(F32)[:, None] * inv[None, :]
            c, s_ = jnp.cos(ang), jnp.sin(ang)
            cs += [c, c]
            sn += [-s_, s_]
        c = jnp.concatenate(cs, axis=1)
        s_ = jnp.concatenate(sn, axis=1)
        reps = LANES // d
        return jnp.tile(c, (1, reps)), jnp.tile(s_, (1, reps))

    ca, sa = head_tables(HEAD_DIM)
    cc, sc = head_tables(C_QK_DIM)
    scale_a = HEAD_DIM ** -0.5
    cos = jnp.concatenate([ca * scale_a, ca, cc * scale_c, cc], axis=1)
    sin = jnp.concatenate([sa * scale_a, sa, sc * scale_c, sc], axis=1)
    return cos, sin


def _ctx_tables(lc, scale_c):
    ones = jnp.ones((lc, LANES), F32)
    cos = jnp.concatenate([ones * HEAD_DIM ** -0.5, ones, ones * scale_c, ones], axis=1)
    return cos, jnp.zeros_like(cos)


def kernel(x, c, ctx, c_ctx, w_ada, b_ada, g_mix_pre, g_mix_post, g_ffn_pre, g_ffn_post, w_in, w_out,
           a_sink, b_rpb, c_lam_q1, c_lam_k1, c_lam_q2, c_lam_k2, c_subln, w_router, w_gate, w_up, w_down):
    b, s, d = x.shape
    lc = ctx.shape[1]
    depth = w_in.shape[0]
    scale_c = C_QK_DIM ** -0.5 * math.log2(math.e)

    cos_x, sin_x = _rope_tables(s, scale_c)
    cos_c, sin_c = _ctx_tables(lc, scale_c)

    pad = (-(b + 1)) % 8
    cstack = jnp.concatenate([c, c_ctx[None, :], jnp.zeros((pad, d), F32)], axis=0)
    mod = _modulation(cstack, w_ada, b_ada)

    tm = 512
    h_ctx = ctx
    for l in range(depth):
        need_ctx = l < depth - 1
        mx = mod[l, :b].reshape(b, 1, N_MOD, d)
        sh1, sc1, gt1, sh2, sc2, gt2 = [mx[:, :, k] for k in range(N_MOD)]
        mc = mod[l, b:b + 1].reshape(1, 1, N_MOD, d)
        csh1, csc1, cgt1, csh2, csc2, cgt2 = [mc[:, :, k] for k in range(N_MOD)]

        w_cat = _inproj_weights(w_in[l])
        w_o = _outproj_weights(w_out[l])
        lam_init = 0.8 - 0.6 * math.exp(-0.3 * l)
        lam = (jnp.exp(jnp.sum(c_lam_q1[l] * c_lam_k1[l])) - jnp.exp(jnp.sum(c_lam_q2[l] * c_lam_k2[l]))
               + lam_init).reshape(1).astype(F32)
        subln2 = jnp.tile(c_subln[l], 2).reshape(1, LANES)
        bias = _b_bias_tables(b_rpb[l])
        wg, wu, wd = w_gate[l].astype(BF16), w_up[l].astype(BF16), w_down[l].astype(BF16)

        p = _inproj(x, g_mix_pre[l], sc1, sh1, cos_x, sin_x, w_cat, tm)
        pc = _inproj(h_ctx, g_mix_pre[l], csc1, csh1, cos_c, sin_c, w_cat, lc)

        oa = _attn_a(p, pc, a_sink[l], True)
        ob = _attn_b(p, pc, bias)
        od = _attn_c(p, pc, lam, subln2, lam_init, True)
        x1, h2 = _outproj(oa, ob, od, x, w_o, g_mix_post[l], gt1, g_ffn_pre[l], sc2, sh2, tm)
        if need_ctx:
            oac = _attn_a(p, pc, a_sink[l], False)
            obc = _attn_b_ctx(pc)
            odc = _attn_c(p, pc, lam, subln2, lam_init, False)
            c1, hc2 = _outproj(oac, obc, odc, h_ctx, w_o, g_mix_post[l], cgt1, g_ffn_pre[l], csc2, csh2, lc)
            yc = _moe(hc2, w_router[l], wg, wu, wd)
            h_ctx = _resid(c1, yc, g_ffn_post[l], cgt2, lc)
        y = _moe(h2, w_router[l], wg, wu, wd)
        x = _resid(x1, y, g_ffn_post[l], gt2, tm)
    return x
```

```python
import functools
import math

import numpy as np
import jax
import jax.numpy as jnp
from jax import lax
from jax.experimental import pallas as pl
from jax.experimental.pallas import tpu as pltpu

F32 = jnp.float32
BF16 = jnp.bfloat16

GRID_W = 64
HEAD_DIM = 64
A_Q_HEADS, A_KV_HEADS = 6, 2
A_Q_PER_KV = A_Q_HEADS // A_KV_HEADS
A_WINDOW = 128
A_BLOCK = 128
B_HEADS, B_MAX_ROWS, B_COLS = 6, 8, 16
C_HEADS, C_QK_DIM, C_V_DIM = 4, 32, 64
N_EXPERTS = 16
CAPACITY_FACTOR = 2
N_MOD = 6
ROPE_THETA = 10000.0
EPS = 1e-6
NEG_INF = -1e30
LOG2E = math.log2(math.e)
QK_SCALE_64 = HEAD_DIM ** -0.5 * LOG2E
QK_SCALE_32 = C_QK_DIM ** -0.5 * LOG2E

LANES = 128
ROPE_W = 1024
OFF_AQ, OFF_AK, OFF_CQ, OFF_CK, OFF_AV, OFF_BQ, OFF_BK, OFF_BV, OFF_CV = (
    0, 384, 512, 768, 1024, 1152, 1536, 1920, 2304)
IN_W = 2560
VMEM_LIMIT = 56 * 1024 * 1024


def _cparams(sem):
    return pltpu.CompilerParams(dimension_semantics=sem, vmem_limit_bytes=VMEM_LIMIT)


def _dot(a, b):
    return jnp.dot(a, b, preferred_element_type=F32)


def _dot_nt(a, b):
    return lax.dot_general(a, b, (((1,), (1,)), ((), ())), preferred_element_type=F32)


def _rms(x, g):
    return x * lax.rsqrt(jnp.mean(x * x, axis=-1, keepdims=True) + EPS) * g


def _mod_kernel(c_ref, w_ref, b_ref, o_ref):
    c = c_ref[...]
    a = c / (1.0 + jnp.exp(-c))
    o_ref[0] = jnp.dot(a, w_ref[0], preferred_element_type=F32,
                       precision=lax.Precision.HIGHEST) + b_ref[0]


def _modulation(cstack, w_ada, b_ada):
    nl, d, n = w_ada.shape
    rows = cstack.shape[0]
    tn = 1536
    return pl.pallas_call(
        _mod_kernel,
        out_shape=jax.ShapeDtypeStruct((nl, rows, n), F32),
        grid=(nl, n // tn),
        in_specs=[
            pl.BlockSpec((rows, d), lambda l, j: (0, 0)),
            pl.BlockSpec((1, d, tn), lambda l, j: (l, 0, j)),
            pl.BlockSpec((1, 1, tn), lambda l, j: (l, 0, j)),
        ],
        out_specs=pl.BlockSpec((1, rows, tn), lambda l, j: (l, 0, j)),
        compiler_params=_cparams(("parallel", "parallel")),
        name="adaln_modulation",
    )(cstack, w_ada, b_ada.reshape(nl, 1, n))


_ROPE_TABLE_OF_UNIT = (0, 0, 0, 1, 2, 2, 3, 3)


def _inproj_kernel(x_ref, g_ref, sc_ref, sh_ref, cos_ref, sin_ref, w_ref, o_ref):
    x = x_ref[0]
    h = _rms(x, g_ref[...]) * (1.0 + sc_ref[0]) + sh_ref[0]
    hb = h.astype(BF16)
    for j in range(ROPE_W // 256):
        t0, t1 = _ROPE_TABLE_OF_UNIT[2 * j], _ROPE_TABLE_OF_UNIT[2 * j + 1]
        cos = jnp.concatenate([cos_ref[:, t0 * LANES:(t0 + 1) * LANES],
                               cos_ref[:, t1 * LANES:(t1 + 1) * LANES]], axis=1)
        sin = jnp.concatenate([sin_ref[:, t0 * LANES:(t0 + 1) * LANES],
                               sin_ref[:, t1 * LANES:(t1 + 1) * LANES]], axis=1)
        p = _dot(hb, w_ref[:, 256 * j:256 * (j + 1)])
        ps = _dot(hb, w_ref[:, IN_W + 256 * j:IN_W + 256 * (j + 1)])
        o_ref[0, :, 256 * j:256 * (j + 1)] = (p * cos + ps * sin).astype(BF16)
    for j in range(ROPE_W // 256, IN_W // 256):
        o_ref[0, :, 256 * j:256 * (j + 1)] = _dot(hb, w_ref[:, 256 * j:256 * (j + 1)]).astype(BF16)


def _inproj(x, g, sc, sh, cos, sin, w_cat, tm):
    b, s, d = x.shape
    per_sample = sc.shape[0] > 1
    mod_map = (lambda bi, i: (bi, 0, 0)) if per_sample else (lambda bi, i: (0, 0, 0))
    return pl.pallas_call(
        _inproj_kernel,
        out_shape=jax.ShapeDtypeStruct((b, s, IN_W), BF16),
        grid=(b, s // tm),
        in_specs=[
            pl.BlockSpec((1, tm, d), lambda bi, i: (bi, i, 0)),
            pl.BlockSpec((1, d), lambda bi, i: (0, 0)),
            pl.BlockSpec((1, 1, d), mod_map),
            pl.BlockSpec((1, 1, d), mod_map),
            pl.BlockSpec((tm, 4 * LANES), lambda bi, i: (i, 0)),
            pl.BlockSpec((tm, 4 * LANES), lambda bi, i: (i, 0)),
            pl.BlockSpec((d, IN_W + ROPE_W), lambda bi, i: (0, 0)),
        ],
        out_specs=pl.BlockSpec((1, tm, IN_W), lambda bi, i: (bi, i, 0)),
        compiler_params=_cparams(("parallel", "parallel")),
        name="norm_inproj_rope",
    )(x, g.reshape(1, d), sc, sh, cos, sin, w_cat)


def _lo_mask():
    return lax.broadcasted_iota(jnp.int32, (1, LANES), 1) < HEAD_DIM


def _split_halves(q):
    lo = _lo_mask()
    zero = jnp.zeros_like(q)
    return jnp.where(lo, q, zero), jnp.where(lo, zero, q)


A_STEP_BLOCKS = 4


def _attn_a_kernel(sink_ref, q_ref, *refs, seq, latent):
    if latent:
        k_ref, v_ref, kc_ref, vc_ref, o_ref = refs
    else:
        kc_ref, vc_ref, o_ref = refs
    tq = q_ref.shape[1]
    npair = A_Q_HEADS // 2
    kc = kc_ref[0]
    vc = vc_ref[0]
    lo = _lo_mask()
    if latent:
        i = pl.program_id(1)
        band = tq + 2 * A_BLOCK
        start = pl.multiple_of(jnp.clip(i * tq - A_BLOCK, 0, seq - band), A_BLOCK)
        kb = k_ref[0, pl.ds(start, band), :]
        vb = v_ref[0, pl.ds(start, band), :]
        delta = ((start - i * tq) + lax.broadcasted_iota(jnp.int32, (1, band), 1)
                 - lax.broadcasted_iota(jnp.int32, (tq, 1), 0))
        wmask = jnp.where(jnp.abs(delta) <= A_WINDOW, 0.0, NEG_INF)
        wmask = jnp.concatenate([wmask, wmask], axis=0)
    outs = []
    for t in range(npair):
        qa, qb = _split_halves(q_ref[0, :, t * LANES:(t + 1) * LANES])
        q2 = jnp.concatenate([qa, qb], axis=0)
        sink = jnp.concatenate([jnp.full((tq, 1), sink_ref[t], F32),
                                jnp.full((tq, 1), sink_ref[npair + t], F32)], axis=0)
        s_ctx = _dot_nt(q2, kc)
        m = jnp.maximum(jnp.max(s_ctx, axis=-1, keepdims=True), sink)
        if latent:
            s_lat = _dot_nt(q2, kb) + wmask
            m = jnp.maximum(m, jnp.max(s_lat, axis=-1, keepdims=True))
            e_lat = jnp.exp2(s_lat - m)
        e_ctx = jnp.exp2(s_ctx - m)
        l = jnp.sum(e_ctx, axis=-1, keepdims=True) + jnp.exp2(sink - m)
        o = _dot(e_ctx.astype(BF16), vc)
        if latent:
            l = l + jnp.sum(e_lat, axis=-1, keepdims=True)
            o = o + _dot(e_lat.astype(BF16), vb)
        o = o / l
        outs.append(jnp.where(lo, o[:tq], o[tq:]))
    o_ref[0] = jnp.concatenate(outs, axis=1).astype(BF16)


def _attn_a(p, pc, sink, latent):
    src = p if latent else pc
    b, s, _ = src.shape
    lc = pc.shape[1]
    tq = A_STEP_BLOCKS * A_BLOCK if latent else s
    wq = A_Q_HEADS * HEAD_DIM
    in_specs = [pl.BlockSpec(memory_space=pltpu.SMEM),
                pl.BlockSpec((1, tq, wq), lambda bi, i: (bi, i, OFF_AQ // wq))]
    args = [sink, src]
    if latent:
        in_specs += [pl.BlockSpec((1, s, LANES), lambda bi, i: (bi, 0, OFF_AK // LANES)),
                     pl.BlockSpec((1, s, LANES), lambda bi, i: (bi, 0, OFF_AV // LANES))]
        args += [p, p]
    in_specs += [pl.BlockSpec((1, lc, LANES), lambda bi, i: (bi, 0, OFF_AK // LANES)),
                 pl.BlockSpec((1, lc, LANES), lambda bi, i: (bi, 0, OFF_AV // LANES))]
    args += [pc, pc]
    return pl.pallas_call(
        functools.partial(_attn_a_kernel, seq=s, latent=latent),
        out_shape=jax.ShapeDtypeStruct((b, s, wq), BF16),
        grid=(b, s // tq),
        in_specs=in_specs,
        out_specs=pl.BlockSpec((1, tq, wq), lambda bi, i: (bi, i, 0)),
        compiler_params=_cparams(("parallel", "arbitrary")),
        name="attn_window_gqa" + ("" if latent else "_ctx"),
    )(*args)


B_GROUP_ROWS = 4
B_UNION_ROWS = B_GROUP_ROWS + B_MAX_ROWS


def _attn_b_softmax_out(s_parts, v_parts):
    m = functools.reduce(jnp.maximum, [jnp.max(s, axis=-1, keepdims=True) for s in s_parts])
    es = [jnp.exp2(s - m) for s in s_parts]
    l = functools.reduce(jnp.add, [jnp.sum(e, axis=-1, keepdims=True) for e in es])
    o = functools.reduce(jnp.add, [_dot(e.astype(BF16), v) for e, v in zip(es, v_parts)])
    return o / l


def _attn_b_kernel(q_ref, k_ref, v_ref, kc_ref, vc_ref, bias_ref, o_ref, *, rows_n):
    i = pl.program_id(1)
    npair = B_HEADS // 2
    lo = _lo_mask()
    tq = q_ref.shape[1]
    nk = B_UNION_ROWS * GRID_W
    us = jnp.clip(i * B_GROUP_ROWS - B_MAX_ROWS // 2, 0, rows_n - B_UNION_ROWS)
    kstart = pl.multiple_of(us * GRID_W, GRID_W)
    outs = []
    for t in range(npair):
        cols = slice(t * LANES, (t + 1) * LANES)
        qa, qb = _split_halves(q_ref[0, :, cols])
        q2 = jnp.concatenate([qa, qb], axis=0)
        s_lat = _dot_nt(q2, k_ref[0, pl.ds(kstart, nk), cols]) + bias_ref[0, t]
        s_ctx = _dot_nt(q2, kc_ref[0, :, cols])
        o = _attn_b_softmax_out([s_lat, s_ctx], [v_ref[0, pl.ds(kstart, nk), cols], vc_ref[0, :, cols]])
        outs.append(jnp.where(lo, o[:tq], o[tq:]))
    o_ref[0] = jnp.concatenate(outs, axis=1).astype(BF16)


def _attn_b_ctx_kernel(q_ref, kc_ref, vc_ref, o_ref):
    npair = B_HEADS // 2
    lo = _lo_mask()
    n = q_ref.shape[1]
    for t in range(npair):
        cols = slice(t * LANES, (t + 1) * LANES)
        qa, qb = _split_halves(q_ref[0, :, cols])
        q2 = jnp.concatenate([qa, qb], axis=0)
        s_ctx = _dot_nt(q2, kc_ref[0, :, cols])
        o = _attn_b_softmax_out([s_ctx], [vc_ref[0, :, cols]])
        o_ref[0, :, cols] = jnp.where(lo, o[:n], o[n:]).astype(BF16)


def _attn_b(p, pc, bias):
    b, s, _ = p.shape
    lc = pc.shape[1]
    w = B_HEADS * HEAD_DIM
    rows_n = s // GRID_W
    tq = B_GROUP_ROWS * GRID_W
    ng = s // tq
    variant = lambda bi, i: (jnp.minimum(i, 1) + (i == ng - 1).astype(jnp.int32), 0, 0, 0)
    return pl.pallas_call(
        functools.partial(_attn_b_kernel, rows_n=rows_n),
        out_shape=jax.ShapeDtypeStruct((b, s, w), BF16),
        grid=(b, ng),
        in_specs=[
            pl.BlockSpec((1, tq, w), lambda bi, i: (bi, i, OFF_BQ // w)),
            pl.BlockSpec((1, s, w), lambda bi, i: (bi, 0, OFF_BK // w)),
            pl.BlockSpec((1, s, w), lambda bi, i: (bi, 0, OFF_BV // w)),
            pl.BlockSpec((1, lc, w), lambda bi, i: (bi, 0, OFF_BK // w)),
            pl.BlockSpec((1, lc, w), lambda bi, i: (bi, 0, OFF_BV // w)),
            pl.BlockSpec((1,) + bias.shape[1:], variant),
        ],
        out_specs=pl.BlockSpec((1, tq, w), lambda bi, i: (bi, i, 0)),
        compiler_params=_cparams(("parallel", "arbitrary")),
        name="attn_neighbourhood",
    )(p, p, p, pc, pc, bias)


def _attn_b_ctx(pc):
    b, lc, _ = pc.shape
    w = B_HEADS * HEAD_DIM
    return pl.pallas_call(
        _attn_b_ctx_kernel,
        out_shape=jax.ShapeDtypeStruct((b, lc, w), BF16),
        grid=(b,),
        in_specs=[
            pl.BlockSpec((1, lc, w), lambda bi: (bi, 0, OFF_BQ // w)),
            pl.BlockSpec((1, lc, w), lambda bi: (bi, 0, OFF_BK // w)),
            pl.BlockSpec((1, lc, w), lambda bi: (bi, 0, OFF_BV // w)),
        ],
        out_specs=pl.BlockSpec((1, lc, w), lambda bi: (bi, 0, 0)),
        compiler_params=_cparams(("parallel",)),
        name="attn_neighbourhood_ctx",
    )(pc, pc, pc)


def _b_bias_tables(rpb, rows_n, scale):
    g_rows, u_rows = B_GROUP_ROWS, B_UNION_ROWS
    col = np.arange(GRID_W)
    cstart = np.clip(col - B_COLS // 2, 0, GRID_W - B_COLS)
    col_ok = (col[None, :] >= cstart[:, None]) & (col[None, :] < cstart[:, None] + B_COLS)
    dc_idx = np.clip(col[None, :] - col[:, None], -(B_COLS - 1), B_COLS - 1) + (B_COLS - 1)
    pick_col = (dc_idx[None] == np.arange(2 * B_COLS - 1)[:, None, None]).astype(np.float32)
    n_groups = rows_n // g_rows
    patterns, group_variant = [], []
    for g in range(n_groups):
        us = np.clip(g * g_rows - B_MAX_ROWS // 2, 0, rows_n - u_rows)
        r = g * g_rows + np.arange(g_rows)[:, None]
        rs = np.clip(r - B_MAX_ROWS // 2, 0, rows_n - B_MAX_ROWS)
        key_row = us + np.arange(u_rows)[None, :]
        dr = np.where((key_row >= rs) & (key_row < rs + B_MAX_ROWS), key_row - r + (B_MAX_ROWS - 1), -1)
        if not any(np.array_equal(dr, p_) for p_ in patterns):
            patterns.append(dr)
        group_variant.append([np.array_equal(dr, p_) for p_ in patterns].index(True))
    assert group_variant == [0] + [1] * (n_groups - 2) + [2], group_variant
    dr = np.stack(patterns)
    pick_row = (dr[..., None] == np.arange(2 * B_MAX_ROWS - 1)).astype(np.float32)
    hi = lax.Precision.HIGHEST
    rsel = jnp.einsum('vuwd,hdc->vhuwc', pick_row, rpb.astype(F32), precision=hi)
    t = jnp.einsum('vhuwc,cqk->vhuqwk', rsel, pick_col, precision=hi) * scale
    ok = (dr >= 0)[:, None, :, None, :, None] & col_ok[None, None, None, :, None, :]
    t = jnp.where(ok, t, NEG_INF)
    return t.reshape(len(patterns), B_HEADS // 2, 2 * g_rows * GRID_W, u_rows * GRID_W)


C_KEY_CHUNK = 1024
C_Q_TILE = 256


def _attn_c_kernel(lam_ref, q_ref, *refs, latent, out_scale):
    if latent:
        k_ref, v_ref, kc_ref, vc_ref, g_ref, o_ref = refs
    else:
        kc_ref, vc_ref, g_ref, o_ref = refs
    tq = q_ref.shape[1]
    lam = lam_ref[0]
    q = q_ref[0]
    quarter = lax.broadcasted_iota(jnp.int32, (1, LANES), 1) // C_QK_DIM
    zero = jnp.zeros_like(q)
    q4 = jnp.concatenate([jnp.where(quarter == j, q, zero) for j in range(4)], axis=0)
    chunks = []
    if latent:
        kc_n = min(C_KEY_CHUNK, k_ref.shape[1])
        chunks += [(k_ref, v_ref, c * kc_n, kc_n) for c in range(k_ref.shape[1] // kc_n)]
    chunks.append((kc_ref, vc_ref, 0, kc_ref.shape[1]))
    m = l = acc = None
    for kr, vr, st, n in chunks:
        s = _dot_nt(q4, kr[0, st:st + n, :])
        mc = jnp.max(s, axis=-1, keepdims=True)
        if m is None:
            m = mc
            e = jnp.exp2(s - m)
            l = jnp.sum(e, axis=-1, keepdims=True)
            acc = _dot(e.astype(BF16), vr[0, st:st + n, :])
        else:
            m_new = jnp.maximum(m, mc)
            alpha = jnp.exp2(m - m_new)
            e = jnp.exp2(s - m_new)
            l = l * alpha + jnp.sum(e, axis=-1, keepdims=True)
            acc = acc * alpha + _dot(e.astype(BF16), vr[0, st:st + n, :])
            m = m_new
    o4 = acc / l
    outs = [o4[2 * h * tq:(2 * h + 1) * tq] - lam * o4[(2 * h + 1) * tq:(2 * h + 2) * tq]
            for h in range(2)]
    lo = _lo_mask()
    o = jnp.where(lo, outs[0], outs[1])
    sq = o * o
    s_lo = jnp.sum(jnp.where(lo, sq, 0.0), axis=-1, keepdims=True)
    s_hi = jnp.sum(jnp.where(lo, 0.0, sq), axis=-1, keepdims=True)
    ms = jnp.where(lo, s_lo, s_hi) * (1.0 / C_V_DIM)
    o_ref[0] = (o * lax.rsqrt(ms + EPS) * g_ref[...] * out_scale).astype(BF16)


def _attn_c(p, pc, lam, subln2, lam_init, latent):
    src = p if latent else pc
    b, s, _ = src.shape
    lc = pc.shape[1]
    tq = C_Q_TILE if latent else s
    npair = C_HEADS // 2
    in_specs = [pl.BlockSpec(memory_space=pltpu.SMEM),
                pl.BlockSpec((1, tq, LANES), lambda bi, hp, i: (bi, i, OFF_CQ // LANES + hp))]
    args = [lam, src]
    if latent:
        in_specs += [pl.BlockSpec((1, s, LANES), lambda bi, hp, i: (bi, 0, OFF_CK // LANES + hp)),
                     pl.BlockSpec((1, s, LANES), lambda bi, hp, i: (bi, 0, OFF_CV // LANES + hp))]
        args += [p, p]
    in_specs += [pl.BlockSpec((1, lc, LANES), lambda bi, hp, i: (bi, 0, OFF_CK // LANES + hp)),
                 pl.BlockSpec((1, lc, LANES), lambda bi, hp, i: (bi, 0, OFF_CV // LANES + hp)),
                 pl.BlockSpec((1, LANES), lambda bi, hp, i: (0, 0))]
    args += [pc, pc, subln2]
    return pl.pallas_call(
        functools.partial(_attn_c_kernel, latent=latent, out_scale=1.0 - lam_init),
        out_shape=jax.ShapeDtypeStruct((b, s, C_HEADS * C_V_DIM), BF16),
        grid=(b, npair, s // tq),
        in_specs=in_specs,
        out_specs=pl.BlockSpec((1, tq, LANES), lambda bi, hp, i: (bi, i, hp)),
        compiler_params=_cparams(("parallel", "parallel", "arbitrary")),
        name="attn_differential" + ("" if latent else "_ctx"),
    )(*args)


def _outproj_kernel(oa_ref, ob_ref, oc_ref, x_ref, w_ref, gpost_ref, gt_ref, gpre_ref, sc_ref, sh_ref,
                    x1_ref, h2_ref):
    wa = oa_ref.shape[2]
    wb = ob_ref.shape[2]
    y = (_dot(oa_ref[0], w_ref[0:wa, :]) + _dot(ob_ref[0], w_ref[wa:wa + wb, :])
         + _dot(oc_ref[0], w_ref[wa + wb:, :]))
    x1 = x_ref[0] + gt_ref[0] * _rms(y, gpost_ref[...])
    x1_ref[0] = x1
    h2_ref[0] = _rms(x1, gpre_ref[...]) * (1.0 + sc_ref[0]) + sh_ref[0]


def _outproj(oa, ob, oc, x, w_out, g_post, gt, g_pre, sc, sh, tm):
    b, s, d = x.shape
    per_sample = gt.shape[0] > 1
    mod_map = (lambda bi, i: (bi, 0, 0)) if per_sample else (lambda bi, i: (0, 0, 0))
    row = lambda bi, i: (bi, i, 0)
    const2 = lambda bi, i: (0, 0)
    return pl.pallas_call(
        _outproj_kernel,
        out_shape=(jax.ShapeDtypeStruct((b, s, d), F32), jax.ShapeDtypeStruct((b, s, d), F32)),
        grid=(b, s // tm),
        in_specs=[
            pl.BlockSpec((1, tm, oa.shape[2]), row),
            pl.BlockSpec((1, tm, ob.shape[2]), row),
            pl.BlockSpec((1, tm, oc.shape[2]), row),
            pl.BlockSpec((1, tm, d), row),
            pl.BlockSpec(w_out.shape, const2),
            pl.BlockSpec((1, d), const2),
            pl.BlockSpec((1, 1, d), mod_map),
            pl.BlockSpec((1, d), const2),
            pl.BlockSpec((1, 1, d), mod_map),
            pl.BlockSpec((1, 1, d), mod_map),
        ],
        out_specs=(pl.BlockSpec((1, tm, d), row), pl.BlockSpec((1, tm, d), row)),
        compiler_params=_cparams(("parallel", "parallel")),
        name="outproj_residual_norm",
    )(oa, ob, oc, x, w_out, g_post.reshape(1, d), gt, g_pre.reshape(1, d), sc, sh)


ROW_UNROLL = 8


def _gather_kernel(idx_ref, h_ref, o_ref, buf_ref):
    cap = buf_ref.shape[0]

    def body(j, carry):
        base = pl.multiple_of(j * ROW_UNROLL, ROW_UNROLL)
        for u in range(ROW_UNROLL):
            n = idx_ref[0, 0, base + u]
            buf_ref[pl.ds(base + u, 1), :] = h_ref[0, pl.ds(n, 1), :]
        return carry

    lax.fori_loop(0, cap // ROW_UNROLL, body, 0)
    o_ref[0, 0] = buf_ref[...].astype(BF16)


def _moe_gather(h2, idx):
    b, n, d = h2.shape
    e, cap = idx.shape[1], idx.shape[2]
    return pl.pallas_call(
        _gather_kernel,
        out_shape=jax.ShapeDtypeStruct((e, b, cap, d), BF16),
        grid=(b, e),
        in_specs=[
            pl.BlockSpec((1, 1, cap), lambda bi, ei: (bi * e + ei, 0, 0), memory_space=pltpu.SMEM),
            pl.BlockSpec((1, n, d), lambda bi, ei: (bi, 0, 0)),
        ],
        out_specs=pl.BlockSpec((1, 1, cap, d), lambda bi, ei: (ei, bi, 0, 0)),
        scratch_shapes=[pltpu.VMEM((cap, d), F32)],
        compiler_params=_cparams(("parallel", "arbitrary")),
        name="moe_gather",
    )(idx.reshape(b * e, 1, cap), h2)


FF_CHUNK = 512


def _ffn_kernel(xs_ref, gate_ref, wg_ref, wu_ref, wd_ref, o_ref):
    xs = xs_ref[0]
    ff = wg_ref.shape[2]
    acc = None
    for c in range(ff // FF_CHUNK):
        cs = slice(c * FF_CHUNK, (c + 1) * FF_CHUNK)
        g = _dot(xs, wg_ref[0, :, cs])
        u = _dot(xs, wu_ref[0, :, cs])
        hid = ((g / (1.0 + jnp.exp(-g))) * u).astype(BF16)
        part = _dot(hid, wd_ref[0, cs, :])
        acc = part if acc is None else acc + part
    o_ref[0] = acc * gate_ref[0]


FFN_ROWS = 512


def _moe_ffn(xs, gates, wg, wu, wd):
    e, r, d = xs.shape
    ff = wg.shape[2]
    tr = min(r, FFN_ROWS)
    return pl.pallas_call(
        _ffn_kernel,
        out_shape=jax.ShapeDtypeStruct((e, r, d), F32),
        grid=(e, r // tr),
        in_specs=[
            pl.BlockSpec((1, tr, d), lambda ei, ri: (ei, ri, 0)),
            pl.BlockSpec((1, tr, 1), lambda ei, ri: (ei, ri, 0)),
            pl.BlockSpec((1, d, ff), lambda ei, ri: (ei, 0, 0)),
            pl.BlockSpec((1, d, ff), lambda ei, ri: (ei, 0, 0)),
            pl.BlockSpec((1, ff, d), lambda ei, ri: (ei, 0, 0)),
        ],
        out_specs=pl.BlockSpec((1, tr, d), lambda ei, ri: (ei, ri, 0)),
        compiler_params=_cparams(("parallel", "arbitrary")),
        name="moe_expert_ffn",
    )(xs, gates, wg, wu, wd)


def _combine_kernel(idx_ref, y_ref, o_ref):
    cap = y_ref.shape[2]

    @pl.when(pl.program_id(1) == 0)
    def _():
        o_ref[...] = jnp.zeros_like(o_ref)

    def body(j, carry):
        base = pl.multiple_of(j * ROW_UNROLL, ROW_UNROLL)
        toks = [idx_ref[0, 0, base + u] for u in range(ROW_UNROLL)]
        ys = y_ref[0, 0, pl.ds(base, ROW_UNROLL), :]
        rows = [o_ref[0, pl.ds(n, 1), :] for n in toks]
        for u, n in enumerate(toks):
            o_ref[0, pl.ds(n, 1), :] = rows[u] + ys[u:u + 1]
        return carry

    lax.fori_loop(0, cap // ROW_UNROLL, body, 0)


def _moe_combine(y, idx, n):
    e, b, cap, d = y.shape
    return pl.pallas_call(
        _combine_kernel,
        out_shape=jax.ShapeDtypeStruct((b, n, d), F32),
        grid=(b, e),
        in_specs=[
            pl.BlockSpec((1, 1, cap), lambda bi, ei: (bi * e + ei, 0, 0), memory_space=pltpu.SMEM),
            pl.BlockSpec((1, 1, cap, d), lambda bi, ei: (ei, bi, 0, 0)),
        ],
        out_specs=pl.BlockSpec((1, n, d), lambda bi, ei: (bi, 0, 0)),
        compiler_params=_cparams(("parallel", "arbitrary")),
        name="moe_combine",
    )(idx.reshape(b * e, 1, cap), y)


def _resid_kernel(x_ref, y_ref, g_ref, gt_ref, o_ref):
    o_ref[0] = x_ref[0] + gt_ref[0] * _rms(y_ref[0], g_ref[...])


def _resid(x, y, g, gt, tm):
    b, s, d = x.shape
    per_sample = gt.shape[0] > 1
    mod_map = (lambda bi, i: (bi, 0, 0)) if per_sample else (lambda bi, i: (0, 0, 0))
    row = lambda bi, i: (bi, i, 0)
    return pl.pallas_call(
        _resid_kernel,
        out_shape=jax.ShapeDtypeStruct((b, s, d), F32),
        grid=(b, s // tm),
        in_specs=[pl.BlockSpec((1, tm, d), row), pl.BlockSpec((1, tm, d), row),
                  pl.BlockSpec((1, d), lambda bi, i: (0, 0)), pl.BlockSpec((1, 1, d), mod_map)],
        out_specs=pl.BlockSpec((1, tm, d), row),
        compiler_params=_cparams(("parallel", "parallel")),
        name="ffn_residual",
    )(x, y, g.reshape(1, d), gt)


ROUTE_VALS = 16
ROUTE_TOK_SPLIT = 64


def _count(mask):
    return jnp.sum(jnp.where(mask, 1.0, 0.0), axis=1, keepdims=True)


def _route_kernel(h_ref, wr_ref, idx_ref, gate_ref, logit_ref, aff_ref, posm_ref, *, cap, tn):
    i = pl.program_id(1)
    n_exp, n_tok = logit_ref.shape
    h = h_ref[0]
    h_hi = h.astype(BF16)
    h_lo = (h - h_hi.astype(F32)).astype(BF16)
    w = wr_ref[...]
    w_hi = w.astype(BF16)
    w_lo = (w - w_hi.astype(F32)).astype(BF16)
    logit_ref[:, pl.ds(pl.multiple_of(i * tn, tn), tn)] = (
        _dot_nt(w_hi, h_hi) + (_dot_nt(w_hi, h_lo) + _dot_nt(w_lo, h_hi)))

    @pl.when(i == pl.num_programs(1) - 1)
    def _():
        lg = logit_ref[...]
        ex = jnp.exp(lg - jnp.max(lg, axis=0, keepdims=True))
        aff = ex / jnp.sum(ex, axis=0, keepdims=True)
        aff_ref[...] = aff
        capf = float(cap)

        def tbody(it, t):
            cand = t | jnp.left_shift(jnp.int32(1), 30 - it)
            cnt = _count(aff >= lax.bitcast_convert_type(cand, F32))
            return jnp.where(cnt >= capf, cand, t)

        t = lax.fori_loop(0, 31, tbody, jnp.zeros((n_exp, 1), jnp.int32))
        above = aff >= lax.bitcast_convert_type(t + 1, F32)
        tied = jnp.logical_and(aff >= lax.bitcast_convert_type(t, F32), jnp.logical_not(above))
        need = capf - _count(above)
        tok = lax.broadcasted_iota(jnp.int32, (1, n_tok), 1)
        nbits = n_tok.bit_length()

        def mbody(it, bound):
            cand = bound | jnp.left_shift(jnp.int32(1), nbits - 1 - it)
            f = _count(jnp.logical_and(tied, tok < cand))
            return jnp.where(f <= need, cand, bound)

        bound = lax.fori_loop(0, nbits, mbody, jnp.zeros((n_exp, 1), jnp.int32))
        sel = jnp.logical_or(above, jnp.logical_and(tied, tok < bound))
        self = jnp.where(sel, 1.0, 0.0)
        csum = self
        shift = 1
        while shift < n_tok:
            csum = csum + jnp.where(tok >= shift, pltpu.roll(csum, shift, 1), 0.0)
            shift *= 2
        posm_ref[...] = jnp.where(sel, csum - self, -1.0)

        tok_hi = (tok // ROUTE_TOK_SPLIT).astype(F32)
        tok_lo = (tok % ROUTE_TOK_SPLIT).astype(F32)
        zeros = jnp.zeros((ROUTE_VALS - 5, n_tok), F32)
        jc = min(cap, 128)

        def ebody(e, carry):
            prow = posm_ref[pl.ds(e, 1), :]
            a = aff_ref[pl.ds(e, 1), :]
            a_hi = a.astype(BF16).astype(F32)
            a_mid = (a - a_hi).astype(BF16).astype(F32)
            a_lo = (a - a_hi) - a_mid
            vals = jnp.concatenate([tok_hi, tok_lo, a_hi, a_mid, a_lo, zeros], axis=0).astype(BF16)
            for c in range(cap // jc):
                slot = (lax.broadcasted_iota(jnp.int32, (jc, 1), 0) + c * jc).astype(F32)
                onehot = jnp.where(prow == slot, 1.0, 0.0).astype(BF16)
                r = _dot_nt(onehot, vals)
                idx_ref[0, e, c * jc:(c + 1) * jc, :] = (
                    r[:, 0:1] * float(ROUTE_TOK_SPLIT) + r[:, 1:2]).astype(jnp.int32)
                gate_ref[0, e, c * jc:(c + 1) * jc, :] = r[:, 2:3] + (r[:, 3:4] + r[:, 4:5])
            return carry

        lax.fori_loop(0, n_exp, ebody, 0)


def _route(h2, w_router, cap):
    b, n, d = h2.shape
    e = w_router.shape[1]
    tn = min(n, 1024)
    return pl.pallas_call(
        functools.partial(_route_kernel, cap=cap, tn=tn),
        out_shape=(jax.ShapeDtypeStruct((b, e, cap, 1), jnp.int32),
                   jax.ShapeDtypeStruct((b, e, cap, 1), F32)),
        grid=(b, n // tn),
        in_specs=[pl.BlockSpec((1, tn, d), lambda bi, i: (bi, i, 0)),
                  pl.BlockSpec((e, d), lambda bi, i: (0, 0))],
        out_specs=(pl.BlockSpec((1, e, cap, 1), lambda bi, i: (bi, 0, 0, 0)),
                   pl.BlockSpec((1, e, cap, 1), lambda bi, i: (bi, 0, 0, 0))),
        scratch_shapes=[pltpu.VMEM((e, n), F32), pltpu.VMEM((e, n), F32), pltpu.VMEM((e, n), F32)],
        compiler_params=_cparams(("parallel", "arbitrary")),
        name="moe_route",
    )(h2, w_router.T)


def _moe(h2, w_router, wg, wu, wd):
    b, n, d = h2.shape
    e = w_router.shape[1]
    cap = n * CAPACITY_FACTOR // e
    idx, gates = _route(h2, w_router, cap)
    idx = idx.reshape(b, e, cap)
    xs = _moe_gather(h2, idx)
    y = _moe_ffn(xs.reshape(e, b * cap, d), gates.transpose(1, 0, 2, 3).reshape(e, b * cap, 1),
                 wg, wu, wd)
    return _moe_combine(y.reshape(e, b, cap, d), idx, n)


def _swap_rope_halves(w, half):
    d, n = w.shape
    return w.reshape(d, n // (2 * half), 2, half)[:, :, ::-1, :].reshape(d, n)


def _inproj_weights(w):
    d = w.shape[0]
    sec = np.cumsum([0, 384, 128, 128, 384, 384, 384, 256, 256, 256])
    aq, ak, av, bq, bk, bv, cq, ck, cv = [w[:, sec[i]:sec[i + 1]] for i in range(9)]
    aq = aq.reshape(d, A_KV_HEADS, A_Q_PER_KV, HEAD_DIM).transpose(0, 2, 1, 3).reshape(d, -1)
    rope = [aq, ak, cq, ck]
    halves = [HEAD_DIM // 4, HEAD_DIM // 4, C_QK_DIM // 4, C_QK_DIM // 4]
    partner = [_swap_rope_halves(m, h) for m, h in zip(rope, halves)]
    return jnp.concatenate(rope + [av, bq * QK_SCALE_64, bk, bv, cv] + partner, axis=1).astype(BF16)


def _outproj_weights(w):
    d = w.shape[1]
    wa = A_Q_HEADS * HEAD_DIM
    a = w[:wa].reshape(A_KV_HEADS, A_Q_PER_KV, HEAD_DIM, d).transpose(1, 0, 2, 3).reshape(wa, d)
    return jnp.concatenate([a, w[wa:]], axis=0).astype(BF16)


def _rope_tables(s):
    t = jnp.arange(s)
    rows, cols = t // GRID_W, t % GRID_W

    def head_tables(d):
        q = d // 4
        inv = ROPE_THETA ** (-jnp.arange(q, dtype=F32) / q)
        cs, sn = [], []
        for pos in (rows, cols):
            ang = pos.astype(F32)[:, None] * inv[None, :]
            c, s_ = jnp.cos(ang), jnp.sin(ang)
            cs += [c, c]
            sn += [-s_, s_]
        c = jnp.concatenate(cs, axis=1)
        s_ = jnp.concatenate(sn, axis=1)
        reps = LANES // d
        return jnp.tile(c, (1, reps)), jnp.tile(s_, (1, reps))

    ca, sa = head_tables(HEAD_DIM)
    cc, sc = head_tables(C_QK_DIM)
    cos = jnp.concatenate([ca * QK_SCALE_64, ca, cc * QK_SCALE_32, cc], axis=1)
    sin = jnp.concatenate([sa * QK_SCALE_64, sa, sc * QK_SCALE_32, sc], axis=1)
    return cos, sin


def _ctx_tables(lc):
    ones = jnp.ones((lc, LANES), F32)
    cos = jnp.concatenate([ones * QK_SCALE_64, ones, ones * QK_SCALE_32, ones], axis=1)
    return cos, jnp.zeros_like(cos)


def kernel(x, c, ctx, c_ctx, w_ada, b_ada, g_mix_pre, g_mix_post, g_ffn_pre, g_ffn_post, w_in, w_out,
           a_sink, b_rpb, c_lam_q1, c_lam_k1, c_lam_q2, c_lam_k2, c_subln, w_router, w_gate, w_up, w_down):
    b, s, d = x.shape
    lc = ctx.shape[1]
    depth = w_in.shape[0]

    cos_x, sin_x = _rope_tables(s)
    cos_c, sin_c = _ctx_tables(lc)

    pad = (-(b + 1)) % 8
    cstack = jnp.concatenate([c, c_ctx[None, :], jnp.zeros((pad, d), F32)], axis=0)
    mod = _modulation(cstack, w_ada, b_ada)

    tm = 512
    h_ctx = ctx
    for l in range(depth):
        need_ctx = l < depth - 1
        mx = mod[l, :b].reshape(b, 1, N_MOD, d)
        sh1, sc1, gt1, sh2, sc2, gt2 = [mx[:, :, k] for k in range(N_MOD)]
        mc = mod[l, b:b + 1].reshape(1, 1, N_MOD, d)
        csh1, csc1, cgt1, csh2, csc2, cgt2 = [mc[:, :, k] for k in range(N_MOD)]

        w_cat = _inproj_weights(w_in[l])
        w_o = _outproj_weights(w_out[l])
        lam_init = 0.8 - 0.6 * math.exp(-0.3 * l)
        lam = (jnp.exp(jnp.sum(c_lam_q1[l] * c_lam_k1[l])) - jnp.exp(jnp.sum(c_lam_q2[l] * c_lam_k2[l]))
               + lam_init).reshape(1).astype(F32)
        subln2 = jnp.tile(c_subln[l], 2).reshape(1, LANES)
        bias = _b_bias_tables(b_rpb[l], s // GRID_W, LOG2E)
        sink = a_sink[l] * LOG2E
        wg, wu, wd = w_gate[l].astype(BF16), w_up[l].astype(BF16), w_down[l].astype(BF16)

        p = _inproj(x, g_mix_pre[l], sc1, sh1, cos_x, sin_x, w_cat, tm)
        pc = _inproj(h_ctx, g_mix_pre[l], csc1, csh1, cos_c, sin_c, w_cat, lc)

        oa = _attn_a(p, pc, sink, True)
        ob = _attn_b(p, pc, bias)
        od = _attn_c(p, pc, lam, subln2, lam_init, True)
        x1, h2 = _outproj(oa, ob, od, x, w_o, g_mix_post[l], gt1, g_ffn_pre[l], sc2, sh2, tm)
        if need_ctx:
            oac = _attn_a(p, pc, sink, False)
            obc = _attn_b_ctx(pc)
            odc = _attn_c(p, pc, lam, subln2, lam_init, False)
            c1, hc2 = _outproj(oac, obc, odc, h_ctx, w_o, g_mix_post[l], cgt1, g_ffn_pre[l], csc2, csh2, lc)
            yc = _moe(hc2, w_router[l], wg, wu, wd)
            h_ctx = _resid(c1, yc, g_ffn_post[l], cgt2, lc)
        y = _moe(h2, w_router[l], wg, wu, wd)
        x = _resid(x1, y, g_ffn_post[l], gt2, tm)
    return x
```

```python
import functools
import math

import numpy as np
import jax
import jax.numpy as jnp
from jax import lax
from jax.experimental import pallas as pl
from jax.experimental.pallas import tpu as pltpu

F32 = jnp.float32
BF16 = jnp.bfloat16

GRID_W = 64
HEAD_DIM = 64
A_Q_HEADS, A_KV_HEADS = 6, 2
A_Q_PER_KV = A_Q_HEADS // A_KV_HEADS
A_WINDOW = 128
A_BLOCK = 128
B_HEADS, B_MAX_ROWS, B_COLS = 6, 8, 16
C_HEADS, C_QK_DIM, C_V_DIM = 4, 32, 64
N_EXPERTS = 16
CAPACITY_FACTOR = 2
N_MOD = 6
ROPE_THETA = 10000.0
EPS = 1e-6
NEG_INF = -1e30
LOG2E = math.log2(math.e)
QK_SCALE_64 = HEAD_DIM ** -0.5 * LOG2E
QK_SCALE_32 = C_QK_DIM ** -0.5 * LOG2E

LANES = 128
ROPE_W = 1024
OFF_AQ, OFF_AK, OFF_CQ, OFF_CK, OFF_AV, OFF_BQ, OFF_BK, OFF_BV, OFF_CV = (
    0, 384, 512, 768, 1024, 1152, 1536, 1920, 2304)
IN_W = 2560
VMEM_LIMIT = 56 * 1024 * 1024


def _cparams(sem):
    return pltpu.CompilerParams(dimension_semantics=sem, vmem_limit_bytes=VMEM_LIMIT)


def _dot(a, b):
    return jnp.dot(a, b, preferred_element_type=F32)


def _dot_nt(a, b):
    return lax.dot_general(a, b, (((1,), (1,)), ((), ())), preferred_element_type=F32)


def _rms(x, g):
    return x * lax.rsqrt(jnp.mean(x * x, axis=-1, keepdims=True) + EPS) * g


def _mod_kernel(c_ref, w_ref, b_ref, o_ref):
    c = c_ref[...]
    a = c / (1.0 + jnp.exp(-c))
    o_ref[0] = jnp.dot(a, w_ref[0], preferred_element_type=F32,
                       precision=lax.Precision.HIGHEST) + b_ref[0]


def _modulation(cstack, w_ada, b_ada):
    nl, d, n = w_ada.shape
    rows = cstack.shape[0]
    tn = 1536
    return pl.pallas_call(
        _mod_kernel,
        out_shape=jax.ShapeDtypeStruct((nl, rows, n), F32),
        grid=(nl, n // tn),
        in_specs=[
            pl.BlockSpec((rows, d), lambda l, j: (0, 0)),
            pl.BlockSpec((1, d, tn), lambda l, j: (l, 0, j)),
            pl.BlockSpec((1, 1, tn), lambda l, j: (l, 0, j)),
        ],
        out_specs=pl.BlockSpec((1, rows, tn), lambda l, j: (l, 0, j)),
        compiler_params=_cparams(("parallel", "parallel")),
        name="adaln_modulation",
    )(cstack, w_ada, b_ada.reshape(nl, 1, n))


_ROPE_TABLE_OF_UNIT = (0, 0, 0, 1, 2, 2, 3, 3)


def _inproj_kernel(x_ref, g_ref, sc_ref, sh_ref, cos_ref, sin_ref, w_ref, o_ref):
    x = x_ref[0]
    h = _rms(x, g_ref[...]) * (1.0 + sc_ref[0]) + sh_ref[0]
    hb = h.astype(BF16)
    for j in range(ROPE_W // 256):
        t0, t1 = _ROPE_TABLE_OF_UNIT[2 * j], _ROPE_TABLE_OF_UNIT[2 * j + 1]
        cos = jnp.concatenate([cos_ref[:, t0 * LANES:(t0 + 1) * LANES],
                               cos_ref[:, t1 * LANES:(t1 + 1) * LANES]], axis=1)
        sin = jnp.concatenate([sin_ref[:, t0 * LANES:(t0 + 1) * LANES],
                               sin_ref[:, t1 * LANES:(t1 + 1) * LANES]], axis=1)
        p = _dot(hb, w_ref[:, 256 * j:256 * (j + 1)])
        ps = _dot(hb, w_ref[:, IN_W + 256 * j:IN_W + 256 * (j + 1)])
        o_ref[0, :, 256 * j:256 * (j + 1)] = (p * cos + ps * sin).astype(BF16)
    for j in range(ROPE_W // 256, IN_W // 256):
        o_ref[0, :, 256 * j:256 * (j + 1)] = _dot(hb, w_ref[:, 256 * j:256 * (j + 1)]).astype(BF16)


def _inproj(x, g, sc, sh, cos, sin, w_cat, tm):
    b, s, d = x.shape
    per_sample = sc.shape[0] > 1
    mod_map = (lambda bi, i: (bi, 0, 0)) if per_sample else (lambda bi, i: (0, 0, 0))
    return pl.pallas_call(
        _inproj_kernel,
        out_shape=jax.ShapeDtypeStruct((b, s, IN_W), BF16),
        grid=(b, s // tm),
        in_specs=[
            pl.BlockSpec((1, tm, d), lambda bi, i: (bi, i, 0)),
            pl.BlockSpec((1, d), lambda bi, i: (0, 0)),
            pl.BlockSpec((1, 1, d), mod_map),
            pl.BlockSpec((1, 1, d), mod_map),
            pl.BlockSpec((tm, 4 * LANES), lambda bi, i: (i, 0)),
            pl.BlockSpec((tm, 4 * LANES), lambda bi, i: (i, 0)),
            pl.BlockSpec((d, IN_W + ROPE_W), lambda bi, i: (0, 0)),
        ],
        out_specs=pl.BlockSpec((1, tm, IN_W), lambda bi, i: (bi, i, 0)),
        compiler_params=_cparams(("parallel", "parallel")),
        name="norm_inproj_rope",
    )(x, g.reshape(1, d), sc, sh, cos, sin, w_cat)


def _lo_mask():
    return lax.broadcasted_iota(jnp.int32, (1, LANES), 1) < HEAD_DIM


def _split_halves(q):
    lo = _lo_mask()
    zero = jnp.zeros_like(q)
    return jnp.where(lo, q, zero), jnp.where(lo, zero, q)


A_STEP_BLOCKS = 4


def _attn_a_kernel(sink_ref, q_ref, *refs, seq, latent):
    if latent:
        k_ref, v_ref, kc_ref, vc_ref, o_ref = refs
    else:
        kc_ref, vc_ref, o_ref = refs
    tq = q_ref.shape[1]
    npair = A_Q_HEADS // 2
    kc = kc_ref[0]
    vc = vc_ref[0]
    lo = _lo_mask()
    if latent:
        i = pl.program_id(1)
        band = tq + 2 * A_BLOCK
        start = pl.multiple_of(jnp.clip(i * tq - A_BLOCK, 0, seq - band), A_BLOCK)
        kb = k_ref[0, pl.ds(start, band), :]
        vb = v_ref[0, pl.ds(start, band), :]
        delta = ((start - i * tq) + lax.broadcasted_iota(jnp.int32, (1, band), 1)
                 - lax.broadcasted_iota(jnp.int32, (tq, 1), 0))
        wmask = jnp.where(jnp.abs(delta) <= A_WINDOW, 0.0, NEG_INF)
        wmask = jnp.concatenate([wmask, wmask], axis=0)
    outs = []
    for t in range(npair):
        qa, qb = _split_halves(q_ref[0, :, t * LANES:(t + 1) * LANES])
        q2 = jnp.concatenate([qa, qb], axis=0)
        sink = jnp.concatenate([jnp.full((tq, 1), sink_ref[t], F32),
                                jnp.full((tq, 1), sink_ref[npair + t], F32)], axis=0)
        s_ctx = _dot_nt(q2, kc)
        m = jnp.maximum(jnp.max(s_ctx, axis=-1, keepdims=True), sink)
        if latent:
            s_lat = _dot_nt(q2, kb) + wmask
            m = jnp.maximum(m, jnp.max(s_lat, axis=-1, keepdims=True))
            e_lat = jnp.exp2(s_lat - m)
        e_ctx = jnp.exp2(s_ctx - m)
        l = jnp.sum(e_ctx, axis=-1, keepdims=True) + jnp.exp2(sink - m)
        o = _dot(e_ctx.astype(BF16), vc)
        if latent:
            l = l + jnp.sum(e_lat, axis=-1, keepdims=True)
            o = o + _dot(e_lat.astype(BF16), vb)
        o = o / l
        outs.append(jnp.where(lo, o[:tq], o[tq:]))
    o_ref[0] = jnp.concatenate(outs, axis=1).astype(BF16)


def _attn_a(p, pc, sink, latent):
    src = p if latent else pc
    b, s, _ = src.shape
    lc = pc.shape[1]
    tq = A_STEP_BLOCKS * A_BLOCK if latent else s
    wq = A_Q_HEADS * HEAD_DIM
    in_specs = [pl.BlockSpec(memory_space=pltpu.SMEM),
                pl.BlockSpec((1, tq, wq), lambda bi, i: (bi, i, OFF_AQ // wq))]
    args = [sink, src]
    if latent:
        in_specs += [pl.BlockSpec((1, s, LANES), lambda bi, i: (bi, 0, OFF_AK // LANES)),
                     pl.BlockSpec((1, s, LANES), lambda bi, i: (bi, 0, OFF_AV // LANES))]
        args += [p, p]
    in_specs += [pl.BlockSpec((1, lc, LANES), lambda bi, i: (bi, 0, OFF_AK // LANES)),
                 pl.BlockSpec((1, lc, LANES), lambda bi, i: (bi, 0, OFF_AV // LANES))]
    args += [pc, pc]
    return pl.pallas_call(
        functools.partial(_attn_a_kernel, seq=s, latent=latent),
        out_shape=jax.ShapeDtypeStruct((b, s, wq), BF16),
        grid=(b, s // tq),
        in_specs=in_specs,
        out_specs=pl.BlockSpec((1, tq, wq), lambda bi, i: (bi, i, 0)),
        compiler_params=_cparams(("parallel", "arbitrary")),
        name="attn_window_gqa" + ("" if latent else "_ctx"),
    )(*args)


B_GROUP_ROWS = 4
B_UNION_ROWS = B_GROUP_ROWS + B_MAX_ROWS


def _attn_b_softmax_out(s_parts, v_parts):
    m = functools.reduce(jnp.maximum, [jnp.max(s, axis=-1, keepdims=True) for s in s_parts])
    es = [jnp.exp2(s - m) for s in s_parts]
    l = functools.reduce(jnp.add, [jnp.sum(e, axis=-1, keepdims=True) for e in es])
    o = functools.reduce(jnp.add, [_dot(e.astype(BF16), v) for e, v in zip(es, v_parts)])
    return o / l


def _attn_b_kernel(q_ref, k_ref, v_ref, kc_ref, vc_ref, bias_ref, o_ref, *, rows_n):
    i = pl.program_id(1)
    npair = B_HEADS // 2
    lo = _lo_mask()
    tq = q_ref.shape[1]
    nk = B_UNION_ROWS * GRID_W
    us = jnp.clip(i * B_GROUP_ROWS - B_MAX_ROWS // 2, 0, rows_n - B_UNION_ROWS)
    kstart = pl.multiple_of(us * GRID_W, GRID_W)
    outs = []
    for t in range(npair):
        cols = slice(t * LANES, (t + 1) * LANES)
        qa, qb = _split_halves(q_ref[0, :, cols])
        q2 = jnp.concatenate([qa, qb], axis=0)
        s_lat = _dot_nt(q2, k_ref[0, pl.ds(kstart, nk), cols]) + bias_ref[0, t]
        s_ctx = _dot_nt(q2, kc_ref[0, :, cols])
        o = _attn_b_softmax_out([s_lat, s_ctx], [v_ref[0, pl.ds(kstart, nk), cols], vc_ref[0, :, cols]])
        outs.append(jnp.where(lo, o[:tq], o[tq:]))
    o_ref[0] = jnp.concatenate(outs, axis=1).astype(BF16)


def _attn_b_ctx_kernel(q_ref, kc_ref, vc_ref, o_ref):
    npair = B_HEADS // 2
    lo = _lo_mask()
    n = q_ref.shape[1]
    for t in range(npair):
        cols = slice(t * LANES, (t + 1) * LANES)
        qa, qb = _split_halves(q_ref[0, :, cols])
        q2 = jnp.concatenate([qa, qb], axis=0)
        s_ctx = _dot_nt(q2, kc_ref[0, :, cols])
        o = _attn_b_softmax_out([s_ctx], [vc_ref[0, :, cols]])
        o_ref[0, :, cols] = jnp.where(lo, o[:n], o[n:]).astype(BF16)


def _attn_b(p, pc, bias):
    b, s, _ = p.shape
    lc = pc.shape[1]
    w = B_HEADS * HEAD_DIM
    rows_n = s // GRID_W
    tq = B_GROUP_ROWS * GRID_W
    ng = s // tq
    variant = lambda bi, i: (jnp.minimum(i, 1) + (i == ng - 1).astype(jnp.int32), 0, 0, 0)
    return pl.pallas_call(
        functools.partial(_attn_b_kernel, rows_n=rows_n),
        out_shape=jax.ShapeDtypeStruct((b, s, w), BF16),
        grid=(b, ng),
        in_specs=[
            pl.BlockSpec((1, tq, w), lambda bi, i: (bi, i, OFF_BQ // w)),
            pl.BlockSpec((1, s, w), lambda bi, i: (bi, 0, OFF_BK // w)),
            pl.BlockSpec((1, s, w), lambda bi, i: (bi, 0, OFF_BV // w)),
            pl.BlockSpec((1, lc, w), lambda bi, i: (bi, 0, OFF_BK // w)),
            pl.BlockSpec((1, lc, w), lambda bi, i: (bi, 0, OFF_BV // w)),
            pl.BlockSpec((1,) + bias.shape[1:], variant),
        ],
        out_specs=pl.BlockSpec((1, tq, w), lambda bi, i: (bi, i, 0)),
        compiler_params=_cparams(("parallel", "arbitrary")),
        name="attn_neighbourhood",
    )(p, p, p, pc, pc, bias)


def _attn_b_ctx(pc):
    b, lc, _ = pc.shape
    w = B_HEADS * HEAD_DIM
    return pl.pallas_call(
        _attn_b_ctx_kernel,
        out_shape=jax.ShapeDtypeStruct((b, lc, w), BF16),
        grid=(b,),
        in_specs=[
            pl.BlockSpec((1, lc, w), lambda bi: (bi, 0, OFF_BQ // w)),
            pl.BlockSpec((1, lc, w), lambda bi: (bi, 0, OFF_BK // w)),
            pl.BlockSpec((1, lc, w), lambda bi: (bi, 0, OFF_BV // w)),
        ],
        out_specs=pl.BlockSpec((1, lc, w), lambda bi: (bi, 0, 0)),
        compiler_params=_cparams(("parallel",)),
        name="attn_neighbourhood_ctx",
    )(pc, pc, pc)


def _b_bias_tables(rpb, rows_n, scale):
    g_rows, u_rows = B_GROUP_ROWS, B_UNION_ROWS
    col = np.arange(GRID_W)
    cstart = np.clip(col - B_COLS // 2, 0, GRID_W - B_COLS)
    col_ok = (col[None, :] >= cstart[:, None]) & (col[None, :] < cstart[:, None] + B_COLS)
    dc_idx = np.clip(col[None, :] - col[:, None], -(B_COLS - 1), B_COLS - 1) + (B_COLS - 1)
    pick_col = (dc_idx[None] == np.arange(2 * B_COLS - 1)[:, None, None]).astype(np.float32)
    n_groups = rows_n // g_rows
    patterns, group_variant = [], []
    for g in range(n_groups):
        us = np.clip(g * g_rows - B_MAX_ROWS // 2, 0, rows_n - u_rows)
        r = g * g_rows + np.arange(g_rows)[:, None]
        rs = np.clip(r - B_MAX_ROWS // 2, 0, rows_n - B_MAX_ROWS)
        key_row = us + np.arange(u_rows)[None, :]
        dr = np.where((key_row >= rs) & (key_row < rs + B_MAX_ROWS), key_row - r + (B_MAX_ROWS - 1), -1)
        if not any(np.array_equal(dr, p_) for p_ in patterns):
            patterns.append(dr)
        group_variant.append([np.array_equal(dr, p_) for p_ in patterns].index(True))
    assert group_variant == [0] + [1] * (n_groups - 2) + [2], group_variant
    dr = np.stack(patterns)
    pick_row = (dr[..., None] == np.arange(2 * B_MAX_ROWS - 1)).astype(np.float32)
    hi = lax.Precision.HIGHEST
    rsel = jnp.einsum('vuwd,hdc->vhuwc', pick_row, rpb.astype(F32), precision=hi)
    t = jnp.einsum('vhuwc,cqk->vhuqwk', rsel, pick_col, precision=hi) * scale
    ok = (dr >= 0)[:, None, :, None, :, None] & col_ok[None, None, None, :, None, :]
    t = jnp.where(ok, t, NEG_INF)
    return t.reshape(len(patterns), B_HEADS // 2, 2 * g_rows * GRID_W, u_rows * GRID_W)


C_KEY_CHUNK = 2048
C_Q_TILE = 256


def _attn_c_kernel(lam_ref, q_ref, *refs, latent, out_scale):
    if latent:
        k_ref, v_ref, kc_ref, vc_ref, g_ref, o_ref = refs
    else:
        kc_ref, vc_ref, g_ref, o_ref = refs
    tq = q_ref.shape[1]
    lam = lam_ref[0]
    q = q_ref[0]
    quarter = lax.broadcasted_iota(jnp.int32, (1, LANES), 1) // C_QK_DIM
    zero = jnp.zeros_like(q)
    q4 = jnp.concatenate([jnp.where(quarter == j, q, zero) for j in range(4)], axis=0)
    chunks = []
    if latent:
        kc_n = min(C_KEY_CHUNK, k_ref.shape[1])
        chunks += [(k_ref, v_ref, c * kc_n, kc_n) for c in range(k_ref.shape[1] // kc_n)]
    chunks.append((kc_ref, vc_ref, 0, kc_ref.shape[1]))
    m = l = acc = None
    for kr, vr, st, n in chunks:
        s = _dot_nt(q4, kr[0, st:st + n, :])
        mc = jnp.max(s, axis=-1, keepdims=True)
        if m is None:
            m = mc
            e = jnp.exp2(s - m)
            l = jnp.sum(e, axis=-1, keepdims=True)
            acc = _dot(e.astype(BF16), vr[0, st:st + n, :])
        else:
            m_new = jnp.maximum(m, mc)
            alpha = jnp.exp2(m - m_new)
            e = jnp.exp2(s - m_new)
            l = l * alpha + jnp.sum(e, axis=-1, keepdims=True)
            acc = acc * alpha + _dot(e.astype(BF16), vr[0, st:st + n, :])
            m = m_new
    o4 = acc / l
    outs = [o4[2 * h * tq:(2 * h + 1) * tq] - lam * o4[(2 * h + 1) * tq:(2 * h + 2) * tq]
            for h in range(2)]
    lo = _lo_mask()
    o = jnp.where(lo, outs[0], outs[1])
    sq = o * o
    s_lo = jnp.sum(jnp.where(lo, sq, 0.0), axis=-1, keepdims=True)
    s_hi = jnp.sum(jnp.where(lo, 0.0, sq), axis=-1, keepdims=True)
    ms = jnp.where(lo, s_lo, s_hi) * (1.0 / C_V_DIM)
    o_ref[0] = (o * lax.rsqrt(ms + EPS) * g_ref[...] * out_scale).astype(BF16)


def _attn_c(p, pc, lam, subln2, lam_init, latent):
    src = p if latent else pc
    b, s, _ = src.shape
    lc = pc.shape[1]
    tq = C_Q_TILE if latent else s
    npair = C_HEADS // 2
    in_specs = [pl.BlockSpec(memory_space=pltpu.SMEM),
                pl.BlockSpec((1, tq, LANES), lambda bi, hp, i: (bi, i, OFF_CQ // LANES + hp))]
    args = [lam, src]
    if latent:
        in_specs += [pl.BlockSpec((1, s, LANES), lambda bi, hp, i: (bi, 0, OFF_CK // LANES + hp)),
                     pl.BlockSpec((1, s, LANES), lambda bi, hp, i: (bi, 0, OFF_CV // LANES + hp))]
        args += [p, p]
    in_specs += [pl.BlockSpec((1, lc, LANES), lambda bi, hp, i: (bi, 0, OFF_CK // LANES + hp)),
                 pl.BlockSpec((1, lc, LANES), lambda bi, hp, i: (bi, 0, OFF_CV // LANES + hp)),
                 pl.BlockSpec((1, LANES), lambda bi, hp, i: (0, 0))]
    args += [pc, pc, subln2]
    return pl.pallas_call(
        functools.partial(_attn_c_kernel, latent=latent, out_scale=1.0 - lam_init),
        out_shape=jax.ShapeDtypeStruct((b, s, C_HEADS * C_V_DIM), BF16),
        grid=(b, npair, s // tq),
        in_specs=in_specs,
        out_specs=pl.BlockSpec((1, tq, LANES), lambda bi, hp, i: (bi, i, hp)),
        compiler_params=_cparams(("parallel", "parallel", "arbitrary")),
        name="attn_differential" + ("" if latent else "_ctx"),
    )(*args)


def _outproj_kernel(oa_ref, ob_ref, oc_ref, x_ref, w_ref, gpost_ref, gt_ref, gpre_ref, sc_ref, sh_ref,
                    x1_ref, h2_ref):
    wa = oa_ref.shape[2]
    wb = ob_ref.shape[2]
    y = (_dot(oa_ref[0], w_ref[0:wa, :]) + _dot(ob_ref[0], w_ref[wa:wa + wb, :])
         + _dot(oc_ref[0], w_ref[wa + wb:, :]))
    x1 = x_ref[0] + gt_ref[0] * _rms(y, gpost_ref[...])
    x1_ref[0] = x1
    h2_ref[0] = _rms(x1, gpre_ref[...]) * (1.0 + sc_ref[0]) + sh_ref[0]


def _outproj(oa, ob, oc, x, w_out, g_post, gt, g_pre, sc, sh, tm):
    b, s, d = x.shape
    per_sample = gt.shape[0] > 1
    mod_map = (lambda bi, i: (bi, 0, 0)) if per_sample else (lambda bi, i: (0, 0, 0))
    row = lambda bi, i: (bi, i, 0)
    const2 = lambda bi, i: (0, 0)
    return pl.pallas_call(
        _outproj_kernel,
        out_shape=(jax.ShapeDtypeStruct((b, s, d), F32), jax.ShapeDtypeStruct((b, s, d), F32)),
        grid=(b, s // tm),
        in_specs=[
            pl.BlockSpec((1, tm, oa.shape[2]), row),
            pl.BlockSpec((1, tm, ob.shape[2]), row),
            pl.BlockSpec((1, tm, oc.shape[2]), row),
            pl.BlockSpec((1, tm, d), row),
            pl.BlockSpec(w_out.shape, const2),
            pl.BlockSpec((1, d), const2),
            pl.BlockSpec((1, 1, d), mod_map),
            pl.BlockSpec((1, d), const2),
            pl.BlockSpec((1, 1, d), mod_map),
            pl.BlockSpec((1, 1, d), mod_map),
        ],
        out_specs=(pl.BlockSpec((1, tm, d), row), pl.BlockSpec((1, tm, d), row)),
        compiler_params=_cparams(("parallel", "parallel")),
        name="outproj_residual_norm",
    )(oa, ob, oc, x, w_out, g_post.reshape(1, d), gt, g_pre.reshape(1, d), sc, sh)


ROW_UNROLL = 8


GATHER_ROWS = 16


def _gather_kernel(idx_ref, h_ref, o_ref):
    cap, d = o_ref.shape[2], o_ref.shape[3]
    sub = lax.broadcasted_iota(jnp.int32, (ROW_UNROLL, d), 0)

    def body(j, carry):
        base = pl.multiple_of(j * GATHER_ROWS, GATHER_ROWS)
        halves = []
        for g in range(GATHER_ROWS // ROW_UNROLL):
            tile = jnp.zeros((ROW_UNROLL, d), F32)
            for u in range(ROW_UNROLL):
                n = idx_ref[0, 0, base + g * ROW_UNROLL + u]
                row = jnp.broadcast_to(h_ref[0, pl.ds(n, 1), :], (ROW_UNROLL, d))
                tile = jnp.where(sub == u, row, tile)
            halves.append(tile)
        o_ref[0, 0, pl.ds(base, GATHER_ROWS), :] = jnp.concatenate(halves, axis=0).astype(BF16)
        return carry

    lax.fori_loop(0, cap // GATHER_ROWS, body, 0)


def _moe_gather(h2, idx):
    b, n, d = h2.shape
    e, cap = idx.shape[1], idx.shape[2]
    return pl.pallas_call(
        _gather_kernel,
        out_shape=jax.ShapeDtypeStruct((e, b, cap, d), BF16),
        grid=(b, e),
        in_specs=[
            pl.BlockSpec((1, 1, cap), lambda bi, ei: (bi * e + ei, 0, 0), memory_space=pltpu.SMEM),
            pl.BlockSpec((1, n, d), lambda bi, ei: (bi, 0, 0)),
        ],
        out_specs=pl.BlockSpec((1, 1, cap, d), lambda bi, ei: (ei, bi, 0, 0)),
        compiler_params=_cparams(("parallel", "arbitrary")),
        name="moe_gather",
    )(idx.reshape(b * e, 1, cap), h2)


FF_CHUNK = 512


def _ffn_kernel(xs_ref, gate_ref, wg_ref, wu_ref, wd_ref, o_ref):
    xs = xs_ref[0]
    ff = wg_ref.shape[2]
    acc = None
    for c in range(ff // FF_CHUNK):
        cs = slice(c * FF_CHUNK, (c + 1) * FF_CHUNK)
        g = _dot(xs, wg_ref[0, :, cs])
        u = _dot(xs, wu_ref[0, :, cs])
        hid = ((g / (1.0 + jnp.exp(-g))) * u).astype(BF16)
        part = _dot(hid, wd_ref[0, cs, :])
        acc = part if acc is None else acc + part
    o_ref[0] = acc * gate_ref[0]


FFN_ROWS = 512


def _moe_ffn(xs, gates, wg, wu, wd, layer):
    e, r, d = xs.shape
    ff = wg.shape[2]
    tr = min(r, FFN_ROWS)
    return pl.pallas_call(
        _ffn_kernel,
        out_shape=jax.ShapeDtypeStruct((e, r, d), F32),
        grid=(e, r // tr),
        in_specs=[
            pl.BlockSpec((1, tr, d), lambda ei, ri: (ei, ri, 0)),
            pl.BlockSpec((1, tr, 1), lambda ei, ri: (ei, ri, 0)),
            pl.BlockSpec((1, d, ff), lambda ei, ri: (layer * e + ei, 0, 0)),
            pl.BlockSpec((1, d, ff), lambda ei, ri: (layer * e + ei, 0, 0)),
            pl.BlockSpec((1, ff, d), lambda ei, ri: (layer * e + ei, 0, 0)),
        ],
        out_specs=pl.BlockSpec((1, tr, d), lambda ei, ri: (ei, ri, 0)),
        compiler_params=_cparams(("parallel", "arbitrary")),
        name="moe_expert_ffn",
    )(xs, gates, wg, wu, wd)


def _combine_kernel(idx_ref, y_ref, o_ref):
    cap = y_ref.shape[2]

    @pl.when(pl.program_id(1) == 0)
    def _():
        o_ref[...] = jnp.zeros_like(o_ref)

    def body(j, carry):
        base = pl.multiple_of(j * ROW_UNROLL, ROW_UNROLL)
        toks = [idx_ref[0, 0, base + u] for u in range(ROW_UNROLL)]
        ys = y_ref[0, 0, pl.ds(base, ROW_UNROLL), :]
        rows = [o_ref[0, pl.ds(n, 1), :] for n in toks]
        for u, n in enumerate(toks):
            o_ref[0, pl.ds(n, 1), :] = rows[u] + ys[u:u + 1]
        return carry

    lax.fori_loop(0, cap // ROW_UNROLL, body, 0)


def _moe_combine(y, idx, n):
    e, b, cap, d = y.shape
    return pl.pallas_call(
        _combine_kernel,
        out_shape=jax.ShapeDtypeStruct((b, n, d), F32),
        grid=(b, e),
        in_specs=[
            pl.BlockSpec((1, 1, cap), lambda bi, ei: (bi * e + ei, 0, 0), memory_space=pltpu.SMEM),
            pl.BlockSpec((1, 1, cap, d), lambda bi, ei: (ei, bi, 0, 0)),
        ],
        out_specs=pl.BlockSpec((1, n, d), lambda bi, ei: (bi, 0, 0)),
        compiler_params=_cparams(("parallel", "arbitrary")),
        name="moe_combine",
    )(idx.reshape(b * e, 1, cap), y)


def _resid_kernel(x_ref, y_ref, g_ref, gt_ref, o_ref):
    o_ref[0] = x_ref[0] + gt_ref[0] * _rms(y_ref[0], g_ref[...])


def _resid(x, y, g, gt, tm):
    b, s, d = x.shape
    per_sample = gt.shape[0] > 1
    mod_map = (lambda bi, i: (bi, 0, 0)) if per_sample else (lambda bi, i: (0, 0, 0))
    row = lambda bi, i: (bi, i, 0)
    return pl.pallas_call(
        _resid_kernel,
        out_shape=jax.ShapeDtypeStruct((b, s, d), F32),
        grid=(b, s // tm),
        in_specs=[pl.BlockSpec((1, tm, d), row), pl.BlockSpec((1, tm, d), row),
                  pl.BlockSpec((1, d), lambda bi, i: (0, 0)), pl.BlockSpec((1, 1, d), mod_map)],
        out_specs=pl.BlockSpec((1, tm, d), row),
        compiler_params=_cparams(("parallel", "parallel")),
        name="ffn_residual",
    )(x, y, g.reshape(1, d), gt)


ROUTE_SLOT_LO = 32
ROUTE_TOK_SPLIT = 64


def _count(mask):
    return jnp.sum(jnp.where(mask, 1.0, 0.0), axis=1, keepdims=True)


def _route_kernel(h_ref, wr_ref, idx_ref, gate_ref, logit_ref, aff_ref, posm_ref, *, cap, tn):
    i = pl.program_id(1)
    n_exp, n_tok = logit_ref.shape
    h = h_ref[0]
    h_hi = h.astype(BF16)
    h_lo = (h - h_hi.astype(F32)).astype(BF16)
    w = wr_ref[...]
    w_hi = w.astype(BF16)
    w_lo = (w - w_hi.astype(F32)).astype(BF16)
    logit_ref[:, pl.ds(pl.multiple_of(i * tn, tn), tn)] = (
        _dot_nt(w_hi, h_hi) + (_dot_nt(w_hi, h_lo) + _dot_nt(w_lo, h_hi)))

    @pl.when(i == pl.num_programs(1) - 1)
    def _():
        lg = logit_ref[...]
        ex = jnp.exp(lg - jnp.max(lg, axis=0, keepdims=True))
        aff = ex / jnp.sum(ex, axis=0, keepdims=True)
        aff_ref[...] = aff
        capf = float(cap)

        def tbody(it, t):
            cand = t | jnp.left_shift(jnp.int32(1), 30 - it)
            cnt = _count(aff >= lax.bitcast_convert_type(cand, F32))
            return jnp.where(cnt >= capf, cand, t)

        t = lax.fori_loop(0, 31, tbody, jnp.zeros((n_exp, 1), jnp.int32))
        above = aff >= lax.bitcast_convert_type(t + 1, F32)
        tied = jnp.logical_and(aff >= lax.bitcast_convert_type(t, F32), jnp.logical_not(above))
        need = capf - _count(above)
        tok = lax.broadcasted_iota(jnp.int32, (1, n_tok), 1)
        nbits = n_tok.bit_length()

        def mbody(it, bound):
            cand = bound | jnp.left_shift(jnp.int32(1), nbits - 1 - it)
            f = _count(jnp.logical_and(tied, tok < cand))
            return jnp.where(f <= need, cand, bound)

        bound = lax.fori_loop(0, nbits, mbody, jnp.zeros((n_exp, 1), jnp.int32))
        sel = jnp.logical_or(above, jnp.logical_and(tied, tok < bound))
        self = jnp.where(sel, 1.0, 0.0)
        csum = self
        shift = 1
        while shift < n_tok:
            csum = csum + jnp.where(tok >= shift, pltpu.roll(csum, shift, 1), 0.0)
            shift *= 2
        posm_ref[...] = jnp.where(sel, csum - self, -1.0)

        tok_hi = (tok // ROUTE_TOK_SPLIT).astype(F32)
        tok_lo = (tok % ROUTE_TOK_SPLIT).astype(F32)
        n_hi = cap // ROUTE_SLOT_LO
        hi_iota = lax.broadcasted_iota(jnp.int32, (n_hi, 1), 0)
        lo_iota = lax.broadcasted_iota(jnp.int32, (ROUTE_SLOT_LO, 1), 0)
        pad_rows = (-5 * n_hi) % 16
        zeros = [jnp.zeros((pad_rows, n_tok), F32)] if pad_rows else []

        def ebody(e, carry):
            slot = posm_ref[pl.ds(e, 1), :].astype(jnp.int32)
            in_hi = (slot // ROUTE_SLOT_LO) == hi_iota
            lo_hot = jnp.where((slot % ROUTE_SLOT_LO) == lo_iota, 1.0, 0.0).astype(BF16)
            a = aff_ref[pl.ds(e, 1), :]
            a_hi = a.astype(BF16).astype(F32)
            a_mid = (a - a_hi).astype(BF16).astype(F32)
            a_lo = (a - a_hi) - a_mid
            lhs = jnp.concatenate([jnp.where(in_hi, v, 0.0) for v in (tok_hi, tok_lo, a_hi, a_mid, a_lo)]
                                  + zeros, axis=0).astype(BF16)
            r = _dot_nt(lhs, lo_hot)
            idx_ref[0, e] = (r[0:n_hi] * float(ROUTE_TOK_SPLIT) + r[n_hi:2 * n_hi]).astype(jnp.int32)
            gate_ref[0, e] = r[2 * n_hi:3 * n_hi] + (r[3 * n_hi:4 * n_hi] + r[4 * n_hi:5 * n_hi])
            return carry

        lax.fori_loop(0, n_exp, ebody, 0)


def _route(h2, w_router, cap):
    b, n, d = h2.shape
    e = w_router.shape[1]
    tn = min(n, 1024)
    n_hi = cap // ROUTE_SLOT_LO
    return pl.pallas_call(
        functools.partial(_route_kernel, cap=cap, tn=tn),
        out_shape=(jax.ShapeDtypeStruct((b, e, n_hi, ROUTE_SLOT_LO), jnp.int32),
                   jax.ShapeDtypeStruct((b, e, n_hi, ROUTE_SLOT_LO), F32)),
        grid=(b, n // tn),
        in_specs=[pl.BlockSpec((1, tn, d), lambda bi, i: (bi, i, 0)),
                  pl.BlockSpec((e, d), lambda bi, i: (0, 0))],
        out_specs=(pl.BlockSpec((1, e, n_hi, ROUTE_SLOT_LO), lambda bi, i: (bi, 0, 0, 0)),
                   pl.BlockSpec((1, e, n_hi, ROUTE_SLOT_LO), lambda bi, i: (bi, 0, 0, 0))),
        scratch_shapes=[pltpu.VMEM((e, n), F32), pltpu.VMEM((e, n), F32), pltpu.VMEM((e, n), F32)],
        compiler_params=_cparams(("parallel", "arbitrary")),
        name="moe_route",
    )(h2, w_router.T)


def _moe(h2, w_router, wg, wu, wd, layer):
    b, n, d = h2.shape
    e = w_router.shape[1]
    cap = n * CAPACITY_FACTOR // e
    idx, gates = _route(h2, w_router, cap)
    idx = idx.reshape(b, e, cap)
    xs = _moe_gather(h2, idx)
    y = _moe_ffn(xs.reshape(e, b * cap, d), gates.reshape(b, e, cap).transpose(1, 0, 2).reshape(e, b * cap, 1),
                 wg, wu, wd, layer)
    return _moe_combine(y.reshape(e, b, cap, d), idx, n)


def _swap_rope_halves(w, half):
    d, n = w.shape
    return w.reshape(d, n // (2 * half), 2, half)[:, :, ::-1, :].reshape(d, n)


def _inproj_weights(w):
    d = w.shape[0]
    sec = np.cumsum([0, 384, 128, 128, 384, 384, 384, 256, 256, 256])
    aq, ak, av, bq, bk, bv, cq, ck, cv = [w[:, sec[i]:sec[i + 1]] for i in range(9)]
    aq = aq.reshape(d, A_KV_HEADS, A_Q_PER_KV, HEAD_DIM).transpose(0, 2, 1, 3).reshape(d, -1)
    rope = [aq, ak, cq, ck]
    halves = [HEAD_DIM // 4, HEAD_DIM // 4, C_QK_DIM // 4, C_QK_DIM // 4]
    partner = [_swap_rope_halves(m, h) for m, h in zip(rope, halves)]
    return jnp.concatenate(rope + [av, bq * QK_SCALE_64, bk, bv, cv] + partner, axis=1).astype(BF16)


def _outproj_weights(w):
    d = w.shape[1]
    wa = A_Q_HEADS * HEAD_DIM
    a = w[:wa].reshape(A_KV_HEADS, A_Q_PER_KV, HEAD_DIM, d).transpose(1, 0, 2, 3).reshape(wa, d)
    return jnp.concatenate([a, w[wa:]], axis=0).astype(BF16)


def _rope_tables(s):
    t = jnp.arange(s)
    rows, cols = t // GRID_W, t % GRID_W

    def head_tables(d):
        q = d // 4
        inv = ROPE_THETA ** (-jnp.arange(q, dtype=F32) / q)
        cs, sn = [], []
        for pos in (rows, cols):
            ang = pos.astype(F32)[:, None] * inv[None, :]
            c, s_ = jnp.cos(ang), jnp.sin(ang)
            cs += [c, c]
            sn += [-s_, s_]
        c = jnp.concatenate(cs, axis=1)
        s_ = jnp.concatenate(sn, axis=1)
        reps = LANES // d
        return jnp.tile(c, (1, reps)), jnp.tile(s_, (1, reps))

    ca, sa = head_tables(HEAD_DIM)
    cc, sc = head_tables(C_QK_DIM)
    cos = jnp.concatenate([ca * QK_SCALE_64, ca, cc * QK_SCALE_32, cc], axis=1)
    sin = jnp.concatenate([sa * QK_SCALE_64, sa, sc * QK_SCALE_32, sc], axis=1)
    return cos, sin


def _ctx_tables(lc):
    ones = jnp.ones((lc, LANES), F32)
    cos = jnp.concatenate([ones * QK_SCALE_64, ones, ones * QK_SCALE_32, ones], axis=1)
    return cos, jnp.zeros_like(cos)


def kernel(x, c, ctx, c_ctx, w_ada, b_ada, g_mix_pre, g_mix_post, g_ffn_pre, g_ffn_post, w_in, w_out,
           a_sink, b_rpb, c_lam_q1, c_lam_k1, c_lam_q2, c_lam_k2, c_subln, w_router, w_gate, w_up, w_down):
    b, s, d = x.shape
    lc = ctx.shape[1]
    depth = w_in.shape[0]

    cos_x, sin_x = _rope_tables(s)
    cos_c, sin_c = _ctx_tables(lc)

    pad = (-(b + 1)) % 8
    cstack = jnp.concatenate([c, c_ctx[None, :], jnp.zeros((pad, d), F32)], axis=0)
    mod = _modulation(cstack, w_ada, b_ada)

    n_exp, ff = w_gate.shape[1], w_gate.shape[3]
    wg = w_gate.astype(BF16).reshape(depth * n_exp, d, ff)
    wu = w_up.astype(BF16).reshape(depth * n_exp, d, ff)
    wd = w_down.astype(BF16).reshape(depth * n_exp, ff, d)

    tm = 512
    h_ctx = ctx
    for l in range(depth):
        need_ctx = l < depth - 1
        mx = mod[l, :b].reshape(b, 1, N_MOD, d)
        sh1, sc1, gt1, sh2, sc2, gt2 = [mx[:, :, k] for k in range(N_MOD)]
        mc = mod[l, b:b + 1].reshape(1, 1, N_MOD, d)
        csh1, csc1, cgt1, csh2, csc2, cgt2 = [mc[:, :, k] for k in range(N_MOD)]

        w_cat = _inproj_weights(w_in[l])
        w_o = _outproj_weights(w_out[l])
        lam_init = 0.8 - 0.6 * math.exp(-0.3 * l)
        lam = (jnp.exp(jnp.sum(c_lam_q1[l] * c_lam_k1[l])) - jnp.exp(jnp.sum(c_lam_q2[l] * c_lam_k2[l]))
               + lam_init).reshape(1).astype(F32)
        subln2 = jnp.tile(c_subln[l], 2).reshape(1, LANES)
        bias = _b_bias_tables(b_rpb[l], s // GRID_W, LOG2E)
        sink = a_sink[l] * LOG2E

        p = _inproj(x, g_mix_pre[l], sc1, sh1, cos_x, sin_x, w_cat, tm)
        pc = _inproj(h_ctx, g_mix_pre[l], csc1, csh1, cos_c, sin_c, w_cat, lc)

        oa = _attn_a(p, pc, sink, True)
        ob = _attn_b(p, pc, bias)
        od = _attn_c(p, pc, lam, subln2, lam_init, True)
        x1, h2 = _outproj(oa, ob, od, x, w_o, g_mix_post[l], gt1, g_ffn_pre[l], sc2, sh2, tm)
        if need_ctx:
            oac = _attn_a(p, pc, sink, False)
            obc = _attn_b_ctx(pc)
            odc = _attn_c(p, pc, lam, subln2, lam_init, False)
            c1, hc2 = _outproj(oac, obc, odc, h_ctx, w_o, g_mix_post[l], cgt1, g_ffn_pre[l], csc2, csh2, lc)
            yc = _moe(hc2, w_router[l], wg, wu, wd, l)
            h_ctx = _resid(c1, yc, g_ffn_post[l], cgt2, lc)
        y = _moe(h2, w_router[l], wg, wu, wd, l)
        x = _resid(x1, y, g_ffn_post[l], gt2, tm)
    return x
```

```python
import functools
import math

import numpy as np
import jax
import jax.numpy as jnp
from jax import lax
from jax.experimental import pallas as pl
from jax.experimental.pallas import tpu as pltpu

F32 = jnp.float32
BF16 = jnp.bfloat16

GRID_W = 64
HEAD_DIM = 64
A_Q_HEADS, A_KV_HEADS = 6, 2
A_Q_PER_KV = A_Q_HEADS // A_KV_HEADS
A_WINDOW = 128
A_BLOCK = 128
B_HEADS, B_MAX_ROWS, B_COLS = 6, 8, 16
C_HEADS, C_QK_DIM, C_V_DIM = 4, 32, 64
N_EXPERTS = 16
CAPACITY_FACTOR = 2
N_MOD = 6
ROPE_THETA = 10000.0
EPS = 1e-6
NEG_INF = -1e30
LOG2E = math.log2(math.e)
QK_SCALE_64 = HEAD_DIM ** -0.5 * LOG2E
QK_SCALE_32 = C_QK_DIM ** -0.5 * LOG2E

LANES = 128
ROPE_W = 1024
OFF_AQ, OFF_AK, OFF_CQ, OFF_CK, OFF_AV, OFF_BQ, OFF_BK, OFF_BV, OFF_CV = (
    0, 384, 512, 768, 1024, 1152, 1536, 1920, 2304)
IN_W = 2560
VMEM_LIMIT = 56 * 1024 * 1024


def _cparams(sem, vmem_limit=VMEM_LIMIT):
    return pltpu.CompilerParams(dimension_semantics=sem, vmem_limit_bytes=vmem_limit)


def _dot(a, b):
    return jnp.dot(a, b, preferred_element_type=F32)


def _dot_nt(a, b):
    return lax.dot_general(a, b, (((1,), (1,)), ((), ())), preferred_element_type=F32)


def _rms(x, g):
    return x * lax.rsqrt(jnp.mean(x * x, axis=-1, keepdims=True) + EPS) * g


def _mod_kernel(c_ref, w_ref, b_ref, o_ref):
    c = c_ref[...]
    a = c / (1.0 + jnp.exp(-c))
    o_ref[0] = jnp.dot(a, w_ref[0], preferred_element_type=F32,
                       precision=lax.Precision.HIGHEST) + b_ref[0]


def _modulation(cstack, w_ada, b_ada):
    nl, d, n = w_ada.shape
    rows = cstack.shape[0]
    tn = 1536
    return pl.pallas_call(
        _mod_kernel,
        out_shape=jax.ShapeDtypeStruct((nl, rows, n), F32),
        grid=(nl, n // tn),
        in_specs=[
            pl.BlockSpec((rows, d), lambda l, j: (0, 0)),
            pl.BlockSpec((1, d, tn), lambda l, j: (l, 0, j)),
            pl.BlockSpec((1, 1, tn), lambda l, j: (l, 0, j)),
        ],
        out_specs=pl.BlockSpec((1, rows, tn), lambda l, j: (l, 0, j)),
        compiler_params=_cparams(("parallel", "parallel")),
        name="adaln_modulation",
    )(cstack, w_ada, b_ada.reshape(nl, 1, n))


_ROPE_TABLE_OF_UNIT = (0, 0, 0, 1, 2, 2, 3, 3)


_ROPE_HALF_OF_CHUNK = (HEAD_DIM // 4, HEAD_DIM // 4, C_QK_DIM // 4, C_QK_DIM // 4)


def _inproj_kernel(x_ref, g_ref, sc_ref, sh_ref, cos_ref, sin_ref, w_ref, o_ref):
    x = x_ref[0]
    h = _rms(x, g_ref[...]) * (1.0 + sc_ref[0]) + sh_ref[0]
    hb = h.astype(BF16)
    for j in range(ROPE_W // 256):
        t0, t1 = _ROPE_TABLE_OF_UNIT[2 * j], _ROPE_TABLE_OF_UNIT[2 * j + 1]
        cos = jnp.concatenate([cos_ref[:, t0 * LANES:(t0 + 1) * LANES],
                               cos_ref[:, t1 * LANES:(t1 + 1) * LANES]], axis=1)
        sin = jnp.concatenate([sin_ref[:, t0 * LANES:(t0 + 1) * LANES],
                               sin_ref[:, t1 * LANES:(t1 + 1) * LANES]], axis=1)
        p = _dot(hb, w_ref[:, 256 * j:256 * (j + 1)])
        half = _ROPE_HALF_OF_CHUNK[j]
        lane = lax.broadcasted_iota(jnp.int32, (1, 256), 1)
        ps = jnp.where((lane & (2 * half - 1)) < half, pltpu.roll(p, 256 - half, 1), pltpu.roll(p, half, 1))
        o_ref[0, :, 256 * j:256 * (j + 1)] = (p * cos + ps * sin).astype(BF16)
    for j in range(ROPE_W // 256, IN_W // 256):
        o_ref[0, :, 256 * j:256 * (j + 1)] = _dot(hb, w_ref[:, 256 * j:256 * (j + 1)]).astype(BF16)


def _inproj(x, g, sc, sh, cos, sin, w_cat, tm):
    b, s, d = x.shape
    per_sample = sc.shape[0] > 1
    mod_map = (lambda bi, i: (bi, 0, 0)) if per_sample else (lambda bi, i: (0, 0, 0))
    return pl.pallas_call(
        _inproj_kernel,
        out_shape=jax.ShapeDtypeStruct((b, s, IN_W), BF16),
        grid=(b, s // tm),
        in_specs=[
            pl.BlockSpec((1, tm, d), lambda bi, i: (bi, i, 0)),
            pl.BlockSpec((1, d), lambda bi, i: (0, 0)),
            pl.BlockSpec((1, 1, d), mod_map),
            pl.BlockSpec((1, 1, d), mod_map),
            pl.BlockSpec((tm, 4 * LANES), lambda bi, i: (i, 0)),
            pl.BlockSpec((tm, 4 * LANES), lambda bi, i: (i, 0)),
            pl.BlockSpec((d, IN_W), lambda bi, i: (0, 0)),
        ],
        out_specs=pl.BlockSpec((1, tm, IN_W), lambda bi, i: (bi, i, 0)),
        compiler_params=_cparams(("parallel", "parallel")),
        name="norm_inproj_rope",
    )(x, g.reshape(1, d), sc, sh, cos, sin, w_cat)


def _lo_mask():
    return lax.broadcasted_iota(jnp.int32, (1, LANES), 1) < HEAD_DIM


def _split_halves(q):
    lo = _lo_mask()
    zero = jnp.zeros_like(q)
    return jnp.where(lo, q, zero), jnp.where(lo, zero, q)


A_STEP_BLOCKS = 4


def _attn_a_kernel(sink_ref, q_ref, *refs, seq, latent):
    if latent:
        k_ref, v_ref, kc_ref, vc_ref, o_ref = refs
    else:
        kc_ref, vc_ref, o_ref = refs
    tq = q_ref.shape[1]
    npair = A_Q_HEADS // 2
    kc = kc_ref[0]
    vc = vc_ref[0]
    lo = _lo_mask()
    if latent:
        i = pl.program_id(1)
        band = tq + 2 * A_BLOCK
        start = pl.multiple_of(jnp.clip(i * tq - A_BLOCK, 0, seq - band), A_BLOCK)
        kb = k_ref[0, pl.ds(start, band), :]
        vb = v_ref[0, pl.ds(start, band), :]
        delta = ((start - i * tq) + lax.broadcasted_iota(jnp.int32, (1, band), 1)
                 - lax.broadcasted_iota(jnp.int32, (tq, 1), 0))
        wmask = jnp.where(jnp.abs(delta) <= A_WINDOW, 0.0, NEG_INF)
        wmask = jnp.concatenate([wmask, wmask], axis=0)
    outs = []
    for t in range(npair):
        qa, qb = _split_halves(q_ref[0, :, t * LANES:(t + 1) * LANES])
        q2 = jnp.concatenate([qa, qb], axis=0)
        sink = jnp.concatenate([jnp.full((tq, 1), sink_ref[t], F32),
                                jnp.full((tq, 1), sink_ref[npair + t], F32)], axis=0)
        s_ctx = _dot_nt(q2, kc)
        m = jnp.maximum(jnp.max(s_ctx, axis=-1, keepdims=True), sink)
        if latent:
            s_lat = _dot_nt(q2, kb) + wmask
            m = jnp.maximum(m, jnp.max(s_lat, axis=-1, keepdims=True))
            e_lat = jnp.exp2(s_lat - m)
        e_ctx = jnp.exp2(s_ctx - m)
        l = jnp.sum(e_ctx, axis=-1, keepdims=True) + jnp.exp2(sink - m)
        o = _dot(e_ctx.astype(BF16), vc)
        if latent:
            l = l + jnp.sum(e_lat, axis=-1, keepdims=True)
            o = o + _dot(e_lat.astype(BF16), vb)
        o = o / l
        outs.append(jnp.where(lo, o[:tq], o[tq:]))
    o_ref[0] = jnp.concatenate(outs, axis=1).astype(BF16)


def _attn_a(p, pc, sink, latent):
    src = p if latent else pc
    b, s, _ = src.shape
    lc = pc.shape[1]
    tq = A_STEP_BLOCKS * A_BLOCK if latent else s
    wq = A_Q_HEADS * HEAD_DIM
    in_specs = [pl.BlockSpec(memory_space=pltpu.SMEM),
                pl.BlockSpec((1, tq, wq), lambda bi, i: (bi, i, OFF_AQ // wq))]
    args = [sink, src]
    if latent:
        in_specs += [pl.BlockSpec((1, s, LANES), lambda bi, i: (bi, 0, OFF_AK // LANES)),
                     pl.BlockSpec((1, s, LANES), lambda bi, i: (bi, 0, OFF_AV // LANES))]
        args += [p, p]
    in_specs += [pl.BlockSpec((1, lc, LANES), lambda bi, i: (bi, 0, OFF_AK // LANES)),
                 pl.BlockSpec((1, lc, LANES), lambda bi, i: (bi, 0, OFF_AV // LANES))]
    args += [pc, pc]
    return pl.pallas_call(
        functools.partial(_attn_a_kernel, seq=s, latent=latent),
        out_shape=jax.ShapeDtypeStruct((b, s, wq), BF16),
        grid=(b, s // tq),
        in_specs=in_specs,
        out_specs=pl.BlockSpec((1, tq, wq), lambda bi, i: (bi, i, 0)),
        compiler_params=_cparams(("parallel", "arbitrary")),
        name="attn_window_gqa" + ("" if latent else "_ctx"),
    )(*args)


B_GROUP_ROWS = 4
B_UNION_ROWS = B_GROUP_ROWS + B_MAX_ROWS


def _attn_b_softmax_out(s_parts, v_parts):
    m = functools.reduce(jnp.maximum, [jnp.max(s, axis=-1, keepdims=True) for s in s_parts])
    es = [jnp.exp2(s - m) for s in s_parts]
    l = functools.reduce(jnp.add, [jnp.sum(e, axis=-1, keepdims=True) for e in es])
    o = functools.reduce(jnp.add, [_dot(e.astype(BF16), v) for e, v in zip(es, v_parts)])
    return o / l


def _attn_b_kernel(q_ref, k_ref, v_ref, kc_ref, vc_ref, bias_ref, o_ref, *, rows_n):
    i = pl.program_id(1)
    npair = B_HEADS // 2
    lo = _lo_mask()
    tq = q_ref.shape[1]
    nk = B_UNION_ROWS * GRID_W
    us = jnp.clip(i * B_GROUP_ROWS - B_MAX_ROWS // 2, 0, rows_n - B_UNION_ROWS)
    kstart = pl.multiple_of(us * GRID_W, GRID_W)
    outs = []
    for t in range(npair):
        cols = slice(t * LANES, (t + 1) * LANES)
        qa, qb = _split_halves(q_ref[0, :, cols])
        q2 = jnp.concatenate([qa, qb], axis=0)
        s_lat = _dot_nt(q2, k_ref[0, pl.ds(kstart, nk), cols]) + bias_ref[0, t]
        s_ctx = _dot_nt(q2, kc_ref[0, :, cols])
        o = _attn_b_softmax_out([s_lat, s_ctx], [v_ref[0, pl.ds(kstart, nk), cols], vc_ref[0, :, cols]])
        outs.append(jnp.where(lo, o[:tq], o[tq:]))
    o_ref[0] = jnp.concatenate(outs, axis=1).astype(BF16)


def _attn_b_ctx_kernel(q_ref, kc_ref, vc_ref, o_ref):
    npair = B_HEADS // 2
    lo = _lo_mask()
    n = q_ref.shape[1]
    for t in range(npair):
        cols = slice(t * LANES, (t + 1) * LANES)
        qa, qb = _split_halves(q_ref[0, :, cols])
        q2 = jnp.concatenate([qa, qb], axis=0)
        s_ctx = _dot_nt(q2, kc_ref[0, :, cols])
        o = _attn_b_softmax_out([s_ctx], [vc_ref[0, :, cols]])
        o_ref[0, :, cols] = jnp.where(lo, o[:n], o[n:]).astype(BF16)


def _attn_b(p, pc, bias):
    b, s, _ = p.shape
    lc = pc.shape[1]
    w = B_HEADS * HEAD_DIM
    rows_n = s // GRID_W
    tq = B_GROUP_ROWS * GRID_W
    ng = s // tq
    variant = lambda bi, i: (jnp.minimum(i, 1) + (i == ng - 1).astype(jnp.int32), 0, 0, 0)
    return pl.pallas_call(
        functools.partial(_attn_b_kernel, rows_n=rows_n),
        out_shape=jax.ShapeDtypeStruct((b, s, w), BF16),
        grid=(b, ng),
        in_specs=[
            pl.BlockSpec((1, tq, w), lambda bi, i: (bi, i, OFF_BQ // w)),
            pl.BlockSpec((1, s, w), lambda bi, i: (bi, 0, OFF_BK // w)),
            pl.BlockSpec((1, s, w), lambda bi, i: (bi, 0, OFF_BV // w)),
            pl.BlockSpec((1, lc, w), lambda bi, i: (bi, 0, OFF_BK // w)),
            pl.BlockSpec((1, lc, w), lambda bi, i: (bi, 0, OFF_BV // w)),
            pl.BlockSpec((1,) + bias.shape[1:], variant),
        ],
        out_specs=pl.BlockSpec((1, tq, w), lambda bi, i: (bi, i, 0)),
        compiler_params=_cparams(("parallel", "arbitrary")),
        name="attn_neighbourhood",
    )(p, p, p, pc, pc, bias)


def _attn_b_ctx(pc):
    b, lc, _ = pc.shape
    w = B_HEADS * HEAD_DIM
    return pl.pallas_call(
        _attn_b_ctx_kernel,
        out_shape=jax.ShapeDtypeStruct((b, lc, w), BF16),
        grid=(b,),
        in_specs=[
            pl.BlockSpec((1, lc, w), lambda bi: (bi, 0, OFF_BQ // w)),
            pl.BlockSpec((1, lc, w), lambda bi: (bi, 0, OFF_BK // w)),
            pl.BlockSpec((1, lc, w), lambda bi: (bi, 0, OFF_BV // w)),
        ],
        out_specs=pl.BlockSpec((1, lc, w), lambda bi: (bi, 0, 0)),
        compiler_params=_cparams(("parallel",)),
        name="attn_neighbourhood_ctx",
    )(pc, pc, pc)


def _b_bias_tables(rpb, rows_n, scale):
    g_rows, u_rows = B_GROUP_ROWS, B_UNION_ROWS
    col = np.arange(GRID_W)
    cstart = np.clip(col - B_COLS // 2, 0, GRID_W - B_COLS)
    col_ok = (col[None, :] >= cstart[:, None]) & (col[None, :] < cstart[:, None] + B_COLS)
    dc_idx = np.clip(col[None, :] - col[:, None], -(B_COLS - 1), B_COLS - 1) + (B_COLS - 1)
    pick_col = (dc_idx[None] == np.arange(2 * B_COLS - 1)[:, None, None]).astype(np.float32)
    n_groups = rows_n // g_rows
    patterns, group_variant = [], []
    for g in range(n_groups):
        us = np.clip(g * g_rows - B_MAX_ROWS // 2, 0, rows_n - u_rows)
        r = g * g_rows + np.arange(g_rows)[:, None]
        rs = np.clip(r - B_MAX_ROWS // 2, 0, rows_n - B_MAX_ROWS)
        key_row = us + np.arange(u_rows)[None, :]
        dr = np.where((key_row >= rs) & (key_row < rs + B_MAX_ROWS), key_row - r + (B_MAX_ROWS - 1), -1)
        if not any(np.array_equal(dr, p_) for p_ in patterns):
            patterns.append(dr)
        group_variant.append([np.array_equal(dr, p_) for p_ in patterns].index(True))
    assert group_variant == [0] + [1] * (n_groups - 2) + [2], group_variant
    dr = np.stack(patterns)
    pick_row = (dr[..., None] == np.arange(2 * B_MAX_ROWS - 1)).astype(np.float32)
    hi = lax.Precision.HIGHEST
    rsel = jnp.einsum('vuwd,hdc->vhuwc', pick_row, rpb.astype(F32), precision=hi)
    t = jnp.einsum('vhuwc,cqk->vhuqwk', rsel, pick_col, precision=hi) * scale
    ok = (dr >= 0)[:, None, :, None, :, None] & col_ok[None, None, None, :, None, :]
    t = jnp.where(ok, t, NEG_INF)
    return t.reshape(len(patterns), B_HEADS // 2, 2 * g_rows * GRID_W, u_rows * GRID_W)


C_KEY_CHUNK = 2048
C_Q_TILE = 256


def _attn_c_kernel(lam_ref, q_ref, *refs, latent, out_scale):
    if latent:
        k_ref, v_ref, kc_ref, vc_ref, g_ref, o_ref = refs
    else:
        kc_ref, vc_ref, g_ref, o_ref = refs
    tq = q_ref.shape[1]
    lam = lam_ref[0]
    q = q_ref[0]
    quarter = lax.broadcasted_iota(jnp.int32, (1, LANES), 1) // C_QK_DIM
    zero = jnp.zeros_like(q)
    q4 = jnp.concatenate([jnp.where(quarter == j, q, zero) for j in range(4)], axis=0)
    chunks = []
    if latent:
        kc_n = min(C_KEY_CHUNK, k_ref.shape[1])
        chunks += [(k_ref, v_ref, c * kc_n, kc_n) for c in range(k_ref.shape[1] // kc_n)]
    chunks.append((kc_ref, vc_ref, 0, kc_ref.shape[1]))
    m = l = acc = None
    for kr, vr, st, n in chunks:
        s = _dot_nt(q4, kr[0, st:st + n, :])
        mc = jnp.max(s, axis=-1, keepdims=True)
        if m is None:
            m = mc
            e = jnp.exp2(s - m)
            l = jnp.sum(e, axis=-1, keepdims=True)
            acc = _dot(e.astype(BF16), vr[0, st:st + n, :])
        else:
            m_new = jnp.maximum(m, mc)
            alpha = jnp.exp2(m - m_new)
            e = jnp.exp2(s - m_new)
            l = l * alpha + jnp.sum(e, axis=-1, keepdims=True)
            acc = acc * alpha + _dot(e.astype(BF16), vr[0, st:st + n, :])
            m = m_new
    o4 = acc / l
    outs = [o4[2 * h * tq:(2 * h + 1) * tq] - lam * o4[(2 * h + 1) * tq:(2 * h + 2) * tq]
            for h in range(2)]
    lo = _lo_mask()
    o = jnp.where(lo, outs[0], outs[1])
    sq = o * o
    s_lo = jnp.sum(jnp.where(lo, sq, 0.0), axis=-1, keepdims=True)
    s_hi = jnp.sum(jnp.where(lo, 0.0, sq), axis=-1, keepdims=True)
    ms = jnp.where(lo, s_lo, s_hi) * (1.0 / C_V_DIM)
    o_ref[0] = (o * lax.rsqrt(ms + EPS) * g_ref[...] * out_scale).astype(BF16)


def _attn_c(p, pc, lam, subln2, lam_init, latent):
    src = p if latent else pc
    b, s, _ = src.shape
    lc = pc.shape[1]
    tq = C_Q_TILE if latent else s
    npair = C_HEADS // 2
    in_specs = [pl.BlockSpec(memory_space=pltpu.SMEM),
                pl.BlockSpec((1, tq, LANES), lambda bi, hp, i: (bi, i, OFF_CQ // LANES + hp))]
    args = [lam, src]
    if latent:
        in_specs += [pl.BlockSpec((1, s, LANES), lambda bi, hp, i: (bi, 0, OFF_CK // LANES + hp)),
                     pl.BlockSpec((1, s, LANES), lambda bi, hp, i: (bi, 0, OFF_CV // LANES + hp))]
        args += [p, p]
    in_specs += [pl.BlockSpec((1, lc, LANES), lambda bi, hp, i: (bi, 0, OFF_CK // LANES + hp)),
                 pl.BlockSpec((1, lc, LANES), lambda bi, hp, i: (bi, 0, OFF_CV // LANES + hp)),
                 pl.BlockSpec((1, LANES), lambda bi, hp, i: (0, 0))]
    args += [pc, pc, subln2]
    return pl.pallas_call(
        functools.partial(_attn_c_kernel, latent=latent, out_scale=1.0 - lam_init),
        out_shape=jax.ShapeDtypeStruct((b, s, C_HEADS * C_V_DIM), BF16),
        grid=(b, npair, s // tq),
        in_specs=in_specs,
        out_specs=pl.BlockSpec((1, tq, LANES), lambda bi, hp, i: (bi, i, hp)),
        compiler_params=_cparams(("parallel", "parallel", "arbitrary")),
        name="attn_differential" + ("" if latent else "_ctx"),
    )(*args)


def _outproj_kernel(oa_ref, ob_ref, oc_ref, x_ref, w_ref, gpost_ref, gt_ref, gpre_ref, sc_ref, sh_ref,
                    x1_ref, h2_ref):
    wa = oa_ref.shape[2]
    wb = ob_ref.shape[2]
    y = (_dot(oa_ref[0], w_ref[0:wa, :]) + _dot(ob_ref[0], w_ref[wa:wa + wb, :])
         + _dot(oc_ref[0], w_ref[wa + wb:, :]))
    x1 = x_ref[0] + gt_ref[0] * _rms(y, gpost_ref[...])
    x1_ref[0] = x1
    h2_ref[0] = _rms(x1, gpre_ref[...]) * (1.0 + sc_ref[0]) + sh_ref[0]


def _outproj(oa, ob, oc, x, w_out, g_post, gt, g_pre, sc, sh, tm):
    b, s, d = x.shape
    per_sample = gt.shape[0] > 1
    mod_map = (lambda bi, i: (bi, 0, 0)) if per_sample else (lambda bi, i: (0, 0, 0))
    row = lambda bi, i: (bi, i, 0)
    const2 = lambda bi, i: (0, 0)
    return pl.pallas_call(
        _outproj_kernel,
        out_shape=(jax.ShapeDtypeStruct((b, s, d), F32), jax.ShapeDtypeStruct((b, s, d), F32)),
        grid=(b, s // tm),
        in_specs=[
            pl.BlockSpec((1, tm, oa.shape[2]), row),
            pl.BlockSpec((1, tm, ob.shape[2]), row),
            pl.BlockSpec((1, tm, oc.shape[2]), row),
            pl.BlockSpec((1, tm, d), row),
            pl.BlockSpec(w_out.shape, const2),
            pl.BlockSpec((1, d), const2),
            pl.BlockSpec((1, 1, d), mod_map),
            pl.BlockSpec((1, d), const2),
            pl.BlockSpec((1, 1, d), mod_map),
            pl.BlockSpec((1, 1, d), mod_map),
        ],
        out_specs=(pl.BlockSpec((1, tm, d), row), pl.BlockSpec((1, tm, d), row)),
        compiler_params=_cparams(("parallel", "parallel")),
        name="outproj_residual_norm",
    )(oa, ob, oc, x, w_out, g_post.reshape(1, d), gt, g_pre.reshape(1, d), sc, sh)


ROW_UNROLL = 8


GATHER_ROWS = 16


def _gather_kernel(idx_ref, h_ref, o_ref):
    cap, d = o_ref.shape[2], o_ref.shape[3]
    sub = lax.broadcasted_iota(jnp.int32, (ROW_UNROLL, d), 0)

    def body(j, carry):
        base = pl.multiple_of(j * GATHER_ROWS, GATHER_ROWS)
        halves = []
        for g in range(GATHER_ROWS // ROW_UNROLL):
            tile = jnp.zeros((ROW_UNROLL, d), F32)
            for u in range(ROW_UNROLL):
                n = idx_ref[0, 0, base + g * ROW_UNROLL + u]
                row = jnp.broadcast_to(h_ref[0, pl.ds(n, 1), :], (ROW_UNROLL, d))
                tile = jnp.where(sub == u, row, tile)
            halves.append(tile)
        o_ref[0, 0, pl.ds(base, GATHER_ROWS), :] = jnp.concatenate(halves, axis=0).astype(BF16)
        return carry

    lax.fori_loop(0, cap // GATHER_ROWS, body, 0)


def _moe_gather(h2, idx):
    b, n, d = h2.shape
    e, cap = idx.shape[1], idx.shape[2]
    return pl.pallas_call(
        _gather_kernel,
        out_shape=jax.ShapeDtypeStruct((e, b, cap, d), BF16),
        grid=(b, e),
        in_specs=[
            pl.BlockSpec((1, 1, cap), lambda bi, ei: (bi * e + ei, 0, 0), memory_space=pltpu.SMEM),
            pl.BlockSpec((1, n, d), lambda bi, ei: (bi, 0, 0)),
        ],
        out_specs=pl.BlockSpec((1, 1, cap, d), lambda bi, ei: (ei, bi, 0, 0)),
        compiler_params=_cparams(("parallel", "arbitrary")),
        name="moe_gather",
    )(idx.reshape(b * e, 1, cap), h2)


FF_CHUNK = 512
CAST_ROWS = 256


def _ffn_kernel(xs_ref, gate_ref, wg_hbm, wu_hbm, wd_hbm, o_ref,
                stage_g, stage_u, stage_d, wg_ref, wu_ref, wd_ref, sems, *, first_expert):
    e = pl.program_id(0)
    r = pl.program_id(1)

    def weight_copies(expert):
        row = first_expert + expert
        return [pltpu.make_async_copy(wg_hbm.at[row], stage_g, sems.at[0]),
                pltpu.make_async_copy(wu_hbm.at[row], stage_u, sems.at[1]),
                pltpu.make_async_copy(wd_hbm.at[row], stage_d, sems.at[2])]

    @pl.when(r == 0)
    def _():
        @pl.when(e == 0)
        def _():
            for c in weight_copies(0):
                c.start()

        for c in weight_copies(e):
            c.wait()
        for src, dst in ((stage_g, wg_ref), (stage_u, wu_ref), (stage_d, wd_ref)):
            for r0 in range(0, src.shape[0], CAST_ROWS):
                dst[r0:r0 + CAST_ROWS, :] = src[r0:r0 + CAST_ROWS, :].astype(BF16)

        @pl.when(e + 1 < pl.num_programs(0))
        def _():
            for c in weight_copies(e + 1):
                c.start()

    xs = xs_ref[0]
    ff = wg_ref.shape[1]
    for c in range(ff // FF_CHUNK):
        cs = slice(c * FF_CHUNK, (c + 1) * FF_CHUNK)
        g = _dot(xs, wg_ref[:, cs])
        u = _dot(xs, wu_ref[:, cs])
        hid = ((g / (1.0 + jnp.exp(-g))) * u).astype(BF16)
        part = _dot(hid, wd_ref[cs, :])
        if c == 0:
            o_ref[0] = part
        elif c < ff // FF_CHUNK - 1:
            o_ref[0] += part
        else:
            o_ref[0] = (o_ref[0] + part) * gate_ref[0]


FFN_ROWS = 512
FFN_VMEM_LIMIT = 61 * 1024 * 1024


def _moe_ffn(xs, gates, wg, wu, wd, layer):
    e, r, d = xs.shape
    ff = wg.shape[2]
    tr = min(r, FFN_ROWS)
    return pl.pallas_call(
        functools.partial(_ffn_kernel, first_expert=layer * e),
        out_shape=jax.ShapeDtypeStruct((e, r, d), F32),
        grid=(e, r // tr),
        in_specs=[
            pl.BlockSpec((1, tr, d), lambda ei, ri: (ei, ri, 0)),
            pl.BlockSpec((1, tr, 1), lambda ei, ri: (ei, ri, 0)),
            pl.BlockSpec(memory_space=pl.ANY),
            pl.BlockSpec(memory_space=pl.ANY),
            pl.BlockSpec(memory_space=pl.ANY),
        ],
        out_specs=pl.BlockSpec((1, tr, d), lambda ei, ri: (ei, ri, 0)),
        scratch_shapes=[pltpu.VMEM((d, ff), F32), pltpu.VMEM((d, ff), F32), pltpu.VMEM((ff, d), F32),
                        pltpu.VMEM((d, ff), BF16), pltpu.VMEM((d, ff), BF16), pltpu.VMEM((ff, d), BF16),
                        pltpu.SemaphoreType.DMA((3,))],
        compiler_params=_cparams(("arbitrary", "arbitrary"), FFN_VMEM_LIMIT),
        name="moe_expert_ffn",
    )(xs, gates, wg, wu, wd)


def _combine_kernel(idx_ref, y_ref, o_ref):
    cap = y_ref.shape[2]

    @pl.when(pl.program_id(1) == 0)
    def _():
        o_ref[...] = jnp.zeros_like(o_ref)

    def body(j, carry):
        base = pl.multiple_of(j * ROW_UNROLL, ROW_UNROLL)
        toks = [idx_ref[0, 0, base + u] for u in range(ROW_UNROLL)]
        ys = y_ref[0, 0, pl.ds(base, ROW_UNROLL), :]
        rows = [o_ref[0, pl.ds(n, 1), :] for n in toks]
        for u, n in enumerate(toks):
            o_ref[0, pl.ds(n, 1), :] = rows[u] + ys[u:u + 1]
        return carry

    lax.fori_loop(0, cap // ROW_UNROLL, body, 0)


def _moe_combine(y, idx, n):
    e, b, cap, d = y.shape
    return pl.pallas_call(
        _combine_kernel,
        out_shape=jax.ShapeDtypeStruct((b, n, d), F32),
        grid=(b, e),
        in_specs=[
            pl.BlockSpec((1, 1, cap), lambda bi, ei: (bi * e + ei, 0, 0), memory_space=pltpu.SMEM),
            pl.BlockSpec((1, 1, cap, d), lambda bi, ei: (ei, bi, 0, 0)),
        ],
        out_specs=pl.BlockSpec((1, n, d), lambda bi, ei: (bi, 0, 0)),
        compiler_params=_cparams(("parallel", "arbitrary")),
        name="moe_combine",
    )(idx.reshape(b * e, 1, cap), y)


def _resid_kernel(x_ref, y_ref, g_ref, gt_ref, o_ref):
    o_ref[0] = x_ref[0] + gt_ref[0] * _rms(y_ref[0], g_ref[...])


def _resid(x, y, g, gt, tm):
    b, s, d = x.shape
    per_sample = gt.shape[0] > 1
    mod_map = (lambda bi, i: (bi, 0, 0)) if per_sample else (lambda bi, i: (0, 0, 0))
    row = lambda bi, i: (bi, i, 0)
    return pl.pallas_call(
        _resid_kernel,
        out_shape=jax.ShapeDtypeStruct((b, s, d), F32),
        grid=(b, s // tm),
        in_specs=[pl.BlockSpec((1, tm, d), row), pl.BlockSpec((1, tm, d), row),
                  pl.BlockSpec((1, d), lambda bi, i: (0, 0)), pl.BlockSpec((1, 1, d), mod_map)],
        out_specs=pl.BlockSpec((1, tm, d), row),
        compiler_params=_cparams(("parallel", "parallel")),
        name="ffn_residual",
    )(x, y, g.reshape(1, d), gt)


ROUTE_SLOT_LO = 32
ROUTE_TOK_SPLIT = 64


def _count(mask):
    return jnp.sum(jnp.where(mask, 1.0, 0.0), axis=1, keepdims=True)


def _route_kernel(h_ref, wr_ref, idx_ref, gate_ref, logit_ref, aff_ref, posm_ref, *, cap, tn):
    i = pl.program_id(1)
    n_exp, n_tok = logit_ref.shape
    h = h_ref[0]
    h_hi = h.astype(BF16)
    h_lo = (h - h_hi.astype(F32)).astype(BF16)
    w = wr_ref[...]
    w_hi = w.astype(BF16)
    w_lo = (w - w_hi.astype(F32)).astype(BF16)
    logit_ref[:, pl.ds(pl.multiple_of(i * tn, tn), tn)] = (
        _dot_nt(w_hi, h_hi) + (_dot_nt(w_hi, h_lo) + _dot_nt(w_lo, h_hi)))

    @pl.when(i == pl.num_programs(1) - 1)
    def _():
        lg = logit_ref[...]
        ex = jnp.exp(lg - jnp.max(lg, axis=0, keepdims=True))
        aff = ex / jnp.sum(ex, axis=0, keepdims=True)
        aff_ref[...] = aff
        capf = float(cap)

        def tbody(it, t):
            cand = t | jnp.left_shift(jnp.int32(1), 30 - it)
            cnt = _count(aff >= lax.bitcast_convert_type(cand, F32))
            return jnp.where(cnt >= capf, cand, t)

        t = lax.fori_loop(0, 31, tbody, jnp.zeros((n_exp, 1), jnp.int32))
        above = aff >= lax.bitcast_convert_type(t + 1, F32)
        tied = jnp.logical_and(aff >= lax.bitcast_convert_type(t, F32), jnp.logical_not(above))
        need = capf - _count(above)
        tok = lax.broadcasted_iota(jnp.int32, (1, n_tok), 1)
        nbits = n_tok.bit_length()

        def mbody(it, bound):
            cand = bound | jnp.left_shift(jnp.int32(1), nbits - 1 - it)
            f = _count(jnp.logical_and(tied, tok < cand))
            return jnp.where(f <= need, cand, bound)

        bound = lax.fori_loop(0, nbits, mbody, jnp.zeros((n_exp, 1), jnp.int32))
        sel = jnp.logical_or(above, jnp.logical_and(tied, tok < bound))
        self = jnp.where(sel, 1.0, 0.0)
        csum = self
        shift = 1
        while shift < n_tok:
            csum = csum + jnp.where(tok >= shift, pltpu.roll(csum, shift, 1), 0.0)
            shift *= 2
        posm_ref[...] = jnp.where(sel, csum - self, -1.0)

        tok_hi = (tok // ROUTE_TOK_SPLIT).astype(F32)
        tok_lo = (tok % ROUTE_TOK_SPLIT).astype(F32)
        n_hi = cap // ROUTE_SLOT_LO
        hi_iota = lax.broadcasted_iota(jnp.int32, (n_hi, 1), 0)
        lo_iota = lax.broadcasted_iota(jnp.int32, (ROUTE_SLOT_LO, 1), 0)
        pad_rows = (-5 * n_hi) % 16
        zeros = [jnp.zeros((pad_rows, n_tok), F32)] if pad_rows else []

        def ebody(e, carry):
            slot = posm_ref[pl.ds(e, 1), :].astype(jnp.int32)
            in_hi = (slot // ROUTE_SLOT_LO) == hi_iota
            lo_hot = jnp.where((slot % ROUTE_SLOT_LO) == lo_iota, 1.0, 0.0).astype(BF16)
            a = aff_ref[pl.ds(e, 1), :]
            a_hi = a.astype(BF16).astype(F32)
            a_mid = (a - a_hi).astype(BF16).astype(F32)
            a_lo = (a - a_hi) - a_mid
            lhs = jnp.concatenate([jnp.where(in_hi, v, 0.0) for v in (tok_hi, tok_lo, a_hi, a_mid, a_lo)]
                                  + zeros, axis=0).astype(BF16)
            r = _dot_nt(lhs, lo_hot)
            idx_ref[0, e] = (r[0:n_hi] * float(ROUTE_TOK_SPLIT) + r[n_hi:2 * n_hi]).astype(jnp.int32)
            gate_ref[0, e] = r[2 * n_hi:3 * n_hi] + (r[3 * n_hi:4 * n_hi] + r[4 * n_hi:5 * n_hi])
            return carry

        lax.fori_loop(0, n_exp, ebody, 0)


def _route(h2, w_router, cap):
    b, n, d = h2.shape
    e = w_router.shape[1]
    tn = min(n, 1024)
    n_hi = cap // ROUTE_SLOT_LO
    return pl.pallas_call(
        functools.partial(_route_kernel, cap=cap, tn=tn),
        out_shape=(jax.ShapeDtypeStruct((b, e, n_hi, ROUTE_SLOT_LO), jnp.int32),
                   jax.ShapeDtypeStruct((b, e, n_hi, ROUTE_SLOT_LO), F32)),
        grid=(b, n // tn),
        in_specs=[pl.BlockSpec((1, tn, d), lambda bi, i: (bi, i, 0)),
                  pl.BlockSpec((e, d), lambda bi, i: (0, 0))],
        out_specs=(pl.BlockSpec((1, e, n_hi, ROUTE_SLOT_LO), lambda bi, i: (bi, 0, 0, 0)),
                   pl.BlockSpec((1, e, n_hi, ROUTE_SLOT_LO), lambda bi, i: (bi, 0, 0, 0))),
        scratch_shapes=[pltpu.VMEM((e, n), F32), pltpu.VMEM((e, n), F32), pltpu.VMEM((e, n), F32)],
        compiler_params=_cparams(("parallel", "arbitrary")),
        name="moe_route",
    )(h2, w_router.T)


def _moe(h2, w_router, wg, wu, wd, layer):
    b, n, d = h2.shape
    e = w_router.shape[1]
    cap = n * CAPACITY_FACTOR // e
    idx, gates = _route(h2, w_router, cap)
    idx = idx.reshape(b, e, cap)
    xs = _moe_gather(h2, idx)
    y = _moe_ffn(xs.reshape(e, b * cap, d), gates.reshape(b, e, cap).transpose(1, 0, 2).reshape(e, b * cap, 1),
                 wg, wu, wd, layer)
    return _moe_combine(y.reshape(e, b, cap, d), idx, n)


def _inproj_weights(w):
    d = w.shape[0]
    sec = np.cumsum([0, 384, 128, 128, 384, 384, 384, 256, 256, 256])
    aq, ak, av, bq, bk, bv, cq, ck, cv = [w[:, sec[i]:sec[i + 1]] for i in range(9)]
    aq = aq.reshape(d, A_KV_HEADS, A_Q_PER_KV, HEAD_DIM).transpose(0, 2, 1, 3).reshape(d, -1)
    return jnp.concatenate([aq, ak, cq, ck, av, bq * QK_SCALE_64, bk, bv, cv], axis=1).astype(BF16)


def _outproj_weights(w):
    d = w.shape[1]
    wa = A_Q_HEADS * HEAD_DIM
    a = w[:wa].reshape(A_KV_HEADS, A_Q_PER_KV, HEAD_DIM, d).transpose(1, 0, 2, 3).reshape(wa, d)
    return jnp.concatenate([a, w[wa:]], axis=0).astype(BF16)


def _rope_tables(s):
    t = jnp.arange(s)
    rows, cols = t // GRID_W, t % GRID_W

    def head_tables(d):
        q = d // 4
        inv = ROPE_THETA ** (-jnp.arange(q, dtype=F32) / q)
        cs, sn = [], []
        for pos in (rows, cols):
            ang = pos.astype(F32)[:, None] * inv[None, :]
            c, s_ = jnp.cos(ang), jnp.sin(ang)
            cs += [c, c]
            sn += [-s_, s_]
        c = jnp.concatenate(cs, axis=1)
        s_ = jnp.concatenate(sn, axis=1)
        reps = LANES // d
        return jnp.tile(c, (1, reps)), jnp.tile(s_, (1, reps))

    ca, sa = head_tables(HEAD_DIM)
    cc, sc = head_tables(C_QK_DIM)
    cos = jnp.concatenate([ca * QK_SCALE_64, ca, cc * QK_SCALE_32, cc], axis=1)
    sin = jnp.concatenate([sa * QK_SCALE_64, sa, sc * QK_SCALE_32, sc], axis=1)
    return cos, sin


def _ctx_tables(lc):
    ones = jnp.ones((lc, LANES), F32)
    cos = jnp.concatenate([ones * QK_SCALE_64, ones, ones * QK_SCALE_32, ones], axis=1)
    return cos, jnp.zeros_like(cos)


def kernel(x, c, ctx, c_ctx, w_ada, b_ada, g_mix_pre, g_mix_post, g_ffn_pre, g_ffn_post, w_in, w_out,
           a_sink, b_rpb, c_lam_q1, c_lam_k1, c_lam_q2, c_lam_k2, c_subln, w_router, w_gate, w_up, w_down):
    b, s, d = x.shape
    lc = ctx.shape[1]
    depth = w_in.shape[0]

    cos_x, sin_x = _rope_tables(s)
    cos_c, sin_c = _ctx_tables(lc)

    pad = (-(b + 1)) % 8
    cstack = jnp.concatenate([c, c_ctx[None, :], jnp.zeros((pad, d), F32)], axis=0)
    mod = _modulation(cstack, w_ada, b_ada)

    n_exp, ff = w_gate.shape[1], w_gate.shape[3]
    wg = w_gate.reshape(depth * n_exp, d, ff)
    wu = w_up.reshape(depth * n_exp, d, ff)
    wd = w_down.reshape(depth * n_exp, ff, d)

    tm = 512
    h_ctx = ctx
    for l in range(depth):
        need_ctx = l < depth - 1
        mx = mod[l, :b].reshape(b, 1, N_MOD, d)
        sh1, sc1, gt1, sh2, sc2, gt2 = [mx[:, :, k] for k in range(N_MOD)]
        mc = mod[l, b:b + 1].reshape(1, 1, N_MOD, d)
        csh1, csc1, cgt1, csh2, csc2, cgt2 = [mc[:, :, k] for k in range(N_MOD)]

        w_cat = _inproj_weights(w_in[l])
        w_o = _outproj_weights(w_out[l])
        lam_init = 0.8 - 0.6 * math.exp(-0.3 * l)
        lam = (jnp.exp(jnp.sum(c_lam_q1[l] * c_lam_k1[l])) - jnp.exp(jnp.sum(c_lam_q2[l] * c_lam_k2[l]))
               + lam_init).reshape(1).astype(F32)
        subln2 = jnp.tile(c_subln[l], 2).reshape(1, LANES)
        bias = _b_bias_tables(b_rpb[l], s // GRID_W, LOG2E)
        sink = a_sink[l] * LOG2E

        p = _inproj(x, g_mix_pre[l], sc1, sh1, cos_x, sin_x, w_cat, tm)
        pc = _inproj(h_ctx, g_mix_pre[l], csc1, csh1, cos_c, sin_c, w_cat, lc)

        oa = _attn_a(p, pc, sink, True)
        ob = _attn_b(p, pc, bias)
        od = _attn_c(p, pc, lam, subln2, lam_init, True)
        x1, h2 = _outproj(oa, ob, od, x, w_o, g_mix_post[l], gt1, g_ffn_pre[l], sc2, sh2, tm)
        if need_ctx:
            oac = _attn_a(p, pc, sink, False)
            obc = _attn_b_ctx(pc)
            odc = _attn_c(p, pc, lam, subln2, lam_init, False)
            c1, hc2 = _outproj(oac, obc, odc, h_ctx, w_o, g_mix_post[l], cgt1, g_ffn_pre[l], csc2, csh2, lc)
            yc = _moe(hc2, w_router[l], wg, wu, wd, l)
            h_ctx = _resid(c1, yc, g_ffn_post[l], cgt2, lc)
        y = _moe(h2, w_router[l], wg, wu, wd, l)
        x = _resid(x1, y, g_ffn_post[l], gt2, tm)
    return x
```

```python
import functools
import math

import numpy as np
import jax
import jax.numpy as jnp
from jax import lax
from jax.experimental import pallas as pl
from jax.experimental.pallas import tpu as pltpu

F32 = jnp.float32
BF16 = jnp.bfloat16

GRID_W = 64
HEAD_DIM = 64
A_Q_HEADS, A_KV_HEADS = 6, 2
A_Q_PER_KV = A_Q_HEADS // A_KV_HEADS
A_WINDOW = 128
A_BLOCK = 128
B_HEADS, B_MAX_ROWS, B_COLS = 6, 8, 16
C_HEADS, C_QK_DIM, C_V_DIM = 4, 32, 64
N_EXPERTS = 16
CAPACITY_FACTOR = 2
N_MOD = 6
ROPE_THETA = 10000.0
EPS = 1e-6
NEG_INF = -1e30
LOG2E = math.log2(math.e)
QK_SCALE_64 = HEAD_DIM ** -0.5 * LOG2E
QK_SCALE_32 = C_QK_DIM ** -0.5 * LOG2E

LANES = 128
ROPE_W = 1024
OFF_AQ, OFF_AK, OFF_CQ, OFF_CK, OFF_AV, OFF_BQ, OFF_BK, OFF_BV, OFF_CV = (
    0, 384, 512, 768, 1024, 1152, 1536, 1920, 2304)
IN_W = 2560
VMEM_LIMIT = 56 * 1024 * 1024


def _cparams(sem, vmem_limit=VMEM_LIMIT):
    return pltpu.CompilerParams(dimension_semantics=sem, vmem_limit_bytes=vmem_limit)


def _dot(a, b):
    return jnp.dot(a, b, preferred_element_type=F32)


def _dot_nt(a, b):
    return lax.dot_general(a, b, (((1,), (1,)), ((), ())), preferred_element_type=F32)


def _rms(x, g):
    return x * lax.rsqrt(jnp.mean(x * x, axis=-1, keepdims=True) + EPS) * g


def _mod_kernel(c_ref, w_ref, b_ref, o_ref):
    c = c_ref[...]
    a = c / (1.0 + jnp.exp(-c))
    o_ref[0] = jnp.dot(a, w_ref[0], preferred_element_type=F32,
                       precision=lax.Precision.HIGHEST) + b_ref[0]


def _modulation(cstack, w_ada, b_ada):
    nl, d, n = w_ada.shape
    rows = cstack.shape[0]
    tn = 1536
    return pl.pallas_call(
        _mod_kernel,
        out_shape=jax.ShapeDtypeStruct((nl, rows, n), F32),
        grid=(nl, n // tn),
        in_specs=[
            pl.BlockSpec((rows, d), lambda l, j: (0, 0)),
            pl.BlockSpec((1, d, tn), lambda l, j: (l, 0, j)),
            pl.BlockSpec((1, 1, tn), lambda l, j: (l, 0, j)),
        ],
        out_specs=pl.BlockSpec((1, rows, tn), lambda l, j: (l, 0, j)),
        compiler_params=_cparams(("parallel", "parallel")),
        name="adaln_modulation",
    )(cstack, w_ada, b_ada.reshape(nl, 1, n))


_ROPE_TABLE_OF_UNIT = (0, 0, 0, 1, 2, 2, 3, 3)


_ROPE_HALF_OF_CHUNK = (HEAD_DIM // 4, HEAD_DIM // 4, C_QK_DIM // 4, C_QK_DIM // 4)


def _inproj_kernel(x_ref, g_ref, sc_ref, sh_ref, cos_ref, sin_ref, w_ref, o_ref):
    x = x_ref[0]
    h = _rms(x, g_ref[...]) * (1.0 + sc_ref[0]) + sh_ref[0]
    hb = h.astype(BF16)
    for j in range(ROPE_W // 256):
        t0, t1 = _ROPE_TABLE_OF_UNIT[2 * j], _ROPE_TABLE_OF_UNIT[2 * j + 1]
        cos = jnp.concatenate([cos_ref[:, t0 * LANES:(t0 + 1) * LANES],
                               cos_ref[:, t1 * LANES:(t1 + 1) * LANES]], axis=1)
        sin = jnp.concatenate([sin_ref[:, t0 * LANES:(t0 + 1) * LANES],
                               sin_ref[:, t1 * LANES:(t1 + 1) * LANES]], axis=1)
        p = _dot(hb, w_ref[:, 256 * j:256 * (j + 1)])
        half = _ROPE_HALF_OF_CHUNK[j]
        lane = lax.broadcasted_iota(jnp.int32, (1, 256), 1)
        ps = jnp.where((lane & (2 * half - 1)) < half, pltpu.roll(p, 256 - half, 1), pltpu.roll(p, half, 1))
        o_ref[0, :, 256 * j:256 * (j + 1)] = (p * cos + ps * sin).astype(BF16)
    for j in range(ROPE_W // 256, IN_W // 256):
        o_ref[0, :, 256 * j:256 * (j + 1)] = _dot(hb, w_ref[:, 256 * j:256 * (j + 1)]).astype(BF16)


def _inproj(x, g, sc, sh, cos, sin, w_cat, tm):
    b, s, d = x.shape
    per_sample = sc.shape[0] > 1
    mod_map = (lambda bi, i: (bi, 0, 0)) if per_sample else (lambda bi, i: (0, 0, 0))
    return pl.pallas_call(
        _inproj_kernel,
        out_shape=jax.ShapeDtypeStruct((b, s, IN_W), BF16),
        grid=(b, s // tm),
        in_specs=[
            pl.BlockSpec((1, tm, d), lambda bi, i: (bi, i, 0)),
            pl.BlockSpec((1, d), lambda bi, i: (0, 0)),
            pl.BlockSpec((1, 1, d), mod_map),
            pl.BlockSpec((1, 1, d), mod_map),
            pl.BlockSpec((tm, 4 * LANES), lambda bi, i: (i, 0)),
            pl.BlockSpec((tm, 4 * LANES), lambda bi, i: (i, 0)),
            pl.BlockSpec((d, IN_W), lambda bi, i: (0, 0)),
        ],
        out_specs=pl.BlockSpec((1, tm, IN_W), lambda bi, i: (bi, i, 0)),
        compiler_params=_cparams(("parallel", "parallel")),
        name="norm_inproj_rope",
    )(x, g.reshape(1, d), sc, sh, cos, sin, w_cat)


def _lo_mask():
    return lax.broadcasted_iota(jnp.int32, (1, LANES), 1) < HEAD_DIM


def _split_halves(q):
    lo = _lo_mask()
    zero = jnp.zeros_like(q)
    return jnp.where(lo, q, zero), jnp.where(lo, zero, q)


A_STEP_BLOCKS = 4


def _attn_a_kernel(sink_ref, q_ref, *refs, seq, latent):
    if latent:
        k_ref, v_ref, kc_ref, vc_ref, o_ref = refs
    else:
        kc_ref, vc_ref, o_ref = refs
    tq = q_ref.shape[1]
    npair = A_Q_HEADS // 2
    kc = kc_ref[0]
    vc = vc_ref[0]
    lo = _lo_mask()
    if latent:
        i = pl.program_id(1)
        band = tq + 2 * A_BLOCK
        start = pl.multiple_of(jnp.clip(i * tq - A_BLOCK, 0, seq - band), A_BLOCK)
        kb = k_ref[0, pl.ds(start, band), :]
        vb = v_ref[0, pl.ds(start, band), :]
        delta = ((start - i * tq) + lax.broadcasted_iota(jnp.int32, (1, band), 1)
                 - lax.broadcasted_iota(jnp.int32, (tq, 1), 0))
        wmask = jnp.where(jnp.abs(delta) <= A_WINDOW, 0.0, NEG_INF)
        wmask = jnp.concatenate([wmask, wmask], axis=0)
    outs = []
    for t in range(npair):
        qa, qb = _split_halves(q_ref[0, :, t * LANES:(t + 1) * LANES])
        q2 = jnp.concatenate([qa, qb], axis=0)
        sink = jnp.concatenate([jnp.full((tq, 1), sink_ref[t], F32),
                                jnp.full((tq, 1), sink_ref[npair + t], F32)], axis=0)
        s_ctx = _dot_nt(q2, kc)
        m = jnp.maximum(jnp.max(s_ctx, axis=-1, keepdims=True), sink)
        if latent:
            s_lat = _dot_nt(q2, kb) + wmask
            m = jnp.maximum(m, jnp.max(s_lat, axis=-1, keepdims=True))
            e_lat = jnp.exp2(s_lat - m)
        e_ctx = jnp.exp2(s_ctx - m)
        l = jnp.sum(e_ctx, axis=-1, keepdims=True) + jnp.exp2(sink - m)
        o = _dot(e_ctx.astype(BF16), vc)
        if latent:
            l = l + jnp.sum(e_lat, axis=-1, keepdims=True)
            o = o + _dot(e_lat.astype(BF16), vb)
        o = o / l
        outs.append(jnp.where(lo, o[:tq], o[tq:]))
    o_ref[0] = jnp.concatenate(outs, axis=1).astype(BF16)


def _attn_a(p, pc, sink, latent):
    src = p if latent else pc
    b, s, _ = src.shape
    lc = pc.shape[1]
    tq = A_STEP_BLOCKS * A_BLOCK if latent else s
    wq = A_Q_HEADS * HEAD_DIM
    in_specs = [pl.BlockSpec(memory_space=pltpu.SMEM),
                pl.BlockSpec((1, tq, wq), lambda bi, i: (bi, i, OFF_AQ // wq))]
    args = [sink, src]
    if latent:
        in_specs += [pl.BlockSpec((1, s, LANES), lambda bi, i: (bi, 0, OFF_AK // LANES)),
                     pl.BlockSpec((1, s, LANES), lambda bi, i: (bi, 0, OFF_AV // LANES))]
        args += [p, p]
    in_specs += [pl.BlockSpec((1, lc, LANES), lambda bi, i: (bi, 0, OFF_AK // LANES)),
                 pl.BlockSpec((1, lc, LANES), lambda bi, i: (bi, 0, OFF_AV // LANES))]
    args += [pc, pc]
    return pl.pallas_call(
        functools.partial(_attn_a_kernel, seq=s, latent=latent),
        out_shape=jax.ShapeDtypeStruct((b, s, wq), BF16),
        grid=(b, s // tq),
        in_specs=in_specs,
        out_specs=pl.BlockSpec((1, tq, wq), lambda bi, i: (bi, i, 0)),
        compiler_params=_cparams(("parallel", "arbitrary")),
        name="attn_window_gqa" + ("" if latent else "_ctx"),
    )(*args)


B_GROUP_ROWS = 4
B_UNION_ROWS = B_GROUP_ROWS + B_MAX_ROWS


def _attn_b_softmax_out(s_parts, v_parts):
    m = functools.reduce(jnp.maximum, [jnp.max(s, axis=-1, keepdims=True) for s in s_parts])
    es = [jnp.exp2(s - m) for s in s_parts]
    l = functools.reduce(jnp.add, [jnp.sum(e, axis=-1, keepdims=True) for e in es])
    o = functools.reduce(jnp.add, [_dot(e.astype(BF16), v) for e, v in zip(es, v_parts)])
    return o / l


def _attn_b_kernel(q_ref, k_ref, v_ref, kc_ref, vc_ref, bias_ref, o_ref, *, rows_n):
    i = pl.program_id(1)
    npair = B_HEADS // 2
    lo = _lo_mask()
    tq = q_ref.shape[1]
    nk = B_UNION_ROWS * GRID_W
    us = jnp.clip(i * B_GROUP_ROWS - B_MAX_ROWS // 2, 0, rows_n - B_UNION_ROWS)
    kstart = pl.multiple_of(us * GRID_W, GRID_W)
    outs = []
    for t in range(npair):
        cols = slice(t * LANES, (t + 1) * LANES)
        qa, qb = _split_halves(q_ref[0, :, cols])
        q2 = jnp.concatenate([qa, qb], axis=0)
        s_lat = _dot_nt(q2, k_ref[0, pl.ds(kstart, nk), cols]) + bias_ref[0, t]
        s_ctx = _dot_nt(q2, kc_ref[0, :, cols])
        o = _attn_b_softmax_out([s_lat, s_ctx], [v_ref[0, pl.ds(kstart, nk), cols], vc_ref[0, :, cols]])
        outs.append(jnp.where(lo, o[:tq], o[tq:]))
    o_ref[0] = jnp.concatenate(outs, axis=1).astype(BF16)


def _attn_b_ctx_kernel(q_ref, kc_ref, vc_ref, o_ref):
    npair = B_HEADS // 2
    lo = _lo_mask()
    n = q_ref.shape[1]
    for t in range(npair):
        cols = slice(t * LANES, (t + 1) * LANES)
        qa, qb = _split_halves(q_ref[0, :, cols])
        q2 = jnp.concatenate([qa, qb], axis=0)
        s_ctx = _dot_nt(q2, kc_ref[0, :, cols])
        o = _attn_b_softmax_out([s_ctx], [vc_ref[0, :, cols]])
        o_ref[0, :, cols] = jnp.where(lo, o[:n], o[n:]).astype(BF16)


def _attn_b(p, pc, bias):
    b, s, _ = p.shape
    lc = pc.shape[1]
    w = B_HEADS * HEAD_DIM
    rows_n = s // GRID_W
    tq = B_GROUP_ROWS * GRID_W
    ng = s // tq
    variant = lambda bi, i: (jnp.minimum(i, 1) + (i == ng - 1).astype(jnp.int32), 0, 0, 0)
    return pl.pallas_call(
        functools.partial(_attn_b_kernel, rows_n=rows_n),
        out_shape=jax.ShapeDtypeStruct((b, s, w), BF16),
        grid=(b, ng),
        in_specs=[
            pl.BlockSpec((1, tq, w), lambda bi, i: (bi, i, OFF_BQ // w)),
            pl.BlockSpec((1, s, w), lambda bi, i: (bi, 0, OFF_BK // w)),
            pl.BlockSpec((1, s, w), lambda bi, i: (bi, 0, OFF_BV // w)),
            pl.BlockSpec((1, lc, w), lambda bi, i: (bi, 0, OFF_BK // w)),
            pl.BlockSpec((1, lc, w), lambda bi, i: (bi, 0, OFF_BV // w)),
            pl.BlockSpec((1,) + bias.shape[1:], variant),
        ],
        out_specs=pl.BlockSpec((1, tq, w), lambda bi, i: (bi, i, 0)),
        compiler_params=_cparams(("parallel", "arbitrary")),
        name="attn_neighbourhood",
    )(p, p, p, pc, pc, bias)


def _attn_b_ctx(pc):
    b, lc, _ = pc.shape
    w = B_HEADS * HEAD_DIM
    return pl.pallas_call(
        _attn_b_ctx_kernel,
        out_shape=jax.ShapeDtypeStruct((b, lc, w), BF16),
        grid=(b,),
        in_specs=[
            pl.BlockSpec((1, lc, w), lambda bi: (bi, 0, OFF_BQ // w)),
            pl.BlockSpec((1, lc, w), lambda bi: (bi, 0, OFF_BK // w)),
            pl.BlockSpec((1, lc, w), lambda bi: (bi, 0, OFF_BV // w)),
        ],
        out_specs=pl.BlockSpec((1, lc, w), lambda bi: (bi, 0, 0)),
        compiler_params=_cparams(("parallel",)),
        name="attn_neighbourhood_ctx",
    )(pc, pc, pc)


def _b_bias_tables(rpb, rows_n, scale):
    g_rows, u_rows = B_GROUP_ROWS, B_UNION_ROWS
    col = np.arange(GRID_W)
    cstart = np.clip(col - B_COLS // 2, 0, GRID_W - B_COLS)
    col_ok = (col[None, :] >= cstart[:, None]) & (col[None, :] < cstart[:, None] + B_COLS)
    dc_idx = np.clip(col[None, :] - col[:, None], -(B_COLS - 1), B_COLS - 1) + (B_COLS - 1)
    pick_col = (dc_idx[None] == np.arange(2 * B_COLS - 1)[:, None, None]).astype(np.float32)
    n_groups = rows_n // g_rows
    patterns, group_variant = [], []
    for g in range(n_groups):
        us = np.clip(g * g_rows - B_MAX_ROWS // 2, 0, rows_n - u_rows)
        r = g * g_rows + np.arange(g_rows)[:, None]
        rs = np.clip(r - B_MAX_ROWS // 2, 0, rows_n - B_MAX_ROWS)
        key_row = us + np.arange(u_rows)[None, :]
        dr = np.where((key_row >= rs) & (key_row < rs + B_MAX_ROWS), key_row - r + (B_MAX_ROWS - 1), -1)
        if not any(np.array_equal(dr, p_) for p_ in patterns):
            patterns.append(dr)
        group_variant.append([np.array_equal(dr, p_) for p_ in patterns].index(True))
    assert group_variant == [0] + [1] * (n_groups - 2) + [2], group_variant
    dr = np.stack(patterns)
    pick_row = (dr[..., None] == np.arange(2 * B_MAX_ROWS - 1)).astype(np.float32)
    hi = lax.Precision.HIGHEST
    rsel = jnp.einsum('vuwd,hdc->vhuwc', pick_row, rpb.astype(F32), precision=hi)
    t = jnp.einsum('vhuwc,cqk->vhuqwk', rsel, pick_col, precision=hi) * scale
    ok = (dr >= 0)[:, None, :, None, :, None] & col_ok[None, None, None, :, None, :]
    t = jnp.where(ok, t, NEG_INF)
    return t.reshape(len(patterns), B_HEADS // 2, 2 * g_rows * GRID_W, u_rows * GRID_W)


C_KEY_CHUNK = 2048
C_Q_TILE = 256


def _attn_c_kernel(lam_ref, q_ref, *refs, latent, out_scale):
    if latent:
        k_ref, v_ref, kc_ref, vc_ref, g_ref, o_ref = refs
    else:
        kc_ref, vc_ref, g_ref, o_ref = refs
    tq = q_ref.shape[1]
    lam = lam_ref[0]
    q = q_ref[0]
    quarter = lax.broadcasted_iota(jnp.int32, (1, LANES), 1) // C_QK_DIM
    zero = jnp.zeros_like(q)
    q4 = jnp.concatenate([jnp.where(quarter == j, q, zero) for j in range(4)], axis=0)
    chunks = []
    if latent:
        kc_n = min(C_KEY_CHUNK, k_ref.shape[1])
        chunks += [(k_ref, v_ref, c * kc_n, kc_n) for c in range(k_ref.shape[1] // kc_n)]
    chunks.append((kc_ref, vc_ref, 0, kc_ref.shape[1]))
    m = l = acc = None
    for kr, vr, st, n in chunks:
        s = _dot_nt(q4, kr[0, st:st + n, :])
        mc = jnp.max(s, axis=-1, keepdims=True)
        if m is None:
            m = mc
            e = jnp.exp2(s - m)
            l = jnp.sum(e, axis=-1, keepdims=True)
            acc = _dot(e.astype(BF16), vr[0, st:st + n, :])
        else:
            m_new = jnp.maximum(m, mc)
            alpha = jnp.exp2(m - m_new)
            e = jnp.exp2(s - m_new)
            l = l * alpha + jnp.sum(e, axis=-1, keepdims=True)
            acc = acc * alpha + _dot(e.astype(BF16), vr[0, st:st + n, :])
            m = m_new
    o4 = acc / l
    outs = [o4[2 * h * tq:(2 * h + 1) * tq] - lam * o4[(2 * h + 1) * tq:(2 * h + 2) * tq]
            for h in range(2)]
    lo = _lo_mask()
    o = jnp.where(lo, outs[0], outs[1])
    sq = o * o
    s_lo = jnp.sum(jnp.where(lo, sq, 0.0), axis=-1, keepdims=True)
    s_hi = jnp.sum(jnp.where(lo, 0.0, sq), axis=-1, keepdims=True)
    ms = jnp.where(lo, s_lo, s_hi) * (1.0 / C_V_DIM)
    o_ref[0] = (o * lax.rsqrt(ms + EPS) * g_ref[...] * out_scale).astype(BF16)


def _attn_c(p, pc, lam, subln2, lam_init, latent):
    src = p if latent else pc
    b, s, _ = src.shape
    lc = pc.shape[1]
    tq = C_Q_TILE if latent else s
    npair = C_HEADS // 2
    in_specs = [pl.BlockSpec(memory_space=pltpu.SMEM),
                pl.BlockSpec((1, tq, LANES), lambda bi, hp, i: (bi, i, OFF_CQ // LANES + hp))]
    args = [lam, src]
    if latent:
        in_specs += [pl.BlockSpec((1, s, LANES), lambda bi, hp, i: (bi, 0, OFF_CK // LANES + hp)),
                     pl.BlockSpec((1, s, LANES), lambda bi, hp, i: (bi, 0, OFF_CV // LANES + hp))]
        args += [p, p]
    in_specs += [pl.BlockSpec((1, lc, LANES), lambda bi, hp, i: (bi, 0, OFF_CK // LANES + hp)),
                 pl.BlockSpec((1, lc, LANES), lambda bi, hp, i: (bi, 0, OFF_CV // LANES + hp)),
                 pl.BlockSpec((1, LANES), lambda bi, hp, i: (0, 0))]
    args += [pc, pc, subln2]
    return pl.pallas_call(
        functools.partial(_attn_c_kernel, latent=latent, out_scale=1.0 - lam_init),
        out_shape=jax.ShapeDtypeStruct((b, s, C_HEADS * C_V_DIM), BF16),
        grid=(b, npair, s // tq),
        in_specs=in_specs,
        out_specs=pl.BlockSpec((1, tq, LANES), lambda bi, hp, i: (bi, i, hp)),
        compiler_params=_cparams(("parallel", "parallel", "arbitrary")),
        name="attn_differential" + ("" if latent else "_ctx"),
    )(*args)


def _outproj_kernel(oa_ref, ob_ref, oc_ref, x_ref, w_ref, gpost_ref, gt_ref, gpre_ref, sc_ref, sh_ref,
                    x1_ref, h2_ref):
    wa = oa_ref.shape[2]
    wb = ob_ref.shape[2]
    y = (_dot(oa_ref[0], w_ref[0:wa, :]) + _dot(ob_ref[0], w_ref[wa:wa + wb, :])
         + _dot(oc_ref[0], w_ref[wa + wb:, :]))
    x1 = x_ref[0] + gt_ref[0] * _rms(y, gpost_ref[...])
    x1_ref[0] = x1
    h2_ref[0] = _rms(x1, gpre_ref[...]) * (1.0 + sc_ref[0]) + sh_ref[0]


def _outproj(oa, ob, oc, x, w_out, g_post, gt, g_pre, sc, sh, tm):
    b, s, d = x.shape
    per_sample = gt.shape[0] > 1
    mod_map = (lambda bi, i: (bi, 0, 0)) if per_sample else (lambda bi, i: (0, 0, 0))
    row = lambda bi, i: (bi, i, 0)
    const2 = lambda bi, i: (0, 0)
    return pl.pallas_call(
        _outproj_kernel,
        out_shape=(jax.ShapeDtypeStruct((b, s, d), F32), jax.ShapeDtypeStruct((b, s, d), F32)),
        grid=(b, s // tm),
        in_specs=[
            pl.BlockSpec((1, tm, oa.shape[2]), row),
            pl.BlockSpec((1, tm, ob.shape[2]), row),
            pl.BlockSpec((1, tm, oc.shape[2]), row),
            pl.BlockSpec((1, tm, d), row),
            pl.BlockSpec(w_out.shape, const2),
            pl.BlockSpec((1, d), const2),
            pl.BlockSpec((1, 1, d), mod_map),
            pl.BlockSpec((1, d), const2),
            pl.BlockSpec((1, 1, d), mod_map),
            pl.BlockSpec((1, 1, d), mod_map),
        ],
        out_specs=(pl.BlockSpec((1, tm, d), row), pl.BlockSpec((1, tm, d), row)),
        compiler_params=_cparams(("parallel", "parallel")),
        name="outproj_residual_norm",
    )(oa, ob, oc, x, w_out, g_post.reshape(1, d), gt, g_pre.reshape(1, d), sc, sh)


ROW_UNROLL = 8


GATHER_ROWS = 16


def _gather_kernel(idx_ref, h_ref, o_ref):
    cap, d = o_ref.shape[2], o_ref.shape[3]
    sub = lax.broadcasted_iota(jnp.int32, (ROW_UNROLL, d), 0)

    def body(j, carry):
        base = pl.multiple_of(j * GATHER_ROWS, GATHER_ROWS)
        halves = []
        for g in range(GATHER_ROWS // ROW_UNROLL):
            tile = jnp.zeros((ROW_UNROLL, d), F32)
            for u in range(ROW_UNROLL):
                n = idx_ref[0, 0, base + g * ROW_UNROLL + u]
                row = jnp.broadcast_to(h_ref[0, pl.ds(n, 1), :], (ROW_UNROLL, d))
                tile = jnp.where(sub == u, row, tile)
            halves.append(tile)
        o_ref[0, 0, pl.ds(base, GATHER_ROWS), :] = jnp.concatenate(halves, axis=0).astype(BF16)
        return carry

    lax.fori_loop(0, cap // GATHER_ROWS, body, 0)


def _moe_gather(h2, idx):
    b, n, d = h2.shape
    e, cap = idx.shape[1], idx.shape[2]
    return pl.pallas_call(
        _gather_kernel,
        out_shape=jax.ShapeDtypeStruct((e, b, cap, d), BF16),
        grid=(b, e),
        in_specs=[
            pl.BlockSpec((1, 1, cap), lambda bi, ei: (bi * e + ei, 0, 0), memory_space=pltpu.SMEM),
            pl.BlockSpec((1, n, d), lambda bi, ei: (bi, 0, 0)),
        ],
        out_specs=pl.BlockSpec((1, 1, cap, d), lambda bi, ei: (ei, bi, 0, 0)),
        compiler_params=_cparams(("parallel", "arbitrary")),
        name="moe_gather",
    )(idx.reshape(b * e, 1, cap), h2)


FF_CHUNK = 512
CAST_ROWS = 256


def _ffn_kernel(*refs, tiles, first_expert):
    ns = len(tiles)
    xs_refs, gate_refs = refs[0:2 * ns:2], refs[1:2 * ns:2]
    wg_hbm, wu_hbm, wd_hbm = refs[2 * ns:2 * ns + 3]
    o_refs = refs[2 * ns + 3:3 * ns + 3]
    stage_g, stage_u, stage_d, wg_ref, wu_ref, wd_ref, sems = refs[3 * ns + 3:]
    e = pl.program_id(0)
    r = pl.program_id(1)

    def weight_copies(expert):
        row = first_expert + expert
        return [pltpu.make_async_copy(wg_hbm.at[row], stage_g, sems.at[0]),
                pltpu.make_async_copy(wu_hbm.at[row], stage_u, sems.at[1]),
                pltpu.make_async_copy(wd_hbm.at[row], stage_d, sems.at[2])]

    @pl.when(r == 0)
    def _():
        @pl.when(e == 0)
        def _():
            for c in weight_copies(0):
                c.start()

        for c in weight_copies(e):
            c.wait()
        for src, dst in ((stage_g, wg_ref), (stage_u, wu_ref), (stage_d, wd_ref)):
            for r0 in range(0, src.shape[0], CAST_ROWS):
                dst[r0:r0 + CAST_ROWS, :] = src[r0:r0 + CAST_ROWS, :].astype(BF16)

        @pl.when(e + 1 < pl.num_programs(0))
        def _():
            for c in weight_copies(e + 1):
                c.start()

    ff = wg_ref.shape[1]

    def swiglu(xs_ref, gate_ref, o_ref):
        xs = xs_ref[0]
        for c in range(ff // FF_CHUNK):
            cs = slice(c * FF_CHUNK, (c + 1) * FF_CHUNK)
            g = _dot(xs, wg_ref[:, cs])
            u = _dot(xs, wu_ref[:, cs])
            hid = ((g / (1.0 + jnp.exp(-g))) * u).astype(BF16)
            part = _dot(hid, wd_ref[cs, :])
            if c == 0:
                o_ref[0] = part
            elif c < ff // FF_CHUNK - 1:
                o_ref[0] += part
            else:
                o_ref[0] = (o_ref[0] + part) * gate_ref[0]

    first = 0
    for k, nt in enumerate(tiles):
        pl.when(jnp.logical_and(r >= first, r < first + nt))(
            functools.partial(swiglu, xs_refs[k], gate_refs[k], o_refs[k]))
        first += nt


FFN_ROWS = 512
FFN_VMEM_LIMIT = 61 * 1024 * 1024


def _moe_ffn(xs_sets, gate_sets, wg, wu, wd, layer):
    e, _, d = xs_sets[0].shape
    ff = wg.shape[2]
    in_specs, out_specs, out_shape, tiles, args = [], [], [], [], []
    first = 0
    for xs, gates in zip(xs_sets, gate_sets):
        r = xs.shape[1]
        tr = min(r, FFN_ROWS)
        nt = r // tr
        row_map = functools.partial(lambda ei, ri, first, nt: (ei, jnp.clip(ri - first, 0, nt - 1), 0),
                                    first=first, nt=nt)
        in_specs += [pl.BlockSpec((1, tr, d), row_map), pl.BlockSpec((1, tr, 1), row_map)]
        out_specs.append(pl.BlockSpec((1, tr, d), row_map))
        out_shape.append(jax.ShapeDtypeStruct((e, r, d), F32))
        args += [xs, gates]
        tiles.append(nt)
        first += nt
    in_specs += [pl.BlockSpec(memory_space=pl.ANY)] * 3
    return pl.pallas_call(
        functools.partial(_ffn_kernel, tiles=tuple(tiles), first_expert=layer * e),
        out_shape=out_shape,
        grid=(e, first),
        in_specs=in_specs,
        out_specs=out_specs,
        scratch_shapes=[pltpu.VMEM((d, ff), F32), pltpu.VMEM((d, ff), F32), pltpu.VMEM((ff, d), F32),
                        pltpu.VMEM((d, ff), BF16), pltpu.VMEM((d, ff), BF16), pltpu.VMEM((ff, d), BF16),
                        pltpu.SemaphoreType.DMA((3,))],
        compiler_params=_cparams(("arbitrary", "arbitrary"), FFN_VMEM_LIMIT),
        name="moe_expert_ffn",
    )(*args, wg, wu, wd)


def _combine_kernel(idx_ref, y_ref, o_ref):
    cap = y_ref.shape[2]

    @pl.when(pl.program_id(1) == 0)
    def _():
        o_ref[...] = jnp.zeros_like(o_ref)

    def body(j, carry):
        base = pl.multiple_of(j * ROW_UNROLL, ROW_UNROLL)
        toks = [idx_ref[0, 0, base + u] for u in range(ROW_UNROLL)]
        ys = y_ref[0, 0, pl.ds(base, ROW_UNROLL), :]
        rows = [o_ref[0, pl.ds(n, 1), :] for n in toks]
        for u, n in enumerate(toks):
            o_ref[0, pl.ds(n, 1), :] = rows[u] + ys[u:u + 1]
        return carry

    lax.fori_loop(0, cap // ROW_UNROLL, body, 0)


def _moe_combine(y, idx, n):
    e, b, cap, d = y.shape
    return pl.pallas_call(
        _combine_kernel,
        out_shape=jax.ShapeDtypeStruct((b, n, d), F32),
        grid=(b, e),
        in_specs=[
            pl.BlockSpec((1, 1, cap), lambda bi, ei: (bi * e + ei, 0, 0), memory_space=pltpu.SMEM),
            pl.BlockSpec((1, 1, cap, d), lambda bi, ei: (ei, bi, 0, 0)),
        ],
        out_specs=pl.BlockSpec((1, n, d), lambda bi, ei: (bi, 0, 0)),
        compiler_params=_cparams(("parallel", "arbitrary")),
        name="moe_combine",
    )(idx.reshape(b * e, 1, cap), y)


def _resid_kernel(x_ref, y_ref, g_ref, gt_ref, o_ref):
    o_ref[0] = x_ref[0] + gt_ref[0] * _rms(y_ref[0], g_ref[...])


def _resid(x, y, g, gt, tm):
    b, s, d = x.shape
    per_sample = gt.shape[0] > 1
    mod_map = (lambda bi, i: (bi, 0, 0)) if per_sample else (lambda bi, i: (0, 0, 0))
    row = lambda bi, i: (bi, i, 0)
    return pl.pallas_call(
        _resid_kernel,
        out_shape=jax.ShapeDtypeStruct((b, s, d), F32),
        grid=(b, s // tm),
        in_specs=[pl.BlockSpec((1, tm, d), row), pl.BlockSpec((1, tm, d), row),
                  pl.BlockSpec((1, d), lambda bi, i: (0, 0)), pl.BlockSpec((1, 1, d), mod_map)],
        out_specs=pl.BlockSpec((1, tm, d), row),
        compiler_params=_cparams(("parallel", "parallel")),
        name="ffn_residual",
    )(x, y, g.reshape(1, d), gt)


ROUTE_SLOT_LO = 32
ROUTE_TOK_SPLIT = 64


def _count(mask):
    return jnp.sum(jnp.where(mask, 1.0, 0.0), axis=1, keepdims=True)


def _route_kernel(h_ref, wr_ref, idx_ref, gate_ref, logit_ref, aff_ref, posm_ref, *, cap, tn):
    i = pl.program_id(1)
    n_exp, n_tok = logit_ref.shape
    h = h_ref[0]
    h_hi = h.astype(BF16)
    h_lo = (h - h_hi.astype(F32)).astype(BF16)
    w = wr_ref[...]
    w_hi = w.astype(BF16)
    w_lo = (w - w_hi.astype(F32)).astype(BF16)
    logit_ref[:, pl.ds(pl.multiple_of(i * tn, tn), tn)] = (
        _dot_nt(w_hi, h_hi) + (_dot_nt(w_hi, h_lo) + _dot_nt(w_lo, h_hi)))

    @pl.when(i == pl.num_programs(1) - 1)
    def _():
        lg = logit_ref[...]
        ex = jnp.exp(lg - jnp.max(lg, axis=0, keepdims=True))
        aff = ex / jnp.sum(ex, axis=0, keepdims=True)
        aff_ref[...] = aff
        capf = float(cap)

        def tbody(it, t):
            cand = t | jnp.left_shift(jnp.int32(1), 30 - it)
            cnt = _count(aff >= lax.bitcast_convert_type(cand, F32))
            return jnp.where(cnt >= capf, cand, t)

        t = lax.fori_loop(0, 31, tbody, jnp.zeros((n_exp, 1), jnp.int32))
        above = aff >= lax.bitcast_convert_type(t + 1, F32)
        tied = jnp.logical_and(aff >= lax.bitcast_convert_type(t, F32), jnp.logical_not(above))
        need = capf - _count(above)
        tok = lax.broadcasted_iota(jnp.int32, (1, n_tok), 1)
        nbits = n_tok.bit_length()

        def mbody(it, bound):
            cand = bound | jnp.left_shift(jnp.int32(1), nbits - 1 - it)
            f = _count(jnp.logical_and(tied, tok < cand))
            return jnp.where(f <= need, cand, bound)

        bound = lax.fori_loop(0, nbits, mbody, jnp.zeros((n_exp, 1), jnp.int32))
        sel = jnp.logical_or(above, jnp.logical_and(tied, tok < bound))
        self = jnp.where(sel, 1.0, 0.0)
        csum = self
        shift = 1
        while shift < n_tok:
            csum = csum + jnp.where(tok >= shift, pltpu.roll(csum, shift, 1), 0.0)
            shift *= 2
        posm_ref[...] = jnp.where(sel, csum - self, -1.0)

        tok_hi = (tok // ROUTE_TOK_SPLIT).astype(F32)
        tok_lo = (tok % ROUTE_TOK_SPLIT).astype(F32)
        n_hi = cap // ROUTE_SLOT_LO
        hi_iota = lax.broadcasted_iota(jnp.int32, (n_hi, 1), 0)
        lo_iota = lax.broadcasted_iota(jnp.int32, (ROUTE_SLOT_LO, 1), 0)
        pad_rows = (-5 * n_hi) % 16
        zeros = [jnp.zeros((pad_rows, n_tok), F32)] if pad_rows else []

        def ebody(e, carry):
            slot = posm_ref[pl.ds(e, 1), :].astype(jnp.int32)
            in_hi = (slot // ROUTE_SLOT_LO) == hi_iota
            lo_hot = jnp.where((slot % ROUTE_SLOT_LO) == lo_iota, 1.0, 0.0).astype(BF16)
            a = aff_ref[pl.ds(e, 1), :]
            a_hi = a.astype(BF16).astype(F32)
            a_mid = (a - a_hi).astype(BF16).astype(F32)
            a_lo = (a - a_hi) - a_mid
            lhs = jnp.concatenate([jnp.where(in_hi, v, 0.0) for v in (tok_hi, tok_lo, a_hi, a_mid, a_lo)]
                                  + zeros, axis=0).astype(BF16)
            r = _dot_nt(lhs, lo_hot)
            idx_ref[0, e] = (r[0:n_hi] * float(ROUTE_TOK_SPLIT) + r[n_hi:2 * n_hi]).astype(jnp.int32)
            gate_ref[0, e] = r[2 * n_hi:3 * n_hi] + (r[3 * n_hi:4 * n_hi] + r[4 * n_hi:5 * n_hi])
            return carry

        lax.fori_loop(0, n_exp, ebody, 0)


def _route(h2, w_router, cap):
    b, n, d = h2.shape
    e = w_router.shape[1]
    tn = min(n, 1024)
    n_hi = cap // ROUTE_SLOT_LO
    return pl.pallas_call(
        functools.partial(_route_kernel, cap=cap, tn=tn),
        out_shape=(jax.ShapeDtypeStruct((b, e, n_hi, ROUTE_SLOT_LO), jnp.int32),
                   jax.ShapeDtypeStruct((b, e, n_hi, ROUTE_SLOT_LO), F32)),
        grid=(b, n // tn),
        in_specs=[pl.BlockSpec((1, tn, d), lambda bi, i: (bi, i, 0)),
                  pl.BlockSpec((e, d), lambda bi, i: (0, 0))],
        out_specs=(pl.BlockSpec((1, e, n_hi, ROUTE_SLOT_LO), lambda bi, i: (bi, 0, 0, 0)),
                   pl.BlockSpec((1, e, n_hi, ROUTE_SLOT_LO), lambda bi, i: (bi, 0, 0, 0))),
        scratch_shapes=[pltpu.VMEM((e, n), F32), pltpu.VMEM((e, n), F32), pltpu.VMEM((e, n), F32)],
        compiler_params=_cparams(("parallel", "arbitrary")),
        name="moe_route",
    )(h2, w_router.T)


def _moe(token_sets, w_router, wg, wu, wd, layer):
    e = w_router.shape[1]
    idxs, xs_sets, gate_sets = [], [], []
    for h in token_sets:
        b, n, d = h.shape
        cap = n * CAPACITY_FACTOR // e
        idx, gates = _route(h, w_router, cap)
        idx = idx.reshape(b, e, cap)
        idxs.append(idx)
        xs_sets.append(_moe_gather(h, idx).reshape(e, b * cap, d))
        gate_sets.append(gates.reshape(b, e, cap).transpose(1, 0, 2).reshape(e, b * cap, 1))
    ys = _moe_ffn(xs_sets, gate_sets, wg, wu, wd, layer)
    return [_moe_combine(y.reshape(e, idx.shape[0], idx.shape[2], h.shape[2]), idx, h.shape[1])
            for y, idx, h in zip(ys, idxs, token_sets)]


def _inproj_weights(w):
    d = w.shape[0]
    sec = np.cumsum([0, 384, 128, 128, 384, 384, 384, 256, 256, 256])
    aq, ak, av, bq, bk, bv, cq, ck, cv = [w[:, sec[i]:sec[i + 1]] for i in range(9)]
    aq = aq.reshape(d, A_KV_HEADS, A_Q_PER_KV, HEAD_DIM).transpose(0, 2, 1, 3).reshape(d, -1)
    return jnp.concatenate([aq, ak, cq, ck, av, bq * QK_SCALE_64, bk, bv, cv], axis=1).astype(BF16)


def _outproj_weights(w):
    d = w.shape[1]
    wa = A_Q_HEADS * HEAD_DIM
    a = w[:wa].reshape(A_KV_HEADS, A_Q_PER_KV, HEAD_DIM, d).transpose(1, 0, 2, 3).reshape(wa, d)
    return jnp.concatenate([a, w[wa:]], axis=0).astype(BF16)


def _rope_tables(s):
    t = jnp.arange(s)
    rows, cols = t // GRID_W, t % GRID_W

    def head_tables(d):
        q = d // 4
        inv = ROPE_THETA ** (-jnp.arange(q, dtype=F32) / q)
        cs, sn = [], []
        for pos in (rows, cols):
            ang = pos.astype(F32)[:, None] * inv[None, :]
            c, s_ = jnp.cos(ang), jnp.sin(ang)
            cs += [c, c]
            sn += [-s_, s_]
        c = jnp.concatenate(cs, axis=1)
        s_ = jnp.concatenate(sn, axis=1)
        reps = LANES // d
        return jnp.tile(c, (1, reps)), jnp.tile(s_, (1, reps))

    ca, sa = head_tables(HEAD_DIM)
    cc, sc = head_tables(C_QK_DIM)
    cos = jnp.concatenate([ca * QK_SCALE_64, ca, cc * QK_SCALE_32, cc], axis=1)
    sin = jnp.concatenate([sa * QK_SCALE_64, sa, sc * QK_SCALE_32, sc], axis=1)
    return cos, sin


def _ctx_tables(lc):
    ones = jnp.ones((lc, LANES), F32)
    cos = jnp.concatenate([ones * QK_SCALE_64, ones, ones * QK_SCALE_32, ones], axis=1)
    return cos, jnp.zeros_like(cos)


def kernel(x, c, ctx, c_ctx, w_ada, b_ada, g_mix_pre, g_mix_post, g_ffn_pre, g_ffn_post, w_in, w_out,
           a_sink, b_rpb, c_lam_q1, c_lam_k1, c_lam_q2, c_lam_k2, c_subln, w_router, w_gate, w_up, w_down):
    b, s, d = x.shape
    lc = ctx.shape[1]
    depth = w_in.shape[0]

    cos_x, sin_x = _rope_tables(s)
    cos_c, sin_c = _ctx_tables(lc)

    pad = (-(b + 1)) % 8
    cstack = jnp.concatenate([c, c_ctx[None, :], jnp.zeros((pad, d), F32)], axis=0)
    mod = _modulation(cstack, w_ada, b_ada)

    n_exp, ff = w_gate.shape[1], w_gate.shape[3]
    wg = w_gate.reshape(depth * n_exp, d, ff)
    wu = w_up.reshape(depth * n_exp, d, ff)
    wd = w_down.reshape(depth * n_exp, ff, d)

    tm = 512
    h_ctx = ctx
    for l in range(depth):
        need_ctx = l < depth - 1
        mx = mod[l, :b].reshape(b, 1, N_MOD, d)
        sh1, sc1, gt1, sh2, sc2, gt2 = [mx[:, :, k] for k in range(N_MOD)]
        mc = mod[l, b:b + 1].reshape(1, 1, N_MOD, d)
        csh1, csc1, cgt1, csh2, csc2, cgt2 = [mc[:, :, k] for k in range(N_MOD)]

        w_cat = _inproj_weights(w_in[l])
        w_o = _outproj_weights(w_out[l])
        lam_init = 0.8 - 0.6 * math.exp(-0.3 * l)
        lam = (jnp.exp(jnp.sum(c_lam_q1[l] * c_lam_k1[l])) - jnp.exp(jnp.sum(c_lam_q2[l] * c_lam_k2[l]))
               + lam_init).reshape(1).astype(F32)
        subln2 = jnp.tile(c_subln[l], 2).reshape(1, LANES)
        bias = _b_bias_tables(b_rpb[l], s // GRID_W, LOG2E)
        sink = a_sink[l] * LOG2E

        p = _inproj(x, g_mix_pre[l], sc1, sh1, cos_x, sin_x, w_cat, tm)
        pc = _inproj(h_ctx, g_mix_pre[l], csc1, csh1, cos_c, sin_c, w_cat, lc)

        oa = _attn_a(p, pc, sink, True)
        ob = _attn_b(p, pc, bias)
        od = _attn_c(p, pc, lam, subln2, lam_init, True)
        x1, h2 = _outproj(oa, ob, od, x, w_o, g_mix_post[l], gt1, g_ffn_pre[l], sc2, sh2, tm)
        if need_ctx:
            oac = _attn_a(p, pc, sink, False)
            obc = _attn_b_ctx(pc)
            odc = _attn_c(p, pc, lam, subln2, lam_init, False)
            c1, hc2 = _outproj(oac, obc, odc, h_ctx, w_o, g_mix_post[l], cgt1, g_ffn_pre[l], csc2, csh2, lc)
            y, yc = _moe([h2, hc2], w_router[l], wg, wu, wd, l)
            h_ctx = _resid(c1, yc, g_ffn_post[l], cgt2, lc)
        else:
            (y,) = _moe([h2], w_router[l], wg, wu, wd, l)
        x = _resid(x1, y, g_ffn_post[l], gt2, tm)
    return x
```

```python
import functools
import math

import numpy as np
import jax
import jax.numpy as jnp
from jax import lax
from jax.experimental import pallas as pl
from jax.experimental.pallas import tpu as pltpu

F32 = jnp.float32
BF16 = jnp.bfloat16

GRID_W = 64
HEAD_DIM = 64
A_Q_HEADS, A_KV_HEADS = 6, 2
A_Q_PER_KV = A_Q_HEADS // A_KV_HEADS
A_WINDOW = 128
A_BLOCK = 128
B_HEADS, B_MAX_ROWS, B_COLS = 6, 8, 16
C_HEADS, C_QK_DIM, C_V_DIM = 4, 32, 64
N_EXPERTS = 16
CAPACITY_FACTOR = 2
N_MOD = 6
ROPE_THETA = 10000.0
EPS = 1e-6
NEG_INF = -1e30
LOG2E = math.log2(math.e)
QK_SCALE_64 = HEAD_DIM ** -0.5 * LOG2E
QK_SCALE_32 = C_QK_DIM ** -0.5 * LOG2E

LANES = 128
ROPE_W = 1024
OFF_AQ, OFF_AK, OFF_CQ, OFF_CK, OFF_AV, OFF_BQ, OFF_BK, OFF_BV, OFF_CV = (
    0, 384, 512, 768, 1024, 1152, 1536, 1920, 2304)
IN_W = 2560
VMEM_LIMIT = 56 * 1024 * 1024


def _cparams(sem, vmem_limit=VMEM_LIMIT):
    return pltpu.CompilerParams(dimension_semantics=sem, vmem_limit_bytes=vmem_limit)


def _dot(a, b):
    return jnp.dot(a, b, preferred_element_type=F32)


def _dot_nt(a, b):
    return lax.dot_general(a, b, (((1,), (1,)), ((), ())), preferred_element_type=F32)


def _rms(x, g):
    return x * lax.rsqrt(jnp.mean(x * x, axis=-1, keepdims=True) + EPS) * g


def _mod_kernel(c_ref, w_ref, b_ref, o_ref):
    c = c_ref[...]
    a = c / (1.0 + jnp.exp(-c))
    o_ref[0] = jnp.dot(a, w_ref[0], preferred_element_type=F32,
                       precision=lax.Precision.HIGHEST) + b_ref[0]


def _modulation(cstack, w_ada, b_ada):
    nl, d, n = w_ada.shape
    rows = cstack.shape[0]
    tn = 1536
    return pl.pallas_call(
        _mod_kernel,
        out_shape=jax.ShapeDtypeStruct((nl, rows, n), F32),
        grid=(nl, n // tn),
        in_specs=[
            pl.BlockSpec((rows, d), lambda l, j: (0, 0)),
            pl.BlockSpec((1, d, tn), lambda l, j: (l, 0, j)),
            pl.BlockSpec((1, 1, tn), lambda l, j: (l, 0, j)),
        ],
        out_specs=pl.BlockSpec((1, rows, tn), lambda l, j: (l, 0, j)),
        compiler_params=_cparams(("parallel", "parallel")),
        name="adaln_modulation",
    )(cstack, w_ada, b_ada.reshape(nl, 1, n))


_ROPE_TABLE_OF_UNIT = (0, 0, 0, 1, 2, 2, 3, 3)


_ROPE_HALF_OF_CHUNK = (HEAD_DIM // 4, HEAD_DIM // 4, C_QK_DIM // 4, C_QK_DIM // 4)


def _inproj_kernel(x_ref, g_ref, sc_ref, sh_ref, cos_ref, sin_ref, w_ref, o_ref):
    x = x_ref[0]
    h = _rms(x, g_ref[...]) * (1.0 + sc_ref[0]) + sh_ref[0]
    hb = h.astype(BF16)
    for j in range(ROPE_W // 256):
        t0, t1 = _ROPE_TABLE_OF_UNIT[2 * j], _ROPE_TABLE_OF_UNIT[2 * j + 1]
        cos = jnp.concatenate([cos_ref[:, t0 * LANES:(t0 + 1) * LANES],
                               cos_ref[:, t1 * LANES:(t1 + 1) * LANES]], axis=1)
        sin = jnp.concatenate([sin_ref[:, t0 * LANES:(t0 + 1) * LANES],
                               sin_ref[:, t1 * LANES:(t1 + 1) * LANES]], axis=1)
        p = _dot(hb, w_ref[:, 256 * j:256 * (j + 1)])
        half = _ROPE_HALF_OF_CHUNK[j]
        lane = lax.broadcasted_iota(jnp.int32, (1, 256), 1)
        ps = jnp.where((lane & (2 * half - 1)) < half, pltpu.roll(p, 256 - half, 1), pltpu.roll(p, half, 1))
        o_ref[0, :, 256 * j:256 * (j + 1)] = (p * cos + ps * sin).astype(BF16)
    for j in range(ROPE_W // 256, IN_W // 256):
        o_ref[0, :, 256 * j:256 * (j + 1)] = _dot(hb, w_ref[:, 256 * j:256 * (j + 1)]).astype(BF16)


def _inproj(x, g, sc, sh, cos, sin, w_cat, tm):
    b, s, d = x.shape
    per_sample = sc.shape[0] > 1
    mod_map = (lambda bi, i: (bi, 0, 0)) if per_sample else (lambda bi, i: (0, 0, 0))
    return pl.pallas_call(
        _inproj_kernel,
        out_shape=jax.ShapeDtypeStruct((b, s, IN_W), BF16),
        grid=(b, s // tm),
        in_specs=[
            pl.BlockSpec((1, tm, d), lambda bi, i: (bi, i, 0)),
            pl.BlockSpec((1, d), lambda bi, i: (0, 0)),
            pl.BlockSpec((1, 1, d), mod_map),
            pl.BlockSpec((1, 1, d), mod_map),
            pl.BlockSpec((tm, 4 * LANES), lambda bi, i: (i, 0)),
            pl.BlockSpec((tm, 4 * LANES), lambda bi, i: (i, 0)),
            pl.BlockSpec((d, IN_W), lambda bi, i: (0, 0)),
        ],
        out_specs=pl.BlockSpec((1, tm, IN_W), lambda bi, i: (bi, i, 0)),
        compiler_params=_cparams(("parallel", "parallel")),
        name="norm_inproj_rope",
    )(x, g.reshape(1, d), sc, sh, cos, sin, w_cat)


def _lo_mask():
    return lax.broadcasted_iota(jnp.int32, (1, LANES), 1) < HEAD_DIM


def _split_halves(q):
    lo = _lo_mask()
    zero = jnp.zeros_like(q)
    return jnp.where(lo, q, zero), jnp.where(lo, zero, q)


A_STEP_BLOCKS = 4


def _attn_a_kernel(sink_ref, q_ref, *refs, seq, latent):
    if latent:
        k_ref, v_ref, kc_ref, vc_ref, o_ref = refs
    else:
        kc_ref, vc_ref, o_ref = refs
    tq = q_ref.shape[1]
    npair = A_Q_HEADS // 2
    kc = kc_ref[0]
    vc = vc_ref[0]
    lo = _lo_mask()
    if latent:
        i = pl.program_id(1)
        band = tq + 2 * A_BLOCK
        start = pl.multiple_of(jnp.clip(i * tq - A_BLOCK, 0, seq - band), A_BLOCK)
        kb = k_ref[0, pl.ds(start, band), :]
        vb = v_ref[0, pl.ds(start, band), :]
        delta = ((start - i * tq) + lax.broadcasted_iota(jnp.int32, (1, band), 1)
                 - lax.broadcasted_iota(jnp.int32, (tq, 1), 0))
        wmask = jnp.where(jnp.abs(delta) <= A_WINDOW, 0.0, NEG_INF)
        wmask = jnp.concatenate([wmask, wmask], axis=0)
    outs = []
    for t in range(npair):
        qa, qb = _split_halves(q_ref[0, :, t * LANES:(t + 1) * LANES])
        q2 = jnp.concatenate([qa, qb], axis=0)
        sink = jnp.concatenate([jnp.full((tq, 1), sink_ref[t], F32),
                                jnp.full((tq, 1), sink_ref[npair + t], F32)], axis=0)
        s_ctx = _dot_nt(q2, kc)
        m = jnp.maximum(jnp.max(s_ctx, axis=-1, keepdims=True), sink)
        if latent:
            s_lat = _dot_nt(q2, kb) + wmask
            m = jnp.maximum(m, jnp.max(s_lat, axis=-1, keepdims=True))
            e_lat = jnp.exp2(s_lat - m)
        e_ctx = jnp.exp2(s_ctx - m)
        l = jnp.sum(e_ctx, axis=-1, keepdims=True) + jnp.exp2(sink - m)
        o = _dot(e_ctx.astype(BF16), vc)
        if latent:
            l = l + jnp.sum(e_lat, axis=-1, keepdims=True)
            o = o + _dot(e_lat.astype(BF16), vb)
        o = o / l
        outs.append(jnp.where(lo, o[:tq], o[tq:]))
    o_ref[0] = jnp.concatenate(outs, axis=1).astype(BF16)


def _attn_a(p, pc, sink, latent):
    src = p if latent else pc
    b, s, _ = src.shape
    lc = pc.shape[1]
    tq = A_STEP_BLOCKS * A_BLOCK if latent else s
    wq = A_Q_HEADS * HEAD_DIM
    in_specs = [pl.BlockSpec(memory_space=pltpu.SMEM),
                pl.BlockSpec((1, tq, wq), lambda bi, i: (bi, i, OFF_AQ // wq))]
    args = [sink, src]
    if latent:
        in_specs += [pl.BlockSpec((1, s, LANES), lambda bi, i: (bi, 0, OFF_AK // LANES)),
                     pl.BlockSpec((1, s, LANES), lambda bi, i: (bi, 0, OFF_AV // LANES))]
        args += [p, p]
    in_specs += [pl.BlockSpec((1, lc, LANES), lambda bi, i: (bi, 0, OFF_AK // LANES)),
                 pl.BlockSpec((1, lc, LANES), lambda bi, i: (bi, 0, OFF_AV // LANES))]
    args += [pc, pc]
    return pl.pallas_call(
        functools.partial(_attn_a_kernel, seq=s, latent=latent),
        out_shape=jax.ShapeDtypeStruct((b, s, wq), BF16),
        grid=(b, s // tq),
        in_specs=in_specs,
        out_specs=pl.BlockSpec((1, tq, wq), lambda bi, i: (bi, i, 0)),
        compiler_params=_cparams(("parallel", "arbitrary")),
        name="attn_window_gqa" + ("" if latent else "_ctx"),
    )(*args)


B_GROUP_ROWS = 4
B_UNION_ROWS = B_GROUP_ROWS + B_MAX_ROWS


def _attn_b_softmax_out(s_parts, v_parts):
    m = functools.reduce(jnp.maximum, [jnp.max(s, axis=-1, keepdims=True) for s in s_parts])
    es = [jnp.exp2(s - m) for s in s_parts]
    l = functools.reduce(jnp.add, [jnp.sum(e, axis=-1, keepdims=True) for e in es])
    o = functools.reduce(jnp.add, [_dot(e.astype(BF16), v) for e, v in zip(es, v_parts)])
    return o / l


def _attn_b_kernel(q_ref, k_ref, v_ref, kc_ref, vc_ref, bias_ref, o_ref, *, rows_n):
    i = pl.program_id(1)
    npair = B_HEADS // 2
    lo = _lo_mask()
    tq = q_ref.shape[1]
    nk = B_UNION_ROWS * GRID_W
    us = jnp.clip(i * B_GROUP_ROWS - B_MAX_ROWS // 2, 0, rows_n - B_UNION_ROWS)
    kstart = pl.multiple_of(us * GRID_W, GRID_W)
    outs = []
    for t in range(npair):
        cols = slice(t * LANES, (t + 1) * LANES)
        qa, qb = _split_halves(q_ref[0, :, cols])
        q2 = jnp.concatenate([qa, qb], axis=0)
        s_lat = _dot_nt(q2, k_ref[0, pl.ds(kstart, nk), cols]) + bias_ref[0, t]
        s_ctx = _dot_nt(q2, kc_ref[0, :, cols])
        o = _attn_b_softmax_out([s_lat, s_ctx], [v_ref[0, pl.ds(kstart, nk), cols], vc_ref[0, :, cols]])
        outs.append(jnp.where(lo, o[:tq], o[tq:]))
    o_ref[0] = jnp.concatenate(outs, axis=1).astype(BF16)


def _attn_b_ctx_kernel(q_ref, kc_ref, vc_ref, o_ref):
    npair = B_HEADS // 2
    lo = _lo_mask()
    n = q_ref.shape[1]
    for t in range(npair):
        cols = slice(t * LANES, (t + 1) * LANES)
        qa, qb = _split_halves(q_ref[0, :, cols])
        q2 = jnp.concatenate([qa, qb], axis=0)
        s_ctx = _dot_nt(q2, kc_ref[0, :, cols])
        o = _attn_b_softmax_out([s_ctx], [vc_ref[0, :, cols]])
        o_ref[0, :, cols] = jnp.where(lo, o[:n], o[n:]).astype(BF16)


def _attn_b(p, pc, bias):
    b, s, _ = p.shape
    lc = pc.shape[1]
    w = B_HEADS * HEAD_DIM
    rows_n = s // GRID_W
    tq = B_GROUP_ROWS * GRID_W
    ng = s // tq
    variant = lambda bi, i: (jnp.minimum(i, 1) + (i == ng - 1).astype(jnp.int32), 0, 0, 0)
    return pl.pallas_call(
        functools.partial(_attn_b_kernel, rows_n=rows_n),
        out_shape=jax.ShapeDtypeStruct((b, s, w), BF16),
        grid=(b, ng),
        in_specs=[
            pl.BlockSpec((1, tq, w), lambda bi, i: (bi, i, OFF_BQ // w)),
            pl.BlockSpec((1, s, w), lambda bi, i: (bi, 0, OFF_BK // w)),
            pl.BlockSpec((1, s, w), lambda bi, i: (bi, 0, OFF_BV // w)),
            pl.BlockSpec((1, lc, w), lambda bi, i: (bi, 0, OFF_BK // w)),
            pl.BlockSpec((1, lc, w), lambda bi, i: (bi, 0, OFF_BV // w)),
            pl.BlockSpec((1,) + bias.shape[1:], variant),
        ],
        out_specs=pl.BlockSpec((1, tq, w), lambda bi, i: (bi, i, 0)),
        compiler_params=_cparams(("parallel", "arbitrary")),
        name="attn_neighbourhood",
    )(p, p, p, pc, pc, bias)


def _attn_b_ctx(pc):
    b, lc, _ = pc.shape
    w = B_HEADS * HEAD_DIM
    return pl.pallas_call(
        _attn_b_ctx_kernel,
        out_shape=jax.ShapeDtypeStruct((b, lc, w), BF16),
        grid=(b,),
        in_specs=[
            pl.BlockSpec((1, lc, w), lambda bi: (bi, 0, OFF_BQ // w)),
            pl.BlockSpec((1, lc, w), lambda bi: (bi, 0, OFF_BK // w)),
            pl.BlockSpec((1, lc, w), lambda bi: (bi, 0, OFF_BV // w)),
        ],
        out_specs=pl.BlockSpec((1, lc, w), lambda bi: (bi, 0, 0)),
        compiler_params=_cparams(("parallel",)),
        name="attn_neighbourhood_ctx",
    )(pc, pc, pc)


def _b_bias_tables(rpb, rows_n, scale):
    g_rows, u_rows = B_GROUP_ROWS, B_UNION_ROWS
    col = np.arange(GRID_W)
    cstart = np.clip(col - B_COLS // 2, 0, GRID_W - B_COLS)
    col_ok = (col[None, :] >= cstart[:, None]) & (col[None, :] < cstart[:, None] + B_COLS)
    dc_idx = np.clip(col[None, :] - col[:, None], -(B_COLS - 1), B_COLS - 1) + (B_COLS - 1)
    pick_col = (dc_idx[None] == np.arange(2 * B_COLS - 1)[:, None, None]).astype(np.float32)
    n_groups = rows_n // g_rows
    patterns, group_variant = [], []
    for g in range(n_groups):
        us = np.clip(g * g_rows - B_MAX_ROWS // 2, 0, rows_n - u_rows)
        r = g * g_rows + np.arange(g_rows)[:, None]
        rs = np.clip(r - B_MAX_ROWS // 2, 0, rows_n - B_MAX_ROWS)
        key_row = us + np.arange(u_rows)[None, :]
        dr = np.where((key_row >= rs) & (key_row < rs + B_MAX_ROWS), key_row - r + (B_MAX_ROWS - 1), -1)
        if not any(np.array_equal(dr, p_) for p_ in patterns):
            patterns.append(dr)
        group_variant.append([np.array_equal(dr, p_) for p_ in patterns].index(True))
    assert group_variant == [0] + [1] * (n_groups - 2) + [2], group_variant
    dr = np.stack(patterns)
    pick_row = (dr[..., None] == np.arange(2 * B_MAX_ROWS - 1)).astype(np.float32)
    hi = lax.Precision.HIGHEST
    rsel = jnp.einsum('vuwd,hdc->vhuwc', pick_row, rpb.astype(F32), precision=hi)
    t = jnp.einsum('vhuwc,cqk->vhuqwk', rsel, pick_col, precision=hi) * scale
    ok = (dr >= 0)[:, None, :, None, :, None] & col_ok[None, None, None, :, None, :]
    t = jnp.where(ok, t, NEG_INF)
    return t.reshape(len(patterns), B_HEADS // 2, 2 * g_rows * GRID_W, u_rows * GRID_W)


C_KEY_CHUNK = 2048
C_Q_TILE = 256


def _attn_c_kernel(lam_ref, q_ref, *refs, latent, out_scale):
    if latent:
        k_ref, v_ref, kc_ref, vc_ref, g_ref, o_ref = refs
    else:
        kc_ref, vc_ref, g_ref, o_ref = refs
    tq = q_ref.shape[1]
    lam = lam_ref[0]
    q = q_ref[0]
    quarter = lax.broadcasted_iota(jnp.int32, (1, LANES), 1) // C_QK_DIM
    zero = jnp.zeros_like(q)
    q4 = jnp.concatenate([jnp.where(quarter == j, q, zero) for j in range(4)], axis=0)
    chunks = []
    if latent:
        kc_n = min(C_KEY_CHUNK, k_ref.shape[1])
        chunks += [(k_ref, v_ref, c * kc_n, kc_n) for c in range(k_ref.shape[1] // kc_n)]
    chunks.append((kc_ref, vc_ref, 0, kc_ref.shape[1]))
    m = l = acc = None
    for kr, vr, st, n in chunks:
        s = _dot_nt(q4, kr[0, st:st + n, :])
        mc = jnp.max(s, axis=-1, keepdims=True)
        if m is None:
            m = mc
            e = jnp.exp2(s - m)
            l = jnp.sum(e, axis=-1, keepdims=True)
            acc = _dot(e.astype(BF16), vr[0, st:st + n, :])
        else:
            m_new = jnp.maximum(m, mc)
            alpha = jnp.exp2(m - m_new)
            e = jnp.exp2(s - m_new)
            l = l * alpha + jnp.sum(e, axis=-1, keepdims=True)
            acc = acc * alpha + _dot(e.astype(BF16), vr[0, st:st + n, :])
            m = m_new
    o4 = acc / l
    outs = [o4[2 * h * tq:(2 * h + 1) * tq] - lam * o4[(2 * h + 1) * tq:(2 * h + 2) * tq]
            for h in range(2)]
    lo = _lo_mask()
    o = jnp.where(lo, outs[0], outs[1])
    sq = o * o
    s_lo = jnp.sum(jnp.where(lo, sq, 0.0), axis=-1, keepdims=True)
    s_hi = jnp.sum(jnp.where(lo, 0.0, sq), axis=-1, keepdims=True)
    ms = jnp.where(lo, s_lo, s_hi) * (1.0 / C_V_DIM)
    o_ref[0] = (o * lax.rsqrt(ms + EPS) * g_ref[...] * out_scale).astype(BF16)


def _attn_c(p, pc, lam, subln2, lam_init, latent):
    src = p if latent else pc
    b, s, _ = src.shape
    lc = pc.shape[1]
    tq = C_Q_TILE if latent else s
    npair = C_HEADS // 2
    in_specs = [pl.BlockSpec(memory_space=pltpu.SMEM),
                pl.BlockSpec((1, tq, LANES), lambda bi, hp, i: (bi, i, OFF_CQ // LANES + hp))]
    args = [lam, src]
    if latent:
        in_specs += [pl.BlockSpec((1, s, LANES), lambda bi, hp, i: (bi, 0, OFF_CK // LANES + hp)),
                     pl.BlockSpec((1, s, LANES), lambda bi, hp, i: (bi, 0, OFF_CV // LANES + hp))]
        args += [p, p]
    in_specs += [pl.BlockSpec((1, lc, LANES), lambda bi, hp, i: (bi, 0, OFF_CK // LANES + hp)),
                 pl.BlockSpec((1, lc, LANES), lambda bi, hp, i: (bi, 0, OFF_CV // LANES + hp)),
                 pl.BlockSpec((1, LANES), lambda bi, hp, i: (0, 0))]
    args += [pc, pc, subln2]
    return pl.pallas_call(
        functools.partial(_attn_c_kernel, latent=latent, out_scale=1.0 - lam_init),
        out_shape=jax.ShapeDtypeStruct((b, s, C_HEADS * C_V_DIM), BF16),
        grid=(b, npair, s // tq),
        in_specs=in_specs,
        out_specs=pl.BlockSpec((1, tq, LANES), lambda bi, hp, i: (bi, i, hp)),
        compiler_params=_cparams(("parallel", "parallel", "arbitrary")),
        name="attn_differential" + ("" if latent else "_ctx"),
    )(*args)


def _outproj_kernel(oa_ref, ob_ref, oc_ref, x_ref, w_ref, gpost_ref, gt_ref, gpre_ref, sc_ref, sh_ref,
                    x1_ref, h2_ref):
    wa = oa_ref.shape[2]
    wb = ob_ref.shape[2]
    y = (_dot(oa_ref[0], w_ref[0:wa, :]) + _dot(ob_ref[0], w_ref[wa:wa + wb, :])
         + _dot(oc_ref[0], w_ref[wa + wb:, :]))
    x1 = x_ref[0] + gt_ref[0] * _rms(y, gpost_ref[...])
    x1_ref[0] = x1
    h2_ref[0] = _rms(x1, gpre_ref[...]) * (1.0 + sc_ref[0]) + sh_ref[0]


def _outproj(oa, ob, oc, x, w_out, g_post, gt, g_pre, sc, sh, tm):
    b, s, d = x.shape
    per_sample = gt.shape[0] > 1
    mod_map = (lambda bi, i: (bi, 0, 0)) if per_sample else (lambda bi, i: (0, 0, 0))
    row = lambda bi, i: (bi, i, 0)
    const2 = lambda bi, i: (0, 0)
    return pl.pallas_call(
        _outproj_kernel,
        out_shape=(jax.ShapeDtypeStruct((b, s, d), F32), jax.ShapeDtypeStruct((b, s, d), F32)),
        grid=(b, s // tm),
        in_specs=[
            pl.BlockSpec((1, tm, oa.shape[2]), row),
            pl.BlockSpec((1, tm, ob.shape[2]), row),
            pl.BlockSpec((1, tm, oc.shape[2]), row),
            pl.BlockSpec((1, tm, d), row),
            pl.BlockSpec(w_out.shape, const2),
            pl.BlockSpec((1, d), const2),
            pl.BlockSpec((1, 1, d), mod_map),
            pl.BlockSpec((1, d), const2),
            pl.BlockSpec((1, 1, d), mod_map),
            pl.BlockSpec((1, 1, d), mod_map),
        ],
        out_specs=(pl.BlockSpec((1, tm, d), row), pl.BlockSpec((1, tm, d), row)),
        compiler_params=_cparams(("parallel", "parallel")),
        name="outproj_residual_norm",
    )(oa, ob, oc, x, w_out, g_post.reshape(1, d), gt, g_pre.reshape(1, d), sc, sh)


ROW_UNROLL = 8


GATHER_ROWS = 16
MOVE_ROWS_PER_STEP = 512


def _experts_per_step(n_experts, cap):
    return min(n_experts, max(1, MOVE_ROWS_PER_STEP // cap))


def _gather_kernel(idx_ref, h_ref, o_ref):
    cap, d = o_ref.shape[2], o_ref.shape[3]
    sub = lax.broadcasted_iota(jnp.int32, (ROW_UNROLL, d), 0)

    for k in range(o_ref.shape[0]):
        def body(j, carry, k=k):
            base = pl.multiple_of(j * GATHER_ROWS, GATHER_ROWS)
            halves = []
            for g in range(GATHER_ROWS // ROW_UNROLL):
                tile = jnp.zeros((ROW_UNROLL, d), F32)
                for u in range(ROW_UNROLL):
                    n = idx_ref[k, 0, base + g * ROW_UNROLL + u]
                    row = jnp.broadcast_to(h_ref[0, pl.ds(n, 1), :], (ROW_UNROLL, d))
                    tile = jnp.where(sub == u, row, tile)
                halves.append(tile)
            o_ref[k, 0, pl.ds(base, GATHER_ROWS), :] = jnp.concatenate(halves, axis=0).astype(BF16)
            return carry

        lax.fori_loop(0, cap // GATHER_ROWS, body, 0)


def _moe_gather(h2, idx):
    b, n, d = h2.shape
    e, cap = idx.shape[1], idx.shape[2]
    es = _experts_per_step(e, cap)
    return pl.pallas_call(
        _gather_kernel,
        out_shape=jax.ShapeDtypeStruct((e, b, cap, d), BF16),
        grid=(b, e // es),
        in_specs=[
            pl.BlockSpec((es, 1, cap), lambda bi, ei: (bi * (e // es) + ei, 0, 0), memory_space=pltpu.SMEM),
            pl.BlockSpec((1, n, d), lambda bi, ei: (bi, 0, 0)),
        ],
        out_specs=pl.BlockSpec((es, 1, cap, d), lambda bi, ei: (ei, bi, 0, 0)),
        compiler_params=_cparams(("parallel", "arbitrary")),
        name="moe_gather",
    )(idx.reshape(b * e, 1, cap), h2)


FF_CHUNK = 512
CAST_ROWS = 256


def _ffn_kernel(*refs, tiles, first_expert):
    ns = len(tiles)
    xs_refs, gate_refs = refs[0:2 * ns:2], refs[1:2 * ns:2]
    wg_hbm, wu_hbm, wd_hbm = refs[2 * ns:2 * ns + 3]
    o_refs = refs[2 * ns + 3:3 * ns + 3]
    stage_g, stage_u, stage_d, wg_ref, wu_ref, wd_ref, sems = refs[3 * ns + 3:]
    e = pl.program_id(0)
    r = pl.program_id(1)

    def weight_copies(expert):
        row = first_expert + expert
        return [pltpu.make_async_copy(wg_hbm.at[row], stage_g, sems.at[0]),
                pltpu.make_async_copy(wu_hbm.at[row], stage_u, sems.at[1]),
                pltpu.make_async_copy(wd_hbm.at[row], stage_d, sems.at[2])]

    @pl.when(r == 0)
    def _():
        @pl.when(e == 0)
        def _():
            for c in weight_copies(0):
                c.start()

        for c in weight_copies(e):
            c.wait()
        for src, dst in ((stage_g, wg_ref), (stage_u, wu_ref), (stage_d, wd_ref)):
            for r0 in range(0, src.shape[0], CAST_ROWS):
                dst[r0:r0 + CAST_ROWS, :] = src[r0:r0 + CAST_ROWS, :].astype(BF16)

        @pl.when(e + 1 < pl.num_programs(0))
        def _():
            for c in weight_copies(e + 1):
                c.start()

    ff = wg_ref.shape[1]

    def swiglu(xs_ref, gate_ref, o_ref):
        xs = xs_ref[0]
        for c in range(ff // FF_CHUNK):
            cs = slice(c * FF_CHUNK, (c + 1) * FF_CHUNK)
            g = _dot(xs, wg_ref[:, cs])
            u = _dot(xs, wu_ref[:, cs])
            hid = ((g / (1.0 + jnp.exp(-g))) * u).astype(BF16)
            part = _dot(hid, wd_ref[cs, :])
            if c == 0:
                o_ref[0] = part
            elif c < ff // FF_CHUNK - 1:
                o_ref[0] += part
            else:
                o_ref[0] = (o_ref[0] + part) * gate_ref[0]

    first = 0
    for k, nt in enumerate(tiles):
        pl.when(jnp.logical_and(r >= first, r < first + nt))(
            functools.partial(swiglu, xs_refs[k], gate_refs[k], o_refs[k]))
        first += nt


FFN_ROWS = 512
FFN_VMEM_LIMIT = 61 * 1024 * 1024


def _moe_ffn(xs_sets, gate_sets, wg, wu, wd, layer):
    e, _, d = xs_sets[0].shape
    ff = wg.shape[2]
    in_specs, out_specs, out_shape, tiles, args = [], [], [], [], []
    first = 0
    for xs, gates in zip(xs_sets, gate_sets):
        r = xs.shape[1]
        tr = min(r, FFN_ROWS)
        nt = r // tr
        row_map = functools.partial(lambda ei, ri, first, nt: (ei, jnp.clip(ri - first, 0, nt - 1), 0),
                                    first=first, nt=nt)
        in_specs += [pl.BlockSpec((1, tr, d), row_map), pl.BlockSpec((1, tr, 1), row_map)]
        out_specs.append(pl.BlockSpec((1, tr, d), row_map))
        out_shape.append(jax.ShapeDtypeStruct((e, r, d), F32))
        args += [xs, gates]
        tiles.append(nt)
        first += nt
    in_specs += [pl.BlockSpec(memory_space=pl.ANY)] * 3
    return pl.pallas_call(
        functools.partial(_ffn_kernel, tiles=tuple(tiles), first_expert=layer * e),
        out_shape=out_shape,
        grid=(e, first),
        in_specs=in_specs,
        out_specs=out_specs,
        scratch_shapes=[pltpu.VMEM((d, ff), F32), pltpu.VMEM((d, ff), F32), pltpu.VMEM((ff, d), F32),
                        pltpu.VMEM((d, ff), BF16), pltpu.VMEM((d, ff), BF16), pltpu.VMEM((ff, d), BF16),
                        pltpu.SemaphoreType.DMA((3,))],
        compiler_params=_cparams(("arbitrary", "arbitrary"), FFN_VMEM_LIMIT),
        name="moe_expert_ffn",
    )(*args, wg, wu, wd)


def _combine_kernel(idx_ref, y_ref, o_ref):
    cap = y_ref.shape[2]

    @pl.when(pl.program_id(1) == 0)
    def _():
        o_ref[...] = jnp.zeros_like(o_ref)

    for k in range(y_ref.shape[0]):
        def body(j, carry, k=k):
            base = pl.multiple_of(j * ROW_UNROLL, ROW_UNROLL)
            toks = [idx_ref[k, 0, base + u] for u in range(ROW_UNROLL)]
            ys = y_ref[k, 0, pl.ds(base, ROW_UNROLL), :]
            rows = [o_ref[0, pl.ds(n, 1), :] for n in toks]
            for u, n in enumerate(toks):
                o_ref[0, pl.ds(n, 1), :] = rows[u] + ys[u:u + 1]
            return carry

        lax.fori_loop(0, cap // ROW_UNROLL, body, 0)


def _moe_combine(y, idx, n):
    e, b, cap, d = y.shape
    es = _experts_per_step(e, cap)
    return pl.pallas_call(
        _combine_kernel,
        out_shape=jax.ShapeDtypeStruct((b, n, d), F32),
        grid=(b, e // es),
        in_specs=[
            pl.BlockSpec((es, 1, cap), lambda bi, ei: (bi * (e // es) + ei, 0, 0), memory_space=pltpu.SMEM),
            pl.BlockSpec((es, 1, cap, d), lambda bi, ei: (ei, bi, 0, 0)),
        ],
        out_specs=pl.BlockSpec((1, n, d), lambda bi, ei: (bi, 0, 0)),
        compiler_params=_cparams(("parallel", "arbitrary")),
        name="moe_combine",
    )(idx.reshape(b * e, 1, cap), y)


def _resid_kernel(x_ref, y_ref, g_ref, gt_ref, o_ref):
    o_ref[0] = x_ref[0] + gt_ref[0] * _rms(y_ref[0], g_ref[...])


def _resid(x, y, g, gt, tm):
    b, s, d = x.shape
    per_sample = gt.shape[0] > 1
    mod_map = (lambda bi, i: (bi, 0, 0)) if per_sample else (lambda bi, i: (0, 0, 0))
    row = lambda bi, i: (bi, i, 0)
    return pl.pallas_call(
        _resid_kernel,
        out_shape=jax.ShapeDtypeStruct((b, s, d), F32),
        grid=(b, s // tm),
        in_specs=[pl.BlockSpec((1, tm, d), row), pl.BlockSpec((1, tm, d), row),
                  pl.BlockSpec((1, d), lambda bi, i: (0, 0)), pl.BlockSpec((1, 1, d), mod_map)],
        out_specs=pl.BlockSpec((1, tm, d), row),
        compiler_params=_cparams(("parallel", "parallel")),
        name="ffn_residual",
    )(x, y, g.reshape(1, d), gt)


ROUTE_SLOT_LO = 32
ROUTE_TOK_SPLIT = 64


def _count(mask):
    return jnp.sum(jnp.where(mask, 1.0, 0.0), axis=1, keepdims=True)


def _route_kernel(h_ref, wr_ref, idx_ref, gate_ref, logit_ref, aff_ref, posm_ref, *, cap, tn):
    i = pl.program_id(1)
    n_exp, n_tok = logit_ref.shape
    h = h_ref[0]
    h_hi = h.astype(BF16)
    h_lo = (h - h_hi.astype(F32)).astype(BF16)
    w = wr_ref[...]
    w_hi = w.astype(BF16)
    w_lo = (w - w_hi.astype(F32)).astype(BF16)
    logit_ref[:, pl.ds(pl.multiple_of(i * tn, tn), tn)] = (
        _dot_nt(w_hi, h_hi) + (_dot_nt(w_hi, h_lo) + _dot_nt(w_lo, h_hi)))

    @pl.when(i == pl.num_programs(1) - 1)
    def _():
        lg = logit_ref[...]
        ex = jnp.exp(lg - jnp.max(lg, axis=0, keepdims=True))
        aff = ex / jnp.sum(ex, axis=0, keepdims=True)
        aff_ref[...] = aff
        capf = float(cap)

        def tbody(it, t):
            cand = t | jnp.left_shift(jnp.int32(1), 30 - it)
            cnt = _count(aff >= lax.bitcast_convert_type(cand, F32))
            return jnp.where(cnt >= capf, cand, t)

        t = lax.fori_loop(0, 31, tbody, jnp.zeros((n_exp, 1), jnp.int32))
        above = aff >= lax.bitcast_convert_type(t + 1, F32)
        tied = jnp.logical_and(aff >= lax.bitcast_convert_type(t, F32), jnp.logical_not(above))
        need = capf - _count(above)
        tok = lax.broadcasted_iota(jnp.int32, (1, n_tok), 1)
        nbits = n_tok.bit_length()

        def mbody(it, bound):
            cand = bound | jnp.left_shift(jnp.int32(1), nbits - 1 - it)
            f = _count(jnp.logical_and(tied, tok < cand))
            return jnp.where(f <= need, cand, bound)

        bound = lax.fori_loop(0, nbits, mbody, jnp.zeros((n_exp, 1), jnp.int32))
        sel = jnp.logical_or(above, jnp.logical_and(tied, tok < bound))
        self = jnp.where(sel, 1.0, 0.0)
        csum = self
        shift = 1
        while shift < n_tok:
            csum = csum + jnp.where(tok >= shift, pltpu.roll(csum, shift, 1), 0.0)
            shift *= 2
        posm_ref[...] = jnp.where(sel, csum - self, -1.0)

        tok_hi = (tok // ROUTE_TOK_SPLIT).astype(F32)
        tok_lo = (tok % ROUTE_TOK_SPLIT).astype(F32)
        n_hi = cap // ROUTE_SLOT_LO
        hi_iota = lax.broadcasted_iota(jnp.int32, (n_hi, 1), 0)
        lo_iota = lax.broadcasted_iota(jnp.int32, (ROUTE_SLOT_LO, 1), 0)
        pad_rows = (-5 * n_hi) % 16
        zeros = [jnp.zeros((pad_rows, n_tok), F32)] if pad_rows else []

        def ebody(e, carry):
            slot = posm_ref[pl.ds(e, 1), :].astype(jnp.int32)
            in_hi = (slot // ROUTE_SLOT_LO) == hi_iota
            lo_hot = jnp.where((slot % ROUTE_SLOT_LO) == lo_iota, 1.0, 0.0).astype(BF16)
            a = aff_ref[pl.ds(e, 1), :]
            a_hi = a.astype(BF16).astype(F32)
            a_mid = (a - a_hi).astype(BF16).astype(F32)
            a_lo = (a - a_hi) - a_mid
            lhs = jnp.concatenate([jnp.where(in_hi, v, 0.0) for v in (tok_hi, tok_lo, a_hi, a_mid, a_lo)]
                                  + zeros, axis=0).astype(BF16)
            r = _dot_nt(lhs, lo_hot)
            idx_ref[0, e] = (r[0:n_hi] * float(ROUTE_TOK_SPLIT) + r[n_hi:2 * n_hi]).astype(jnp.int32)
            gate_ref[0, e] = r[2 * n_hi:3 * n_hi] + (r[3 * n_hi:4 * n_hi] + r[4 * n_hi:5 * n_hi])
            return carry

        lax.fori_loop(0, n_exp, ebody, 0)


def _route(h2, w_router, cap):
    b, n, d = h2.shape
    e = w_router.shape[1]
    tn = min(n, 1024)
    n_hi = cap // ROUTE_SLOT_LO
    return pl.pallas_call(
        functools.partial(_route_kernel, cap=cap, tn=tn),
        out_shape=(jax.ShapeDtypeStruct((b, e, n_hi, ROUTE_SLOT_LO), jnp.int32),
                   jax.ShapeDtypeStruct((b, e, n_hi, ROUTE_SLOT_LO), F32)),
        grid=(b, n // tn),
        in_specs=[pl.BlockSpec((1, tn, d), lambda bi, i: (bi, i, 0)),
                  pl.BlockSpec((e, d), lambda bi, i: (0, 0))],
        out_specs=(pl.BlockSpec((1, e, n_hi, ROUTE_SLOT_LO), lambda bi, i: (bi, 0, 0, 0)),
                   pl.BlockSpec((1, e, n_hi, ROUTE_SLOT_LO), lambda bi, i: (bi, 0, 0, 0))),
        scratch_shapes=[pltpu.VMEM((e, n), F32), pltpu.VMEM((e, n), F32), pltpu.VMEM((e, n), F32)],
        compiler_params=_cparams(("parallel", "arbitrary")),
        name="moe_route",
    )(h2, w_router.T)


def _moe(token_sets, w_router, wg, wu, wd, layer):
    e = w_router.shape[1]
    idxs, xs_sets, gate_sets = [], [], []
    for h in token_sets:
        b, n, d = h.shape
        cap = n * CAPACITY_FACTOR // e
        idx, gates = _route(h, w_router, cap)
        idx = idx.reshape(b, e, cap)
        idxs.append(idx)
        xs_sets.append(_moe_gather(h, idx).reshape(e, b * cap, d))
        gate_sets.append(gates.reshape(b, e, cap).transpose(1, 0, 2).reshape(e, b * cap, 1))
    ys = _moe_ffn(xs_sets, gate_sets, wg, wu, wd, layer)
    return [_moe_combine(y.reshape(e, idx.shape[0], idx.shape[2], h.shape[2]), idx, h.shape[1])
            for y, idx, h in zip(ys, idxs, token_sets)]


def _inproj_weights(w):
    d = w.shape[0]
    sec = np.cumsum([0, 384, 128, 128, 384, 384, 384, 256, 256, 256])
    aq, ak, av, bq, bk, bv, cq, ck, cv = [w[:, sec[i]:sec[i + 1]] for i in range(9)]
    aq = aq.reshape(d, A_KV_HEADS, A_Q_PER_KV, HEAD_DIM).transpose(0, 2, 1, 3).reshape(d, -1)
    return jnp.concatenate([aq, ak, cq, ck, av, bq * QK_SCALE_64, bk, bv, cv], axis=1).astype(BF16)


def _outproj_weights(w):
    d = w.shape[1]
    wa = A_Q_HEADS * HEAD_DIM
    a = w[:wa].reshape(A_KV_HEADS, A_Q_PER_KV, HEAD_DIM, d).transpose(1, 0, 2, 3).reshape(wa, d)
    return jnp.concatenate([a, w[wa:]], axis=0).astype(BF16)


def _rope_tables(s):
    t = jnp.arange(s)
    rows, cols = t // GRID_W, t % GRID_W

    def head_tables(d):
        q = d // 4
        inv = ROPE_THETA ** (-jnp.arange(q, dtype=F32) / q)
        cs, sn = [], []
        for pos in (rows, cols):
            ang = pos.astype(F32)[:, None] * inv[None, :]
            c, s_ = jnp.cos(ang), jnp.sin(ang)
            cs += [c, c]
            sn += [-s_, s_]
        c = jnp.concatenate(cs, axis=1)
        s_ = jnp.concatenate(sn, axis=1)
        reps = LANES // d
        return jnp.tile(c, (1, reps)), jnp.tile(s_, (1, reps))

    ca, sa = head_tables(HEAD_DIM)
    cc, sc = head_tables(C_QK_DIM)
    cos = jnp.concatenate([ca * QK_SCALE_64, ca, cc * QK_SCALE_32, cc], axis=1)
    sin = jnp.concatenate([sa * QK_SCALE_64, sa, sc * QK_SCALE_32, sc], axis=1)
    return cos, sin


def _ctx_tables(lc):
    ones = jnp.ones((lc, LANES), F32)
    cos = jnp.concatenate([ones * QK_SCALE_64, ones, ones * QK_SCALE_32, ones], axis=1)
    return cos, jnp.zeros_like(cos)


def kernel(x, c, ctx, c_ctx, w_ada, b_ada, g_mix_pre, g_mix_post, g_ffn_pre, g_ffn_post, w_in, w_out,
           a_sink, b_rpb, c_lam_q1, c_lam_k1, c_lam_q2, c_lam_k2, c_subln, w_router, w_gate, w_up, w_down):
    b, s, d = x.shape
    lc = ctx.shape[1]
    depth = w_in.shape[0]

    cos_x, sin_x = _rope_tables(s)
    cos_c, sin_c = _ctx_tables(lc)

    pad = (-(b + 1)) % 8
    cstack = jnp.concatenate([c, c_ctx[None, :], jnp.zeros((pad, d), F32)], axis=0)
    mod = _modulation(cstack, w_ada, b_ada)

    n_exp, ff = w_gate.shape[1], w_gate.shape[3]
    wg = w_gate.reshape(depth * n_exp, d, ff)
    wu = w_up.reshape(depth * n_exp, d, ff)
    wd = w_down.reshape(depth * n_exp, ff, d)

    tm = 512
    h_ctx = ctx
    for l in range(depth):
        need_ctx = l < depth - 1
        mx = mod[l, :b].reshape(b, 1, N_MOD, d)
        sh1, sc1, gt1, sh2, sc2, gt2 = [mx[:, :, k] for k in range(N_MOD)]
        mc = mod[l, b:b + 1].reshape(1, 1, N_MOD, d)
        csh1, csc1, cgt1, csh2, csc2, cgt2 = [mc[:, :, k] for k in range(N_MOD)]

        w_cat = _inproj_weights(w_in[l])
        w_o = _outproj_weights(w_out[l])
        lam_init = 0.8 - 0.6 * math.exp(-0.3 * l)
        lam = (jnp.exp(jnp.sum(c_lam_q1[l] * c_lam_k1[l])) - jnp.exp(jnp.sum(c_lam_q2[l] * c_lam_k2[l]))
               + lam_init).reshape(1).astype(F32)
        subln2 = jnp.tile(c_subln[l], 2).reshape(1, LANES)
        bias = _b_bias_tables(b_rpb[l], s // GRID_W, LOG2E)
        sink = a_sink[l] * LOG2E

        p = _inproj(x, g_mix_pre[l], sc1, sh1, cos_x, sin_x, w_cat, tm)
        pc = _inproj(h_ctx, g_mix_pre[l], csc1, csh1, cos_c, sin_c, w_cat, lc)

        oa = _attn_a(p, pc, sink, True)
        ob = _attn_b(p, pc, bias)
        od = _attn_c(p, pc, lam, subln2, lam_init, True)
        x1, h2 = _outproj(oa, ob, od, x, w_o, g_mix_post[l], gt1, g_ffn_pre[l], sc2, sh2, tm)
        if need_ctx:
            oac = _attn_a(p, pc, sink, False)
            obc = _attn_b_ctx(pc)
            odc = _attn_c(p, pc, lam, subln2, lam_init, False)
            c1, hc2 = _outproj(oac, obc, odc, h_ctx, w_o, g_mix_post[l], cgt1, g_ffn_pre[l], csc2, csh2, lc)
            y, yc = _moe([h2, hc2], w_router[l], wg, wu, wd, l)
            h_ctx = _resid(c1, yc, g_ffn_post[l], cgt2, lc)
        else:
            (y,) = _moe([h2], w_router[l], wg, wu, wd, l)
        x = _resid(x1, y, g_ffn_post[l], gt2, tm)
    return x
```

```python
import functools
import math

import numpy as np
import jax
import jax.numpy as jnp
from jax import lax
from jax.experimental import pallas as pl
from jax.experimental.pallas import tpu as pltpu

F32 = jnp.float32
BF16 = jnp.bfloat16

GRID_W = 64
HEAD_DIM = 64
A_Q_HEADS, A_KV_HEADS = 6, 2
A_Q_PER_KV = A_Q_HEADS // A_KV_HEADS
A_WINDOW = 128
A_BLOCK = 128
B_HEADS, B_MAX_ROWS, B_COLS = 6, 8, 16
C_HEADS, C_QK_DIM, C_V_DIM = 4, 32, 64
N_EXPERTS = 16
CAPACITY_FACTOR = 2
N_MOD = 6
ROPE_THETA = 10000.0
EPS = 1e-6
NEG_INF = -1e30
LOG2E = math.log2(math.e)
QK_SCALE_64 = HEAD_DIM ** -0.5 * LOG2E
QK_SCALE_32 = C_QK_DIM ** -0.5 * LOG2E

LANES = 128
ROPE_W = 1024
OFF_AQ, OFF_AK, OFF_CQ, OFF_CK, OFF_AV, OFF_BQ, OFF_BK, OFF_BV, OFF_CV = (
    0, 384, 512, 768, 1024, 1152, 1536, 1920, 2304)
IN_W = 2560
VMEM_LIMIT = 56 * 1024 * 1024


def _cparams(sem, vmem_limit=VMEM_LIMIT):
    return pltpu.CompilerParams(dimension_semantics=sem, vmem_limit_bytes=vmem_limit)


def _dot(a, b):
    return jnp.dot(a, b, preferred_element_type=F32)


def _dot_nt(a, b):
    return lax.dot_general(a, b, (((1,), (1,)), ((), ())), preferred_element_type=F32)


def _rms(x, g):
    return x * lax.rsqrt(jnp.mean(x * x, axis=-1, keepdims=True) + EPS) * g


def _mod_kernel(c_ref, w_ref, b_ref, o_ref):
    c = c_ref[...]
    a = c / (1.0 + jnp.exp(-c))
    o_ref[0] = jnp.dot(a, w_ref[0], preferred_element_type=F32,
                       precision=lax.Precision.HIGHEST) + b_ref[0]


def _modulation(cstack, w_ada, b_ada):
    nl, d, n = w_ada.shape
    rows = cstack.shape[0]
    tn = 1536
    return pl.pallas_call(
        _mod_kernel,
        out_shape=jax.ShapeDtypeStruct((nl, rows, n), F32),
        grid=(nl, n // tn),
        in_specs=[
            pl.BlockSpec((rows, d), lambda l, j: (0, 0)),
            pl.BlockSpec((1, d, tn), lambda l, j: (l, 0, j)),
            pl.BlockSpec((1, 1, tn), lambda l, j: (l, 0, j)),
        ],
        out_specs=pl.BlockSpec((1, rows, tn), lambda l, j: (l, 0, j)),
        compiler_params=_cparams(("parallel", "parallel")),
        name="adaln_modulation",
    )(cstack, w_ada, b_ada.reshape(nl, 1, n))


_ROPE_TABLE_OF_UNIT = (0, 0, 0, 1, 2, 2, 3, 3)


_ROPE_HALF_OF_CHUNK = (HEAD_DIM // 4, HEAD_DIM // 4, C_QK_DIM // 4, C_QK_DIM // 4)


def _inproj_kernel(x_ref, g_ref, sc_ref, sh_ref, cos_ref, sin_ref, w_ref, o_ref):
    x = x_ref[0]
    h = _rms(x, g_ref[...]) * (1.0 + sc_ref[0]) + sh_ref[0]
    hb = h.astype(BF16)
    for j in range(ROPE_W // 256):
        t0, t1 = _ROPE_TABLE_OF_UNIT[2 * j], _ROPE_TABLE_OF_UNIT[2 * j + 1]
        cos = jnp.concatenate([cos_ref[:, t0 * LANES:(t0 + 1) * LANES],
                               cos_ref[:, t1 * LANES:(t1 + 1) * LANES]], axis=1)
        sin = jnp.concatenate([sin_ref[:, t0 * LANES:(t0 + 1) * LANES],
                               sin_ref[:, t1 * LANES:(t1 + 1) * LANES]], axis=1)
        p = _dot(hb, w_ref[:, 256 * j:256 * (j + 1)])
        half = _ROPE_HALF_OF_CHUNK[j]
        lane = lax.broadcasted_iota(jnp.int32, (1, 256), 1)
        ps = jnp.where((lane & (2 * half - 1)) < half, pltpu.roll(p, 256 - half, 1), pltpu.roll(p, half, 1))
        o_ref[0, :, 256 * j:256 * (j + 1)] = (p * cos + ps * sin).astype(BF16)
    for j in range(ROPE_W // 256, IN_W // 256):
        o_ref[0, :, 256 * j:256 * (j + 1)] = _dot(hb, w_ref[:, 256 * j:256 * (j + 1)]).astype(BF16)


def _inproj(x, g, sc, sh, cos, sin, w_cat, tm):
    b, s, d = x.shape
    per_sample = sc.shape[0] > 1
    mod_map = (lambda bi, i: (bi, 0, 0)) if per_sample else (lambda bi, i: (0, 0, 0))
    return pl.pallas_call(
        _inproj_kernel,
        out_shape=jax.ShapeDtypeStruct((b, s, IN_W), BF16),
        grid=(b, s // tm),
        in_specs=[
            pl.BlockSpec((1, tm, d), lambda bi, i: (bi, i, 0)),
            pl.BlockSpec((1, d), lambda bi, i: (0, 0)),
            pl.BlockSpec((1, 1, d), mod_map),
            pl.BlockSpec((1, 1, d), mod_map),
            pl.BlockSpec((tm, 4 * LANES), lambda bi, i: (i, 0)),
            pl.BlockSpec((tm, 4 * LANES), lambda bi, i: (i, 0)),
            pl.BlockSpec((d, IN_W), lambda bi, i: (0, 0)),
        ],
        out_specs=pl.BlockSpec((1, tm, IN_W), lambda bi, i: (bi, i, 0)),
        compiler_params=_cparams(("parallel", "parallel")),
        name="norm_inproj_rope",
    )(x, g.reshape(1, d), sc, sh, cos, sin, w_cat)


def _lo_mask():
    return lax.broadcasted_iota(jnp.int32, (1, LANES), 1) < HEAD_DIM


def _split_halves(q):
    lo = _lo_mask()
    zero = jnp.zeros_like(q)
    return jnp.where(lo, q, zero), jnp.where(lo, zero, q)


A_STEP_BLOCKS = 4


def _attn_a_kernel(sink_ref, q_ref, *refs, seq, latent):
    if latent:
        k_ref, v_ref, kc_ref, vc_ref, o_ref = refs
    else:
        kc_ref, vc_ref, o_ref = refs
    tq = q_ref.shape[1]
    npair = A_Q_HEADS // 2
    kc = kc_ref[0]
    vc = vc_ref[0]
    lo = _lo_mask()
    if latent:
        i = pl.program_id(1)
        band = tq + 2 * A_BLOCK
        start = pl.multiple_of(jnp.clip(i * tq - A_BLOCK, 0, seq - band), A_BLOCK)
        kb = k_ref[0, pl.ds(start, band), :]
        vb = v_ref[0, pl.ds(start, band), :]
        delta = ((start - i * tq) + lax.broadcasted_iota(jnp.int32, (1, band), 1)
                 - lax.broadcasted_iota(jnp.int32, (tq, 1), 0))
        wmask = jnp.where(jnp.abs(delta) <= A_WINDOW, 0.0, NEG_INF)
        wmask = jnp.concatenate([wmask, wmask], axis=0)
    outs = []
    for t in range(npair):
        qa, qb = _split_halves(q_ref[0, :, t * LANES:(t + 1) * LANES])
        q2 = jnp.concatenate([qa, qb], axis=0)
        sink = jnp.concatenate([jnp.full((tq, 1), sink_ref[t], F32),
                                jnp.full((tq, 1), sink_ref[npair + t], F32)], axis=0)
        s_ctx = _dot_nt(q2, kc)
        m = jnp.maximum(jnp.max(s_ctx, axis=-1, keepdims=True), sink)
        if latent:
            s_lat = _dot_nt(q2, kb) + wmask
            m = jnp.maximum(m, jnp.max(s_lat, axis=-1, keepdims=True))
            e_lat = jnp.exp2(s_lat - m)
        e_ctx = jnp.exp2(s_ctx - m)
        l = jnp.sum(e_ctx, axis=-1, keepdims=True) + jnp.exp2(sink - m)
        o = _dot(e_ctx.astype(BF16), vc)
        if latent:
            l = l + jnp.sum(e_lat, axis=-1, keepdims=True)
            o = o + _dot(e_lat.astype(BF16), vb)
        o = o / l
        outs.append(jnp.where(lo, o[:tq], o[tq:]))
    o_ref[0] = jnp.concatenate(outs, axis=1).astype(BF16)


def _attn_a(p, pc, sink, latent):
    src = p if latent else pc
    b, s, _ = src.shape
    lc = pc.shape[1]
    tq = A_STEP_BLOCKS * A_BLOCK if latent else s
    wq = A_Q_HEADS * HEAD_DIM
    in_specs = [pl.BlockSpec(memory_space=pltpu.SMEM),
                pl.BlockSpec((1, tq, wq), lambda bi, i: (bi, i, OFF_AQ // wq))]
    args = [sink, src]
    if latent:
        in_specs += [pl.BlockSpec((1, s, LANES), lambda bi, i: (bi, 0, OFF_AK // LANES)),
                     pl.BlockSpec((1, s, LANES), lambda bi, i: (bi, 0, OFF_AV // LANES))]
        args += [p, p]
    in_specs += [pl.BlockSpec((1, lc, LANES), lambda bi, i: (bi, 0, OFF_AK // LANES)),
                 pl.BlockSpec((1, lc, LANES), lambda bi, i: (bi, 0, OFF_AV // LANES))]
    args += [pc, pc]
    return pl.pallas_call(
        functools.partial(_attn_a_kernel, seq=s, latent=latent),
        out_shape=jax.ShapeDtypeStruct((b, s, wq), BF16),
        grid=(b, s // tq),
        in_specs=in_specs,
        out_specs=pl.BlockSpec((1, tq, wq), lambda bi, i: (bi, i, 0)),
        compiler_params=_cparams(("parallel", "arbitrary")),
        name="attn_window_gqa" + ("" if latent else "_ctx"),
    )(*args)


B_GROUP_ROWS = 4
B_UNION_ROWS = B_GROUP_ROWS + B_MAX_ROWS


def _attn_b_softmax_out(s_parts, v_parts):
    m = functools.reduce(jnp.maximum, [jnp.max(s, axis=-1, keepdims=True) for s in s_parts])
    es = [jnp.exp2(s - m) for s in s_parts]
    l = functools.reduce(jnp.add, [jnp.sum(e, axis=-1, keepdims=True) for e in es])
    o = functools.reduce(jnp.add, [_dot(e.astype(BF16), v) for e, v in zip(es, v_parts)])
    return o / l


def _attn_b_kernel(q_ref, k_ref, v_ref, kc_ref, vc_ref, bias_ref, o_ref, *, rows_n):
    i = pl.program_id(1)
    npair = B_HEADS // 2
    lo = _lo_mask()
    tq = q_ref.shape[1]
    nk = B_UNION_ROWS * GRID_W
    us = jnp.clip(i * B_GROUP_ROWS - B_MAX_ROWS // 2, 0, rows_n - B_UNION_ROWS)
    kstart = pl.multiple_of(us * GRID_W, GRID_W)
    outs = []
    for t in range(npair):
        cols = slice(t * LANES, (t + 1) * LANES)
        qa, qb = _split_halves(q_ref[0, :, cols])
        q2 = jnp.concatenate([qa, qb], axis=0)
        s_lat = _dot_nt(q2, k_ref[0, pl.ds(kstart, nk), cols]) + bias_ref[0, t]
        s_ctx = _dot_nt(q2, kc_ref[0, :, cols])
        o = _attn_b_softmax_out([s_lat, s_ctx], [v_ref[0, pl.ds(kstart, nk), cols], vc_ref[0, :, cols]])
        outs.append(jnp.where(lo, o[:tq], o[tq:]))
    o_ref[0] = jnp.concatenate(outs, axis=1).astype(BF16)


def _attn_b_ctx_kernel(q_ref, kc_ref, vc_ref, o_ref):
    npair = B_HEADS // 2
    lo = _lo_mask()
    n = q_ref.shape[1]
    for t in range(npair):
        cols = slice(t * LANES, (t + 1) * LANES)
        qa, qb = _split_halves(q_ref[0, :, cols])
        q2 = jnp.concatenate([qa, qb], axis=0)
        s_ctx = _dot_nt(q2, kc_ref[0, :, cols])
        o = _attn_b_softmax_out([s_ctx], [vc_ref[0, :, cols]])
        o_ref[0, :, cols] = jnp.where(lo, o[:n], o[n:]).astype(BF16)


def _attn_b(p, pc, bias):
    b, s, _ = p.shape
    lc = pc.shape[1]
    w = B_HEADS * HEAD_DIM
    rows_n = s // GRID_W
    tq = B_GROUP_ROWS * GRID_W
    ng = s // tq
    variant = lambda bi, i: (jnp.minimum(i, 1) + (i == ng - 1).astype(jnp.int32), 0, 0, 0)
    return pl.pallas_call(
        functools.partial(_attn_b_kernel, rows_n=rows_n),
        out_shape=jax.ShapeDtypeStruct((b, s, w), BF16),
        grid=(b, ng),
        in_specs=[
            pl.BlockSpec((1, tq, w), lambda bi, i: (bi, i, OFF_BQ // w)),
            pl.BlockSpec((1, s, w), lambda bi, i: (bi, 0, OFF_BK // w)),
            pl.BlockSpec((1, s, w), lambda bi, i: (bi, 0, OFF_BV // w)),
            pl.BlockSpec((1, lc, w), lambda bi, i: (bi, 0, OFF_BK // w)),
            pl.BlockSpec((1, lc, w), lambda bi, i: (bi, 0, OFF_BV // w)),
            pl.BlockSpec((1,) + bias.shape[1:], variant),
        ],
        out_specs=pl.BlockSpec((1, tq, w), lambda bi, i: (bi, i, 0)),
        compiler_params=_cparams(("parallel", "arbitrary")),
        name="attn_neighbourhood",
    )(p, p, p, pc, pc, bias)


def _attn_b_ctx(pc):
    b, lc, _ = pc.shape
    w = B_HEADS * HEAD_DIM
    return pl.pallas_call(
        _attn_b_ctx_kernel,
        out_shape=jax.ShapeDtypeStruct((b, lc, w), BF16),
        grid=(b,),
        in_specs=[
            pl.BlockSpec((1, lc, w), lambda bi: (bi, 0, OFF_BQ // w)),
            pl.BlockSpec((1, lc, w), lambda bi: (bi, 0, OFF_BK // w)),
            pl.BlockSpec((1, lc, w), lambda bi: (bi, 0, OFF_BV // w)),
        ],
        out_specs=pl.BlockSpec((1, lc, w), lambda bi: (bi, 0, 0)),
        compiler_params=_cparams(("parallel",)),
        name="attn_neighbourhood_ctx",
    )(pc, pc, pc)


def _b_bias_tables(rpb, rows_n, scale):
    g_rows, u_rows = B_GROUP_ROWS, B_UNION_ROWS
    col = np.arange(GRID_W)
    cstart = np.clip(col - B_COLS // 2, 0, GRID_W - B_COLS)
    col_ok = (col[None, :] >= cstart[:, None]) & (col[None, :] < cstart[:, None] + B_COLS)
    dc_idx = np.clip(col[None, :] - col[:, None], -(B_COLS - 1), B_COLS - 1) + (B_COLS - 1)
    pick_col = (dc_idx[None] == np.arange(2 * B_COLS - 1)[:, None, None]).astype(np.float32)
    n_groups = rows_n // g_rows
    patterns, group_variant = [], []
    for g in range(n_groups):
        us = np.clip(g * g_rows - B_MAX_ROWS // 2, 0, rows_n - u_rows)
        r = g * g_rows + np.arange(g_rows)[:, None]
        rs = np.clip(r - B_MAX_ROWS // 2, 0, rows_n - B_MAX_ROWS)
        key_row = us + np.arange(u_rows)[None, :]
        dr = np.where((key_row >= rs) & (key_row < rs + B_MAX_ROWS), key_row - r + (B_MAX_ROWS - 1), -1)
        if not any(np.array_equal(dr, p_) for p_ in patterns):
            patterns.append(dr)
        group_variant.append([np.array_equal(dr, p_) for p_ in patterns].index(True))
    assert group_variant == [0] + [1] * (n_groups - 2) + [2], group_variant
    dr = np.stack(patterns)
    pick_row = (dr[..., None] == np.arange(2 * B_MAX_ROWS - 1)).astype(np.float32)
    hi = lax.Precision.HIGHEST
    rsel = jnp.einsum('vuwd,hdc->vhuwc', pick_row, rpb.astype(F32), precision=hi)
    t = jnp.einsum('vhuwc,cqk->vhuqwk', rsel, pick_col, precision=hi) * scale
    ok = (dr >= 0)[:, None, :, None, :, None] & col_ok[None, None, None, :, None, :]
    t = jnp.where(ok, t, NEG_INF)
    return t.reshape(len(patterns), B_HEADS // 2, 2 * g_rows * GRID_W, u_rows * GRID_W)


C_KEY_CHUNK = 2048
C_Q_TILE = 256


def _attn_c_kernel(lam_ref, q_ref, *refs, latent, out_scale):
    if latent:
        k_ref, v_ref, kc_ref, vc_ref, g_ref, o_ref = refs
    else:
        kc_ref, vc_ref, g_ref, o_ref = refs
    tq = q_ref.shape[1]
    lam = lam_ref[0]
    q = q_ref[0]
    quarter = lax.broadcasted_iota(jnp.int32, (1, LANES), 1) // C_QK_DIM
    zero = jnp.zeros_like(q)
    q4 = jnp.concatenate([jnp.where(quarter == j, q, zero) for j in range(4)], axis=0)
    chunks = []
    if latent:
        kc_n = min(C_KEY_CHUNK, k_ref.shape[1])
        chunks += [(k_ref, v_ref, c * kc_n, kc_n) for c in range(k_ref.shape[1] // kc_n)]
    chunks.append((kc_ref, vc_ref, 0, kc_ref.shape[1]))
    m = l = acc = None
    for kr, vr, st, n in chunks:
        s = _dot_nt(q4, kr[0, st:st + n, :])
        mc = jnp.max(s, axis=-1, keepdims=True)
        if m is None:
            m = mc
            e = jnp.exp2(s - m)
            l = jnp.sum(e, axis=-1, keepdims=True)
            acc = _dot(e.astype(BF16), vr[0, st:st + n, :])
        else:
            m_new = jnp.maximum(m, mc)
            alpha = jnp.exp2(m - m_new)
            e = jnp.exp2(s - m_new)
            l = l * alpha + jnp.sum(e, axis=-1, keepdims=True)
            acc = acc * alpha + _dot(e.astype(BF16), vr[0, st:st + n, :])
            m = m_new
    o4 = acc / l
    outs = [o4[2 * h * tq:(2 * h + 1) * tq] - lam * o4[(2 * h + 1) * tq:(2 * h + 2) * tq]
            for h in range(2)]
    lo = _lo_mask()
    o = jnp.where(lo, outs[0], outs[1])
    sq = o * o
    s_lo = jnp.sum(jnp.where(lo, sq, 0.0), axis=-1, keepdims=True)
    s_hi = jnp.sum(jnp.where(lo, 0.0, sq), axis=-1, keepdims=True)
    ms = jnp.where(lo, s_lo, s_hi) * (1.0 / C_V_DIM)
    o_ref[0] = (o * lax.rsqrt(ms + EPS) * g_ref[...] * out_scale).astype(BF16)


def _attn_c(p, pc, lam, subln2, lam_init, latent):
    src = p if latent else pc
    b, s, _ = src.shape
    lc = pc.shape[1]
    tq = C_Q_TILE if latent else s
    npair = C_HEADS // 2
    in_specs = [pl.BlockSpec(memory_space=pltpu.SMEM),
                pl.BlockSpec((1, tq, LANES), lambda bi, hp, i: (bi, i, OFF_CQ // LANES + hp))]
    args = [lam, src]
    if latent:
        in_specs += [pl.BlockSpec((1, s, LANES), lambda bi, hp, i: (bi, 0, OFF_CK // LANES + hp)),
                     pl.BlockSpec((1, s, LANES), lambda bi, hp, i: (bi, 0, OFF_CV // LANES + hp))]
        args += [p, p]
    in_specs += [pl.BlockSpec((1, lc, LANES), lambda bi, hp, i: (bi, 0, OFF_CK // LANES + hp)),
                 pl.BlockSpec((1, lc, LANES), lambda bi, hp, i: (bi, 0, OFF_CV // LANES + hp)),
                 pl.BlockSpec((1, LANES), lambda bi, hp, i: (0, 0))]
    args += [pc, pc, subln2]
    return pl.pallas_call(
        functools.partial(_attn_c_kernel, latent=latent, out_scale=1.0 - lam_init),
        out_shape=jax.ShapeDtypeStruct((b, s, C_HEADS * C_V_DIM), BF16),
        grid=(b, npair, s // tq),
        in_specs=in_specs,
        out_specs=pl.BlockSpec((1, tq, LANES), lambda bi, hp, i: (bi, i, hp)),
        compiler_params=_cparams(("parallel", "parallel", "arbitrary")),
        name="attn_differential" + ("" if latent else "_ctx"),
    )(*args)


def _outproj_kernel(oa_ref, ob_ref, oc_ref, x_ref, w_ref, gpost_ref, gt_ref, gpre_ref, sc_ref, sh_ref,
                    x1_ref, h2_ref):
    wa = oa_ref.shape[2]
    wb = ob_ref.shape[2]
    y = (_dot(oa_ref[0], w_ref[0:wa, :]) + _dot(ob_ref[0], w_ref[wa:wa + wb, :])
         + _dot(oc_ref[0], w_ref[wa + wb:, :]))
    x1 = x_ref[0] + gt_ref[0] * _rms(y, gpost_ref[...])
    x1_ref[0] = x1
    h2_ref[0] = _rms(x1, gpre_ref[...]) * (1.0 + sc_ref[0]) + sh_ref[0]


def _outproj(oa, ob, oc, x, w_out, g_post, gt, g_pre, sc, sh, tm):
    b, s, d = x.shape
    per_sample = gt.shape[0] > 1
    mod_map = (lambda bi, i: (bi, 0, 0)) if per_sample else (lambda bi, i: (0, 0, 0))
    row = lambda bi, i: (bi, i, 0)
    const2 = lambda bi, i: (0, 0)
    return pl.pallas_call(
        _outproj_kernel,
        out_shape=(jax.ShapeDtypeStruct((b, s, d), F32), jax.ShapeDtypeStruct((b, s, d), F32)),
        grid=(b, s // tm),
        in_specs=[
            pl.BlockSpec((1, tm, oa.shape[2]), row),
            pl.BlockSpec((1, tm, ob.shape[2]), row),
            pl.BlockSpec((1, tm, oc.shape[2]), row),
            pl.BlockSpec((1, tm, d), row),
            pl.BlockSpec(w_out.shape, const2),
            pl.BlockSpec((1, d), const2),
            pl.BlockSpec((1, 1, d), mod_map),
            pl.BlockSpec((1, d), const2),
            pl.BlockSpec((1, 1, d), mod_map),
            pl.BlockSpec((1, 1, d), mod_map),
        ],
        out_specs=(pl.BlockSpec((1, tm, d), row), pl.BlockSpec((1, tm, d), row)),
        compiler_params=_cparams(("parallel", "parallel")),
        name="outproj_residual_norm",
    )(oa, ob, oc, x, w_out, g_post.reshape(1, d), gt, g_pre.reshape(1, d), sc, sh)


ROW_UNROLL = 8


GATHER_ROWS = 16
MOVE_ROWS_PER_STEP = 512


def _experts_per_step(n_experts, cap):
    return min(n_experts, max(1, MOVE_ROWS_PER_STEP // cap))


def _gather_kernel(idx_ref, h_ref, o_ref):
    cap, d = o_ref.shape[2], o_ref.shape[3]
    sub = lax.broadcasted_iota(jnp.int32, (ROW_UNROLL, d), 0)

    for k in range(o_ref.shape[0]):
        def body(j, carry, k=k):
            base = pl.multiple_of(j * GATHER_ROWS, GATHER_ROWS)
            halves = []
            for g in range(GATHER_ROWS // ROW_UNROLL):
                tile = jnp.zeros((ROW_UNROLL, d), F32)
                for u in range(ROW_UNROLL):
                    n = idx_ref[k, 0, base + g * ROW_UNROLL + u]
                    row = jnp.broadcast_to(h_ref[0, pl.ds(n, 1), :], (ROW_UNROLL, d))
                    tile = jnp.where(sub == u, row, tile)
                halves.append(tile)
            o_ref[k, 0, pl.ds(base, GATHER_ROWS), :] = jnp.concatenate(halves, axis=0).astype(BF16)
            return carry

        lax.fori_loop(0, cap // GATHER_ROWS, body, 0)


def _moe_gather(h2, idx):
    b, n, d = h2.shape
    e, cap = idx.shape[1], idx.shape[2]
    es = _experts_per_step(e, cap)
    return pl.pallas_call(
        _gather_kernel,
        out_shape=jax.ShapeDtypeStruct((e, b, cap, d), BF16),
        grid=(b, e // es),
        in_specs=[
            pl.BlockSpec((es, 1, cap), lambda bi, ei: (bi * (e // es) + ei, 0, 0), memory_space=pltpu.SMEM),
            pl.BlockSpec((1, n, d), lambda bi, ei: (bi, 0, 0)),
        ],
        out_specs=pl.BlockSpec((es, 1, cap, d), lambda bi, ei: (ei, bi, 0, 0)),
        compiler_params=_cparams(("parallel", "arbitrary")),
        name="moe_gather",
    )(idx.reshape(b * e, 1, cap), h2)


FF_CHUNK = 512
CAST_ROWS = 256


def _ffn_kernel(*refs, tiles, first_expert):
    ns = len(tiles)
    xs_refs, gate_refs = refs[0:2 * ns:2], refs[1:2 * ns:2]
    wg_hbm, wu_hbm, wd_hbm = refs[2 * ns:2 * ns + 3]
    o_refs = refs[2 * ns + 3:3 * ns + 3]
    stage_g, stage_u, stage_d, wg_ref, wu_ref, wd_ref, sems = refs[3 * ns + 3:]
    e = pl.program_id(0)
    r = pl.program_id(1)

    def weight_copies(expert):
        row = first_expert + expert
        return [pltpu.make_async_copy(wg_hbm.at[row], stage_g, sems.at[0]),
                pltpu.make_async_copy(wu_hbm.at[row], stage_u, sems.at[1]),
                pltpu.make_async_copy(wd_hbm.at[row], stage_d, sems.at[2])]

    @pl.when(r == 0)
    def _():
        @pl.when(e == 0)
        def _():
            for c in weight_copies(0):
                c.start()

        for c in weight_copies(e):
            c.wait()
        for src, dst in ((stage_g, wg_ref), (stage_u, wu_ref), (stage_d, wd_ref)):
            for r0 in range(0, src.shape[0], CAST_ROWS):
                dst[r0:r0 + CAST_ROWS, :] = src[r0:r0 + CAST_ROWS, :].astype(BF16)

        @pl.when(e + 1 < pl.num_programs(0))
        def _():
            for c in weight_copies(e + 1):
                c.start()

    ff = wg_ref.shape[1]

    def swiglu(xs_ref, gate_ref, o_ref):
        xs = xs_ref[0]
        for c in range(ff // FF_CHUNK):
            cs = slice(c * FF_CHUNK, (c + 1) * FF_CHUNK)
            g = _dot(xs, wg_ref[:, cs])
            u = _dot(xs, wu_ref[:, cs])
            hid = ((g / (1.0 + jnp.exp(-g))) * u).astype(BF16)
            part = _dot(hid, wd_ref[cs, :])
            if c == 0:
                o_ref[0] = part
            elif c < ff // FF_CHUNK - 1:
                o_ref[0] += part
            else:
                o_ref[0] = (o_ref[0] + part) * gate_ref[0]

    first = 0
    for k, nt in enumerate(tiles):
        pl.when(jnp.logical_and(r >= first, r < first + nt))(
            functools.partial(swiglu, xs_refs[k], gate_refs[k], o_refs[k]))
        first += nt


FFN_ROWS = 512
FFN_VMEM_LIMIT = 61 * 1024 * 1024


def _moe_ffn(xs_sets, gate_sets, wg, wu, wd, layer):
    e, _, d = xs_sets[0].shape
    ff = wg.shape[2]
    in_specs, out_specs, out_shape, tiles, args = [], [], [], [], []
    first = 0
    for xs, gates in zip(xs_sets, gate_sets):
        r = xs.shape[1]
        tr = min(r, FFN_ROWS)
        nt = r // tr
        row_map = functools.partial(lambda ei, ri, first, nt: (ei, jnp.clip(ri - first, 0, nt - 1), 0),
                                    first=first, nt=nt)
        in_specs += [pl.BlockSpec((1, tr, d), row_map), pl.BlockSpec((1, tr, 1), row_map)]
        out_specs.append(pl.BlockSpec((1, tr, d), row_map))
        out_shape.append(jax.ShapeDtypeStruct((e, r, d), F32))
        args += [xs, gates]
        tiles.append(nt)
        first += nt
    in_specs += [pl.BlockSpec(memory_space=pl.ANY)] * 3
    return pl.pallas_call(
        functools.partial(_ffn_kernel, tiles=tuple(tiles), first_expert=layer * e),
        out_shape=out_shape,
        grid=(e, first),
        in_specs=in_specs,
        out_specs=out_specs,
        scratch_shapes=[pltpu.VMEM((d, ff), F32), pltpu.VMEM((d, ff), F32), pltpu.VMEM((ff, d), F32),
                        pltpu.VMEM((d, ff), BF16), pltpu.VMEM((d, ff), BF16), pltpu.VMEM((ff, d), BF16),
                        pltpu.SemaphoreType.DMA((3,))],
        compiler_params=_cparams(("arbitrary", "arbitrary"), FFN_VMEM_LIMIT),
        name="moe_expert_ffn",
    )(*args, wg, wu, wd)


def _combine_kernel(idx_ref, y_ref, o_ref):
    cap = y_ref.shape[2]

    @pl.when(pl.program_id(1) == 0)
    def _():
        o_ref[...] = jnp.zeros_like(o_ref)

    for k in range(y_ref.shape[0]):
        def body(j, carry, k=k):
            base = pl.multiple_of(j * ROW_UNROLL, ROW_UNROLL)
            toks = [idx_ref[k, 0, base + u] for u in range(ROW_UNROLL)]
            ys = y_ref[k, 0, pl.ds(base, ROW_UNROLL), :]
            rows = [o_ref[0, pl.ds(n, 1), :] for n in toks]
            for u, n in enumerate(toks):
                o_ref[0, pl.ds(n, 1), :] = rows[u] + ys[u:u + 1]
            return carry

        lax.fori_loop(0, cap // ROW_UNROLL, body, 0)


def _moe_combine(y, idx, n):
    e, b, cap, d = y.shape
    es = _experts_per_step(e, cap)
    return pl.pallas_call(
        _combine_kernel,
        out_shape=jax.ShapeDtypeStruct((b, n, d), F32),
        grid=(b, e // es),
        in_specs=[
            pl.BlockSpec((es, 1, cap), lambda bi, ei: (bi * (e // es) + ei, 0, 0), memory_space=pltpu.SMEM),
            pl.BlockSpec((es, 1, cap, d), lambda bi, ei: (ei, bi, 0, 0)),
        ],
        out_specs=pl.BlockSpec((1, n, d), lambda bi, ei: (bi, 0, 0)),
        compiler_params=_cparams(("parallel", "arbitrary")),
        name="moe_combine",
    )(idx.reshape(b * e, 1, cap), y)


def _resid_kernel(x_ref, y_ref, g_ref, gt_ref, o_ref):
    o_ref[0] = x_ref[0] + gt_ref[0] * _rms(y_ref[0], g_ref[...])


def _resid(x, y, g, gt, tm):
    b, s, d = x.shape
    per_sample = gt.shape[0] > 1
    mod_map = (lambda bi, i: (bi, 0, 0)) if per_sample else (lambda bi, i: (0, 0, 0))
    row = lambda bi, i: (bi, i, 0)
    return pl.pallas_call(
        _resid_kernel,
        out_shape=jax.ShapeDtypeStruct((b, s, d), F32),
        grid=(b, s // tm),
        in_specs=[pl.BlockSpec((1, tm, d), row), pl.BlockSpec((1, tm, d), row),
                  pl.BlockSpec((1, d), lambda bi, i: (0, 0)), pl.BlockSpec((1, 1, d), mod_map)],
        out_specs=pl.BlockSpec((1, tm, d), row),
        compiler_params=_cparams(("parallel", "parallel")),
        name="ffn_residual",
    )(x, y, g.reshape(1, d), gt)


ROUTE_SLOT_LO = 32
ROUTE_TOK_SPLIT = 64


def _count(mask):
    return jnp.sum(jnp.where(mask, 1.0, 0.0), axis=1, keepdims=True)


def _route_kernel(h_ref, wr_ref, idx_ref, gate_ref, logit_ref, aff_ref, posm_ref, *, cap, tn):
    i = pl.program_id(1)
    n_exp, n_tok = logit_ref.shape
    h = h_ref[0]
    h_hi = h.astype(BF16)
    h_lo = (h - h_hi.astype(F32)).astype(BF16)
    w = wr_ref[...]
    w_hi = w.astype(BF16)
    w_lo = (w - w_hi.astype(F32)).astype(BF16)
    logit_ref[:, pl.ds(pl.multiple_of(i * tn, tn), tn)] = (
        _dot_nt(w_hi, h_hi) + (_dot_nt(w_hi, h_lo) + _dot_nt(w_lo, h_hi)))

    @pl.when(i == pl.num_programs(1) - 1)
    def _():
        lg = logit_ref[...]
        ex = jnp.exp(lg - jnp.max(lg, axis=0, keepdims=True))
        aff = ex / jnp.sum(ex, axis=0, keepdims=True)
        aff_ref[...] = aff
        capf = float(cap)

        def tbody(it, t):
            cand = t | jnp.left_shift(jnp.int32(1), 30 - it)
            cnt = _count(aff >= lax.bitcast_convert_type(cand, F32))
            return jnp.where(cnt >= capf, cand, t)

        t = lax.fori_loop(0, 31, tbody, jnp.zeros((n_exp, 1), jnp.int32))
        above = aff >= lax.bitcast_convert_type(t + 1, F32)
        tied = jnp.logical_and(aff >= lax.bitcast_convert_type(t, F32), jnp.logical_not(above))
        need = capf - _count(above)
        tok = lax.broadcasted_iota(jnp.int32, (1, n_tok), 1)
        nbits = n_tok.bit_length()

        def mbody(it, bound):
            cand = bound | jnp.left_shift(jnp.int32(1), nbits - 1 - it)
            f = _count(jnp.logical_and(tied, tok < cand))
            return jnp.where(f <= need, cand, bound)

        bound = lax.fori_loop(0, nbits, mbody, jnp.zeros((n_exp, 1), jnp.int32))
        sel = jnp.logical_or(above, jnp.logical_and(tied, tok < bound))
        self = jnp.where(sel, 1.0, 0.0)
        csum = self
        shift = 1
        while shift < n_tok:
            csum = csum + jnp.where(tok >= shift, pltpu.roll(csum, shift, 1), 0.0)
            shift *= 2
        posm_ref[...] = jnp.where(sel, csum - self, -1.0)

        tok_hi = (tok // ROUTE_TOK_SPLIT).astype(F32)
        tok_lo = (tok % ROUTE_TOK_SPLIT).astype(F32)
        n_hi = cap // ROUTE_SLOT_LO
        hi_iota = lax.broadcasted_iota(jnp.int32, (n_hi, 1), 0)
        lo_iota = lax.broadcasted_iota(jnp.int32, (ROUTE_SLOT_LO, 1), 0)
        pad_rows = (-5 * n_hi) % 16
        zeros = [jnp.zeros((pad_rows, n_tok), F32)] if pad_rows else []

        def ebody(e, carry):
            slot = posm_ref[pl.ds(e, 1), :].astype(jnp.int32)
            in_hi = (slot // ROUTE_SLOT_LO) == hi_iota
            lo_hot = jnp.where((slot % ROUTE_SLOT_LO) == lo_iota, 1.0, 0.0).astype(BF16)
            a = aff_ref[pl.ds(e, 1), :]
            a_hi = a.astype(BF16).astype(F32)
            a_mid = (a - a_hi).astype(BF16).astype(F32)
            a_lo = (a - a_hi) - a_mid
            lhs = jnp.concatenate([jnp.where(in_hi, v, 0.0) for v in (tok_hi, tok_lo, a_hi, a_mid, a_lo)]
                                  + zeros, axis=0).astype(BF16)
            r = _dot_nt(lhs, lo_hot)
            idx_ref[0, e] = (r[0:n_hi] * float(ROUTE_TOK_SPLIT) + r[n_hi:2 * n_hi]).astype(jnp.int32)
            gate_ref[0, e] = r[2 * n_hi:3 * n_hi] + (r[3 * n_hi:4 * n_hi] + r[4 * n_hi:5 * n_hi])
            return carry

        lax.fori_loop(0, n_exp, ebody, 0)


def _route(h2, w_router, cap):
    b, n, d = h2.shape
    e = w_router.shape[1]
    tn = min(n, 1024)
    n_hi = cap // ROUTE_SLOT_LO
    return pl.pallas_call(
        functools.partial(_route_kernel, cap=cap, tn=tn),
        out_shape=(jax.ShapeDtypeStruct((b, e, n_hi, ROUTE_SLOT_LO), jnp.int32),
                   jax.ShapeDtypeStruct((b, e, n_hi, ROUTE_SLOT_LO), F32)),
        grid=(b, n // tn),
        in_specs=[pl.BlockSpec((1, tn, d), lambda bi, i: (bi, i, 0)),
                  pl.BlockSpec((e, d), lambda bi, i: (0, 0))],
        out_specs=(pl.BlockSpec((1, e, n_hi, ROUTE_SLOT_LO), lambda bi, i: (bi, 0, 0, 0)),
                   pl.BlockSpec((1, e, n_hi, ROUTE_SLOT_LO), lambda bi, i: (bi, 0, 0, 0))),
        scratch_shapes=[pltpu.VMEM((e, n), F32), pltpu.VMEM((e, n), F32), pltpu.VMEM((e, n), F32)],
        compiler_params=_cparams(("parallel", "arbitrary")),
        name="moe_route",
    )(h2, w_router.T)


def _moe(token_sets, w_router, wg, wu, wd, layer):
    e = w_router.shape[1]
    idxs, xs_sets, gate_sets = [], [], []
    for h in token_sets:
        b, n, d = h.shape
        cap = n * CAPACITY_FACTOR // e
        idx, gates = _route(h, w_router, cap)
        idx = idx.reshape(b, e, cap)
        idxs.append(idx)
        xs_sets.append(_moe_gather(h, idx).reshape(e, b * cap, d))
        gate_sets.append(gates.reshape(b, e, cap).transpose(1, 0, 2).reshape(e, b * cap, 1))
    ys = _moe_ffn(xs_sets, gate_sets, wg, wu, wd, layer)
    return [_moe_combine(y.reshape(e, idx.shape[0], idx.shape[2], h.shape[2]), idx, h.shape[1])
            for y, idx, h in zip(ys, idxs, token_sets)]


def _inproj_weights(w):
    d = w.shape[0]
    sec = np.cumsum([0, 384, 128, 128, 384, 384, 384, 256, 256, 256])
    aq, ak, av, bq, bk, bv, cq, ck, cv = [w[:, sec[i]:sec[i + 1]] for i in range(9)]
    aq = aq.reshape(d, A_KV_HEADS, A_Q_PER_KV, HEAD_DIM).transpose(0, 2, 1, 3).reshape(d, -1)
    return jnp.concatenate([aq, ak, cq, ck, av, bq * QK_SCALE_64, bk, bv, cv], axis=1).astype(BF16)


def _outproj_weights(w):
    d = w.shape[1]
    wa = A_Q_HEADS * HEAD_DIM
    a = w[:wa].reshape(A_KV_HEADS, A_Q_PER_KV, HEAD_DIM, d).transpose(1, 0, 2, 3).reshape(wa, d)
    return jnp.concatenate([a, w[wa:]], axis=0).astype(BF16)


def _rope_tables(s):
    t = jnp.arange(s)
    rows, cols = t // GRID_W, t % GRID_W

    def head_tables(d):
        q = d // 4
        inv = ROPE_THETA ** (-jnp.arange(q, dtype=F32) / q)
        cs, sn = [], []
        for pos in (rows, cols):
            ang = pos.astype(F32)[:, None] * inv[None, :]
            c, s_ = jnp.cos(ang), jnp.sin(ang)
            cs += [c, c]
            sn += [-s_, s_]
        c = jnp.concatenate(cs, axis=1)
        s_ = jnp.concatenate(sn, axis=1)
        reps = LANES // d
        return jnp.tile(c, (1, reps)), jnp.tile(s_, (1, reps))

    ca, sa = head_tables(HEAD_DIM)
    cc, sc = head_tables(C_QK_DIM)
    cos = jnp.concatenate([ca * QK_SCALE_64, ca, cc * QK_SCALE_32, cc], axis=1)
    sin = jnp.concatenate([sa * QK_SCALE_64, sa, sc * QK_SCALE_32, sc], axis=1)
    return cos, sin


def _ctx_tables(lc):
    ones = jnp.ones((lc, LANES), F32)
    cos = jnp.concatenate([ones * QK_SCALE_64, ones, ones * QK_SCALE_32, ones], axis=1)
    return cos, jnp.zeros_like(cos)


def kernel(x, c, ctx, c_ctx, w_ada, b_ada, g_mix_pre, g_mix_post, g_ffn_pre, g_ffn_post, w_in, w_out,
           a_sink, b_rpb, c_lam_q1, c_lam_k1, c_lam_q2, c_lam_k2, c_subln, w_router, w_gate, w_up, w_down):
    b, s, d = x.shape
    lc = ctx.shape[1]
    depth = w_in.shape[0]

    cos_x, sin_x = _rope_tables(s)
    cos_c, sin_c = _ctx_tables(lc)

    pad = (-(b + 1)) % 8
    cstack = jnp.concatenate([c, c_ctx[None, :], jnp.zeros((pad, d), F32)], axis=0)
    mod = _modulation(cstack, w_ada, b_ada)

    n_exp, ff = w_gate.shape[1], w_gate.shape[3]
    wg = w_gate.reshape(depth * n_exp, d, ff)
    wu = w_up.reshape(depth * n_exp, d, ff)
    wd = w_down.reshape(depth * n_exp, ff, d)

    tm = 1024
    h_ctx = ctx
    for l in range(depth):
        need_ctx = l < depth - 1
        mx = mod[l, :b].reshape(b, 1, N_MOD, d)
        sh1, sc1, gt1, sh2, sc2, gt2 = [mx[:, :, k] for k in range(N_MOD)]
        mc = mod[l, b:b + 1].reshape(1, 1, N_MOD, d)
        csh1, csc1, cgt1, csh2, csc2, cgt2 = [mc[:, :, k] for k in range(N_MOD)]

        w_cat = _inproj_weights(w_in[l])
        w_o = _outproj_weights(w_out[l])
        lam_init = 0.8 - 0.6 * math.exp(-0.3 * l)
        lam = (jnp.exp(jnp.sum(c_lam_q1[l] * c_lam_k1[l])) - jnp.exp(jnp.sum(c_lam_q2[l] * c_lam_k2[l]))
               + lam_init).reshape(1).astype(F32)
        subln2 = jnp.tile(c_subln[l], 2).reshape(1, LANES)
        bias = _b_bias_tables(b_rpb[l], s // GRID_W, LOG2E)
        sink = a_sink[l] * LOG2E

        p = _inproj(x, g_mix_pre[l], sc1, sh1, cos_x, sin_x, w_cat, tm)
        pc = _inproj(h_ctx, g_mix_pre[l], csc1, csh1, cos_c, sin_c, w_cat, lc)

        oa = _attn_a(p, pc, sink, True)
        ob = _attn_b(p, pc, bias)
        od = _attn_c(p, pc, lam, subln2, lam_init, True)
        x1, h2 = _outproj(oa, ob, od, x, w_o, g_mix_post[l], gt1, g_ffn_pre[l], sc2, sh2, tm)
        if need_ctx:
            oac = _attn_a(p, pc, sink, False)
            obc = _attn_b_ctx(pc)
            odc = _attn_c(p, pc, lam, subln2, lam_init, False)
            c1, hc2 = _outproj(oac, obc, odc, h_ctx, w_o, g_mix_post[l], cgt1, g_ffn_pre[l], csc2, csh2, lc)
            y, yc = _moe([h2, hc2], w_router[l], wg, wu, wd, l)
            h_ctx = _resid(c1, yc, g_ffn_post[l], cgt2, lc)
        else:
            (y,) = _moe([h2], w_router[l], wg, wu, wd, l)
        x = _resid(x1, y, g_ffn_post[l], gt2, tm)
    return x
```
